```python
import jax, jax.numpy as jnp
from jax import lax
import numpy as np

D_MODEL = 4096
BATCH = 1
SEQ = 8192
DEPTH = 1

HEAD_DIM = 128
ATTN_GROUPS = ((128, 1), (512, 4), (2048, 16))
N_ATTN_GROUPS = len(ATTN_GROUPS)
HEADS_PER_GROUP = 4
N_Q_HEADS = N_ATTN_GROUPS * HEADS_PER_GROUP
N_KV_HEADS = HEADS_PER_GROUP
ROPE_THETA = 10000.0
BLOCK = 128
NEG_INF = -1e30
CONV_WIDTH = D_MODEL // 2
CONV_K = 3
PLE_DIM = 256
N_EXPERT_GROUPS = 4
EXPERTS_PER_GROUP = 8
N_EXPERTS = N_EXPERT_GROUPS * EXPERTS_PER_GROUP
TOP_K = 2
D_EXPERT = 1536
EPS = 1e-6

Q_COLS = N_Q_HEADS * HEAD_DIM
KV_COLS = N_KV_HEADS * HEAD_DIM
A_OUT = N_KV_HEADS * HEAD_DIM
IN_COLS = Q_COLS + 2 * KV_COLS + 3 * CONV_WIDTH + 2 * D_MODEL
SPLITS = tuple(int(c) for c in np.cumsum([Q_COLS, KV_COLS, KV_COLS, CONV_WIDTH, CONV_WIDTH, CONV_WIDTH, D_MODEL])[:])

kernel_name = "hybrid_dilated_attn_shortconv_hmoe"


def _rmsnorm(x, g):
    xf = x.astype(jnp.float32)
    y = xf * lax.rsqrt(jnp.mean(xf * xf, axis=-1, keepdims=True) + EPS)
    return (y * g.astype(jnp.float32)).astype(x.dtype)


def _rope(t, pos):
    half = HEAD_DIM // 2
    inv_freq = ROPE_THETA ** (-jnp.arange(half, dtype=jnp.float32) / half)
    ang = pos.astype(jnp.float32)[:, None] * inv_freq[None, :]
    cos = jnp.cos(ang)[None, :, None, :]
    sin = jnp.sin(ang)[None, :, None, :]
    tf = t.astype(jnp.float32)
    t1, t2 = tf[..., :half], tf[..., half:]
    return jnp.concatenate([t1 * cos - t2 * sin, t2 * cos + t1 * sin], axis=-1).astype(t.dtype)


def _dilated_window_attention(q, k, v, window, dilation):
    B, S, H, Dh = q.shape
    L = S // dilation
    W = window // dilation
    nprev = -(-W // BLOCK)
    Lp = -(-L // BLOCK) * BLOCK
    nb = Lp // BLOCK
    N = B * dilation

    def to_sub(t):
        t = t.reshape(B, L, dilation, H, Dh).transpose(0, 2, 1, 3, 4).reshape(N, L, H, Dh)
        return jnp.pad(t, ((0, 0), (0, Lp - L), (0, 0), (0, 0)))

    qs, ks, vs = to_sub(q), to_sub(k), to_sub(v)
    qb = qs.reshape(N, nb, BLOCK, H, Dh)

    def band(t):
        tp = jnp.pad(t, ((0, 0), (nprev * BLOCK, 0), (0, 0), (0, 0))).reshape(N, nb + nprev, BLOCK, H, Dh)
        return jnp.concatenate([tp[:, j:j + nb] for j in range(nprev + 1)], axis=2)

    kb, vb = band(ks), band(vs)
    KB = (nprev + 1) * BLOCK
    scores = jnp.einsum('nbqhd,nbkhd->nbhqk', qb, kb, preferred_element_type=jnp.float32) * (Dh ** -0.5)
    qi = jnp.arange(BLOCK)[:, None]
    kj = jnp.arange(KB)[None, :]
    dist = qi + nprev * BLOCK - kj
    kpos = jnp.arange(nb)[:, None, None] * BLOCK - nprev * BLOCK + kj[None]
    valid = (dist >= 0)[None] & (dist <= W)[None] & (kpos >= 0)
    scores = jnp.where(valid[None, :, None], scores, NEG_INF)
    m = jnp.max(scores, axis=-1, keepdims=True)
    e = jnp.exp(scores - m)
    s = jnp.sum(e, axis=-1, keepdims=True)
    probs = (e / s).astype(v.dtype)
    out = jnp.einsum('nbhqk,nbkhd->nbqhd', probs, vb)
    lse = (m + jnp.log(s))[..., 0].transpose(0, 1, 3, 2)
    out = out.reshape(N, Lp, H, Dh)[:, :L].reshape(B, dilation, L, H, Dh).transpose(0, 2, 1, 3, 4).reshape(B, S, H, Dh)
    lse = lse.reshape(N, Lp, H)[:, :L].reshape(B, dilation, L, H).transpose(0, 2, 1, 3).reshape(B, S, H)
    return out, lse


def _short_conv(u, w_conv):
    S = u.shape[1]
    up = jnp.pad(u, ((0, 0), (CONV_K - 1, 0), (0, 0)))
    return sum(w_conv[j] * up[:, j:j + S] for j in range(CONV_K))


def _hierarchical_moe(h, w_group, b_group, w_router, b_router, w_gate, w_up, w_down):
    B, S, D = h.shape
    T = B * S
    hf = h.reshape(T, D)
    g_logits = (hf @ w_group).astype(jnp.float32) + b_group.astype(jnp.float32)
    g_prob = jax.nn.softmax(g_logits, axis=-1)
    _, g_idx = lax.top_k(g_logits, 1)
    g_w = jnp.take_along_axis(g_prob, g_idx, axis=-1)
    e_all = ((hf @ w_router).astype(jnp.float32) + b_router.astype(jnp.float32)).reshape(T, N_EXPERT_GROUPS, EXPERTS_PER_GROUP)
    e_sel = jnp.take_along_axis(e_all, jnp.broadcast_to(g_idx[:, :, None], (T, 1, EXPERTS_PER_GROUP)), axis=1)[:, 0]
    e_prob = jax.nn.softmax(e_sel, axis=-1)
    top_p, top_i = lax.top_k(e_prob, TOP_K)
    top_p = top_p / jnp.sum(top_p, axis=-1, keepdims=True)
    weights = (g_w * top_p).reshape(-1)
    experts = (g_idx * EXPERTS_PER_GROUP + top_i).reshape(-1).astype(jnp.int32)
    tokens = jnp.repeat(jnp.arange(T, dtype=jnp.int32), TOP_K)
    A = T * TOP_K
    order = jnp.argsort(experts)
    se, st, sw = experts[order], tokens[order], weights[order]
    counts = jax.ops.segment_sum(jnp.ones((A,), jnp.int32), experts, num_segments=N_EXPERTS)
    starts = jnp.cumsum(counts) - counts
    padded = (counts + BLOCK - 1) // BLOCK * BLOCK
    pends = jnp.cumsum(padded)
    pstarts = pends - padded
    dest = pstarts[se] + (jnp.arange(A, dtype=jnp.int32) - starts[se])
    P = A + N_EXPERTS * BLOCK
    nblk = P // BLOCK
    row_tok = jnp.zeros((P,), jnp.int32).at[dest].set(st)
    row_w = jnp.zeros((P,), jnp.float32).at[dest].set(sw)
    blk_exp = jnp.minimum(jnp.searchsorted(pends, jnp.arange(nblk, dtype=jnp.int32) * BLOCK, side='right'), N_EXPERTS - 1)

    def expert_block(args):
        toks, wts, e = args
        xb = hf[toks]
        hid = jax.nn.silu(xb @ w_gate[e]) * (xb @ w_up[e])
        y = hid @ w_down[e]
        return y * wts[:, None].astype(y.dtype)

    yblk = lax.map(expert_block, (row_tok.reshape(nblk, BLOCK), row_w.reshape(nblk, BLOCK), blk_exp))
    out = jnp.zeros((T, D), yblk.dtype).at[row_tok].add(yblk.reshape(P, D))
    return out.reshape(B, S, D).astype(h.dtype)


def setup_inputs(seed: int = 0) -> dict:
    key = jax.random.key(seed)
    ks = jax.random.split(key, 24)
    f32 = jnp.float32

    def nrm(k, shape, fan_in):
        return jax.random.normal(k, shape, f32) * (fan_in ** -0.5)

    def gain(k, shape):
        return 1.0 + 0.02 * jax.random.normal(k, shape, f32)

    return {
        "x": jax.random.normal(ks[0], (BATCH, SEQ, D_MODEL), f32),
        "p": jax.random.normal(ks[1], (DEPTH, BATCH, SEQ, PLE_DIM), f32),
        "w_in": nrm(ks[2], (DEPTH, D_MODEL, IN_COLS), D_MODEL),
        "w_conv": nrm(ks[3], (DEPTH, CONV_K, CONV_WIDTH), CONV_K),
        "w_up_a": nrm(ks[4], (DEPTH, A_OUT, D_MODEL), A_OUT),
        "w_out_b": nrm(ks[5], (DEPTH, CONV_WIDTH, D_MODEL), CONV_WIDTH),
        "w_o": nrm(ks[6], (DEPTH, D_MODEL, D_MODEL), D_MODEL),
        "norm_mix": gain(ks[7], (DEPTH, D_MODEL)),
        "norm_ffn": gain(ks[8], (DEPTH, D_MODEL)),
        "w_group": nrm(ks[9], (DEPTH, D_MODEL, N_EXPERT_GROUPS), D_MODEL),
        "b_group": 0.01 * jax.random.normal(ks[10], (DEPTH, N_EXPERT_GROUPS), f32),
        "w_router": nrm(ks[11], (DEPTH, D_MODEL, N_EXPERTS), D_MODEL),
        "b_router": 0.01 * jax.random.normal(ks[12], (DEPTH, N_EXPERTS), f32),
        "w_gate": nrm(ks[13], (DEPTH, N_EXPERTS, D_MODEL, D_EXPERT), D_MODEL),
        "w_up": nrm(ks[14], (DEPTH, N_EXPERTS, D_MODEL, D_EXPERT), D_MODEL),
        "w_down": nrm(ks[15], (DEPTH, N_EXPERTS, D_EXPERT, D_MODEL), D_EXPERT),
        "norm_ple": gain(ks[16], (DEPTH, D_MODEL)),
        "w_ple_gate": nrm(ks[17], (DEPTH, D_MODEL, D_MODEL), D_MODEL),
        "w_ple_proj": nrm(ks[18], (DEPTH, PLE_DIM, D_MODEL), PLE_DIM),
        "norm_final": gain(ks[19], (D_MODEL,)),
    }


def reference(x, p, w_in, w_conv, w_up_a, w_out_b, w_o, norm_mix, norm_ffn, w_group, b_group,
              w_router, b_router, w_gate, w_up, w_down, norm_ple, w_ple_gate, w_ple_proj, norm_final):
    B, S, _ = x.shape
    pos = jnp.arange(S, dtype=jnp.int32)
    for i in range(DEPTH):
        h = _rmsnorm(x, norm_mix[i])
        proj = h @ w_in[i]
        q, k, v, u, gate_b, gate_c, merge_a, merge_b = jnp.split(proj, SPLITS, axis=-1)
        q = _rope(q.reshape(B, S, N_Q_HEADS, HEAD_DIM), pos)
        k = _rope(k.reshape(B, S, N_KV_HEADS, HEAD_DIM), pos)
        v = v.reshape(B, S, N_KV_HEADS, HEAD_DIM)
        outs, lses = [], []
        for g, (window, dilation) in enumerate(ATTN_GROUPS):
            q_g = q[:, :, g * HEADS_PER_GROUP:(g + 1) * HEADS_PER_GROUP]
            o_g, l_g = _dilated_window_attention(q_g, k, v, window, dilation)
            outs.append(o_g)
            lses.append(l_g)
        mix_w = jax.nn.softmax(jnp.stack(lses, axis=0), axis=0)
        y_a = jnp.sum(mix_w[..., None].astype(v.dtype) * jnp.stack(outs, axis=0), axis=0).reshape(B, S, A_OUT)
        y_b = gate_b * _short_conv(gate_c * u, w_conv[i])
        merged = jax.nn.sigmoid(merge_a) * (y_a @ w_up_a[i]) + jax.nn.sigmoid(merge_b) * (y_b @ w_out_b[i])
        x = x + merged @ w_o[i]
        h2 = _rmsnorm(x, norm_ffn[i])
        x = x + _hierarchical_moe(h2, w_group[i], b_group[i], w_router[i], b_router[i],
                                  w_gate[i], w_up[i], w_down[i])
        h3 = _rmsnorm(x, norm_ple[i])
        x = x + jax.nn.sigmoid(h3 @ w_ple_gate[i]) * (p[i] @ w_ple_proj[i])
    return _rmsnorm(x, norm_final)
```

```python
import functools

import jax
import jax.numpy as jnp
from jax import lax
from jax.experimental import pallas as pl
from jax.experimental.pallas import tpu as pltpu

F32 = jnp.float32
BF16 = jnp.bfloat16

HEAD_DIM = 128
ATTN_DILATIONS = (1, 4, 16)
HEADS_PER_GROUP = 4
ROPE_THETA = 10000.0
ATTN_BLOCK = 128
NEG_BIG = -1e30
N_EXPERT_GROUPS = 4
TOP_K = 2
EPS = 1e-6
ROW_BLOCK = 128
LANES = 128

VMEM_LIMIT_BYTES = 56 * 1024 * 1024


def _params(*sem):
    return pltpu.CompilerParams(dimension_semantics=sem, vmem_limit_bytes=VMEM_LIMIT_BYTES)


def _rmsnorm_kernel(x_ref, g_ref, o_ref):
    x = x_ref[...]
    ms = jnp.mean(x * x, axis=-1, keepdims=True)
    o_ref[...] = (x * lax.rsqrt(ms + EPS) * g_ref[...]).astype(o_ref.dtype)


def _rmsnorm(x, gain, out_dtype, tm):
    n, d = x.shape
    tm = min(tm, n)
    return pl.pallas_call(
        _rmsnorm_kernel,
        grid=(n // tm,),
        in_specs=[pl.BlockSpec((tm, d), lambda i: (i, 0)), pl.BlockSpec((1, d), lambda i: (0, 0))],
        out_specs=pl.BlockSpec((tm, d), lambda i: (i, 0)),
        out_shape=jax.ShapeDtypeStruct((n, d), out_dtype),
        compiler_params=_params("arbitrary"),
        name="rmsnorm",
    )(x, gain.reshape(1, d))


def _proj_kernel(h_ref, w_ref, c_ref, s_ref, o_ref, wb_ref, *, n_rope_tiles, tn):
    j = pl.program_id(0)
    i = pl.program_id(1)

    @pl.when(i == 0)
    def _():
        wb_ref[...] = w_ref[...].astype(BF16)

    acc = jnp.dot(h_ref[...], wb_ref[...], preferred_element_type=F32)

    @pl.when(j < n_rope_tiles)
    def _():
        c = c_ref[...]
        s = s_ref[...]
        for hh in range(tn // HEAD_DIM):
            sl = slice(hh * HEAD_DIM, (hh + 1) * HEAD_DIM)
            t = acc[:, sl]
            o_ref[:, sl] = (t * c + pltpu.roll(t, HEAD_DIM // 2, 1) * s).astype(o_ref.dtype)

    @pl.when(j >= n_rope_tiles)
    def _():
        o_ref[...] = acc.astype(o_ref.dtype)


def _proj(h, w, rope_c, rope_s, n_rope_cols, tm, tn):
    m, k = h.shape
    n = w.shape[1]
    tm = min(tm, m)
    assert n % tn == 0 and n_rope_cols % tn == 0 and m % tm == 0
    return pl.pallas_call(
        functools.partial(_proj_kernel, n_rope_tiles=n_rope_cols // tn, tn=tn),
        grid=(n // tn, m // tm),
        in_specs=[
            pl.BlockSpec((tm, k), lambda j, i: (i, 0)),
            pl.BlockSpec((k, tn), lambda j, i: (0, j)),
            pl.BlockSpec((tm, HEAD_DIM), lambda j, i: (i, 0)),
            pl.BlockSpec((tm, HEAD_DIM), lambda j, i: (i, 0)),
        ],
        out_specs=pl.BlockSpec((tm, tn), lambda j, i: (i, j)),
        out_shape=jax.ShapeDtypeStruct((m, n), BF16),
        scratch_shapes=[pltpu.VMEM((k, tn), BF16)],
        compiler_params=_params("arbitrary", "arbitrary"),
        name="in_proj",
    )(h, w, rope_c, rope_s)


def _attn_kernel(q_ref, kc_ref, kp_ref, vc_ref, vp_ref, o_ref, lse_ref, *, seg_blocks):
    g = pl.program_id(0)
    b = pl.program_id(1)
    seg = jnp.int32(seg_blocks[-1])
    for gi in range(len(seg_blocks) - 2, -1, -1):
        seg = jnp.where(g == gi, jnp.int32(seg_blocks[gi]), seg)
    has_prev = lax.rem(b, seg) != 0
    qi = lax.broadcasted_iota(jnp.int32, (ATTN_BLOCK, 2 * ATTN_BLOCK), 0)
    kj = lax.broadcasted_iota(jnp.int32, (ATTN_BLOCK, 2 * ATTN_BLOCK), 1)
    dist = qi + ATTN_BLOCK - kj
    valid = (dist >= 0) & (dist <= ATTN_BLOCK) & ((kj >= ATTN_BLOCK) | has_prev)
    scale = HEAD_DIM ** -0.5
    for hh in range(HEADS_PER_GROUP):
        sl = slice(hh * HEAD_DIM, (hh + 1) * HEAD_DIM)
        q = q_ref[:, sl]
        k = jnp.concatenate([kp_ref[:, sl], kc_ref[:, sl]], axis=0)
        v = jnp.concatenate([vp_ref[:, sl], vc_ref[:, sl]], axis=0)
        s = lax.dot_general(q, k, (((1,), (1,)), ((), ())), preferred_element_type=F32) * scale
        s = jnp.where(valid, s, NEG_BIG)
        m = jnp.max(s, axis=-1, keepdims=True)
        e = jnp.exp(s - m)
        ssum = jnp.sum(e, axis=-1, keepdims=True)
        p = (e / ssum).astype(BF16)
        o_ref[:, sl] = jnp.dot(p, v, preferred_element_type=F32)
        lse_ref[:, sl] = jnp.broadcast_to(m + jnp.log(ssum), (ATTN_BLOCK, HEAD_DIM))


def _attention(q3, k3, v3, seg_blocks):
    ng, s, c = q3.shape
    nb = s // ATTN_BLOCK
    cur = pl.BlockSpec((None, ATTN_BLOCK, c), lambda g, b: (g, b, 0))
    prev = pl.BlockSpec((None, ATTN_BLOCK, c), lambda g, b: (g, jnp.maximum(b - 1, 0), 0))
    return pl.pallas_call(
        functools.partial(_attn_kernel, seg_blocks=seg_blocks),
        grid=(ng, nb),
        in_specs=[cur, cur, prev, cur, prev],
        out_specs=[cur, cur],
        out_shape=[jax.ShapeDtypeStruct((ng, s, c), F32), jax.ShapeDtypeStruct((ng, s, c), F32)],
        compiler_params=_params("arbitrary", "arbitrary"),
        name="band_attention",
    )(q3, k3, k3, v3, v3)


def _attn_mix_kernel(o_ref, lse_ref, y_ref):
    lse = lse_ref[...]
    m = jnp.max(lse, axis=0)
    e = jnp.exp(lse - m[None])
    w = e / jnp.sum(e, axis=0)[None]
    y_ref[...] = jnp.sum(w * o_ref[...], axis=0).astype(y_ref.dtype)


def _attn_mix(o3, lse3, tm):
    ng, s, c = o3.shape
    tm = min(tm, s)
    spec = pl.BlockSpec((ng, tm, c), lambda i: (0, i, 0))
    return pl.pallas_call(
        _attn_mix_kernel,
        grid=(s // tm,),
        in_specs=[spec, spec],
        out_specs=pl.BlockSpec((tm, c), lambda i: (i, 0)),
        out_shape=jax.ShapeDtypeStruct((s, c), BF16),
        compiler_params=_params("arbitrary"),
        name="attn_group_mix",
    )(o3, lse3)


CONV_HALO = 16


def _conv_kernel(u_ref, gc_ref, gb_ref, up_ref, gcp_ref, w_ref, o_ref):
    i = pl.program_id(0)
    cur = u_ref[...].astype(F32) * gc_ref[...].astype(F32)
    prev = up_ref[...].astype(F32) * gcp_ref[...].astype(F32)
    prev = jnp.where(i == 0, 0.0, prev)
    big = jnp.concatenate([prev, cur], axis=0)
    back1 = pltpu.roll(big, 1, 0)[CONV_HALO:]
    back2 = pltpu.roll(big, 2, 0)[CONV_HALO:]
    w = w_ref[...]
    y = w[0:1] * back2 + w[1:2] * back1 + w[2:3] * cur
    o_ref[...] = (gb_ref[...].astype(F32) * y).astype(o_ref.dtype)


def _short_conv(proj, w_conv, col_u, col_gb, col_gc, tm, tc):
    s = proj.shape[0]
    kw, cw = w_conv.shape
    assert kw == 3
    tm = min(tm, s)
    tc = min(tc, cw)
    assert col_u % tc == 0 and col_gb % tc == 0 and col_gc % tc == 0 and tm % CONV_HALO == 0
    hb = tm // CONV_HALO

    def cur(col):
        return pl.BlockSpec((tm, tc), lambda i, j: (i, col // tc + j))

    def prev(col):
        return pl.BlockSpec((CONV_HALO, tc), lambda i, j: (jnp.maximum(i * hb - 1, 0), col // tc + j))

    return pl.pallas_call(
        _conv_kernel,
        grid=(s // tm, cw // tc),
        in_specs=[cur(col_u), cur(col_gc), cur(col_gb), prev(col_u), prev(col_gc),
                  pl.BlockSpec((kw, tc), lambda i, j: (0, j))],
        out_specs=pl.BlockSpec((tm, tc), lambda i, j: (i, j)),
        out_shape=jax.ShapeDtypeStruct((s, cw), BF16),
        compiler_params=_params("arbitrary", "arbitrary"),
        name="short_conv",
    )(proj, proj, proj, proj, proj, w_conv)


def _merge_kernel(ya_ref, yb_ref, wa_ref, wb_ref, ma_ref, mb_ref, o_ref, wab_ref, wbb_ref):
    i = pl.program_id(1)

    @pl.when(i == 0)
    def _():
        wab_ref[...] = wa_ref[...].astype(BF16)
        wbb_ref[...] = wb_ref[...].astype(BF16)

    a = jnp.dot(ya_ref[...], wab_ref[...], preferred_element_type=F32)
    b = jnp.dot(yb_ref[...], wbb_ref[...], preferred_element_type=F32)
    ga = jax.nn.sigmoid(ma_ref[...].astype(F32))
    gb = jax.nn.sigmoid(mb_ref[...].astype(F32))
    o_ref[...] = (ga * a + gb * b).astype(o_ref.dtype)


def _merge(y_a, y_b, proj, w_a, w_b, col_ma, col_mb, tm, tn):
    m, ka = y_a.shape
    kb = y_b.shape[1]
    n = w_a.shape[1]
    tm = min(tm, m)
    tn = min(tn, n)
    assert col_ma % tn == 0 and col_mb % tn == 0
    return pl.pallas_call(
        _merge_kernel,
        grid=(n // tn, m // tm),
        in_specs=[
            pl.BlockSpec((tm, ka), lambda j, i: (i, 0)),
            pl.BlockSpec((tm, kb), lambda j, i: (i, 0)),
            pl.BlockSpec((ka, tn), lambda j, i: (0, j)),
            pl.BlockSpec((kb, tn), lambda j, i: (0, j)),
            pl.BlockSpec((tm, tn), lambda j, i: (i, col_ma // tn + j)),
            pl.BlockSpec((tm, tn), lambda j, i: (i, col_mb // tn + j)),
        ],
        out_specs=pl.BlockSpec((tm, tn), lambda j, i: (i, j)),
        out_shape=jax.ShapeDtypeStruct((m, n), BF16),
        scratch_shapes=[pltpu.VMEM((ka, tn), BF16), pltpu.VMEM((kb, tn), BF16)],
        compiler_params=_params("arbitrary", "arbitrary"),
        name="mixer_merge",
    )(y_a, y_b, w_a, w_b, proj, proj)


def _resid_mm_kernel(a_ref, w_ref, x_ref, o_ref, wb_ref):
    i = pl.program_id(1)

    @pl.when(i == 0)
    def _():
        wb_ref[...] = w_ref[...].astype(BF16)

    o_ref[...] = x_ref[...] + jnp.dot(a_ref[...], wb_ref[...], preferred_element_type=F32)


def _resid_mm(a, w, x, tm, tn):
    m, k = a.shape
    n = w.shape[1]
    tm = min(tm, m)
    tn = min(tn, n)
    return pl.pallas_call(
        _resid_mm_kernel,
        grid=(n // tn, m // tm),
        in_specs=[
            pl.BlockSpec((tm, k), lambda j, i: (i, 0)),
            pl.BlockSpec((k, tn), lambda j, i: (0, j)),
            pl.BlockSpec((tm, tn), lambda j, i: (i, j)),
        ],
        out_specs=pl.BlockSpec((tm, tn), lambda j, i: (i, j)),
        out_shape=jax.ShapeDtypeStruct((m, n), F32),
        scratch_shapes=[pltpu.VMEM((k, tn), BF16)],
        compiler_params=_params("arbitrary", "arbitrary"),
        name="out_proj_residual",
    )(a, w, x)


def _split_bf16(a):
    hi = a.astype(BF16)
    lo = (a - hi.astype(F32)).astype(BF16)
    return hi, lo


def _router_kernel(x_ref, g_ref, w_ref, b_ref, h_ref, eid_ref, wt_ref, *, n_groups, n_experts):
    x = x_ref[...]
    ms = jnp.mean(x * x, axis=-1, keepdims=True)
    h = x * lax.rsqrt(ms + EPS) * g_ref[...]
    h_ref[...] = h
    hh, hl = _split_bf16(h)
    wh, wl = _split_bf16(w_ref[...])
    logits = (jnp.dot(hh, wh, preferred_element_type=F32) + jnp.dot(hh, wl, preferred_element_type=F32)
              + jnp.dot(hl, wh, preferred_element_type=F32)) + b_ref[...]
    tm = x.shape[0]
    per_group = n_experts // n_groups
    lane = lax.broadcasted_iota(jnp.int32, (tm, LANES), 1)
    is_g = lane < n_groups
    gl = jnp.where(is_g, logits, NEG_BIG)
    gmax = jnp.max(gl, axis=-1, keepdims=True)
    gidx = jnp.min(jnp.where(gl == gmax, lane, LANES), axis=-1, keepdims=True)
    gsum = jnp.sum(jnp.where(is_g, jnp.exp(gl - gmax), 0.0), axis=-1, keepdims=True)
    g_w = 1.0 / gsum
    lane_e = lane - n_groups
    sel = (lane_e >= 0) & (lane_e < n_experts) & (lane_e // per_group == gidx)
    el = jnp.where(sel, logits, NEG_BIG)
    emax = jnp.max(el, axis=-1, keepdims=True)
    i1 = jnp.min(jnp.where(el == emax, lane, LANES), axis=-1, keepdims=True)
    esum = jnp.sum(jnp.where(sel, jnp.exp(el - emax), 0.0), axis=-1, keepdims=True)
    el2 = jnp.where(lane == i1, NEG_BIG, el)
    emax2 = jnp.max(el2, axis=-1, keepdims=True)
    i2 = jnp.min(jnp.where(el2 == emax2, lane, LANES), axis=-1, keepdims=True)
    p1 = 1.0 / esum
    p2 = jnp.exp(emax2 - emax) / esum
    den = p1 + p2
    eid_ref[...] = jnp.where(lane == 0, i1 - n_groups, jnp.where(lane == 1, i2 - n_groups, 0))
    wt_ref[...] = jnp.where(lane == 0, g_w * (p1 / den), jnp.where(lane == 1, g_w * (p2 / den), 0.0))


def _router(x, gain, w_route, b_route, n_groups, n_experts, tm):
    t, d = x.shape
    tm = min(tm, t)
    row = pl.BlockSpec((tm, d), lambda i: (i, 0))
    narrow = pl.BlockSpec((tm, LANES), lambda i: (i, 0))
    return pl.pallas_call(
        functools.partial(_router_kernel, n_groups=n_groups, n_experts=n_experts),
        grid=(t // tm,),
        in_specs=[row, pl.BlockSpec((1, d), lambda i: (0, 0)), pl.BlockSpec((d, LANES), lambda i: (0, 0)),
                  pl.BlockSpec((1, LANES), lambda i: (0, 0))],
        out_specs=[row, narrow, narrow],
        out_shape=[jax.ShapeDtypeStruct((t, d), F32), jax.ShapeDtypeStruct((t, LANES), jnp.int32),
                   jax.ShapeDtypeStruct((t, LANES), F32)],
        compiler_params=_params("arbitrary"),
        name="ffn_norm_router",
    )(x, gain.reshape(1, d), w_route, b_route)


def _rank_kernel(eid_ref, rank_ref, cnt_ref, carry_ref):
    i = pl.program_id(0)

    @pl.when(i == 0)
    def _():
        carry_ref[...] = jnp.zeros_like(carry_ref)

    eid = eid_ref[...]
    tm = eid.shape[0]
    lane = lax.broadcasted_iota(jnp.int32, (tm, LANES), 1)
    oh0 = lane == eid[:, 0:1]
    oh1 = lane == eid[:, 1:2]
    ohs = oh0.astype(F32) + oh1.astype(F32)
    r = lax.broadcasted_iota(jnp.int32, (tm, tm), 0)
    c = lax.broadcasted_iota(jnp.int32, (tm, tm), 1)
    earlier = (c < r).astype(BF16)
    before = jnp.dot(earlier, ohs.astype(BF16), preferred_element_type=F32) + carry_ref[...]
    rank0 = jnp.sum(jnp.where(oh0, before, 0.0), axis=-1, keepdims=True)
    rank1 = jnp.sum(jnp.where(oh1, before, 0.0), axis=-1, keepdims=True)
    rank_ref[...] = jnp.where(lane == 0, rank0, jnp.where(lane == 1, rank1, 0.0)).astype(jnp.int32)
    total = carry_ref[...] + jnp.sum(ohs, axis=0, keepdims=True)
    carry_ref[...] = total
    cnt_ref[...] = jnp.broadcast_to(total, cnt_ref.shape).astype(jnp.int32)


def _rank(eid, tm):
    t = eid.shape[0]
    tm = min(tm, t)
    narrow = pl.BlockSpec((tm, LANES), lambda i: (i, 0))
    return pl.pallas_call(
        _rank_kernel,
        grid=(t // tm,),
        in_specs=[narrow],
        out_specs=[narrow, pl.BlockSpec((8, LANES), lambda i: (0, 0))],
        out_shape=[jax.ShapeDtypeStruct((t, LANES), jnp.int32), jax.ShapeDtypeStruct((8, LANES), jnp.int32)],
        scratch_shapes=[pltpu.VMEM((1, LANES), F32)],
        compiler_params=_params("arbitrary"),
        name="expert_rank",
    )(eid)


def _dispatch_kernel(zrow_ref, dest_ref, h_hbm, xs_hbm, zbuf_ref, sem, zsem, *, n_experts, tb):
    i = pl.program_id(0)

    def zero_copy(row):
        return pltpu.make_async_copy(zbuf_ref, xs_hbm.at[pl.ds(row, ROW_BLOCK)], zsem)

    @pl.when(i == 0)
    def _():
        zbuf_ref[...] = jnp.zeros_like(zbuf_ref)
        for e in range(n_experts):
            @pl.when(zrow_ref[e] >= 0)
            def _():
                zero_copy(pl.multiple_of(zrow_ref[e], ROW_BLOCK)).start()
        for e in range(n_experts):
            @pl.when(zrow_ref[e] >= 0)
            def _():
                zero_copy(0).wait()

    def row_copy(src, dst):
        return pltpu.make_async_copy(h_hbm.at[pl.ds(src, 1)], xs_hbm.at[pl.ds(dst, 1)], sem)

    base = i * tb

    def issue(r, carry):
        for k in range(TOP_K):
            row_copy(base + r, dest_ref[0, 0, TOP_K * r + k]).start()
        return carry

    lax.fori_loop(0, tb, issue, 0)

    def drain(r, carry):
        for k in range(TOP_K):
            row_copy(0, 0).wait()
        return carry

    lax.fori_loop(0, tb, drain, 0)


def _dispatch(h, dest, zrow, n_rows, tb):
    t, d = h.shape
    tb = min(tb, t)
    n_experts = zrow.shape[0]
    dest3 = dest.reshape(t // tb, 1, TOP_K * tb)
    grid_spec = pltpu.PrefetchScalarGridSpec(
        num_scalar_prefetch=1,
        grid=(t // tb,),
        in_specs=[
            pl.BlockSpec((1, 1, TOP_K * tb), lambda i, z: (i, 0, 0), memory_space=pltpu.SMEM),
            pl.BlockSpec(memory_space=pl.ANY),
        ],
        out_specs=pl.BlockSpec(memory_space=pl.ANY),
        scratch_shapes=[pltpu.VMEM((ROW_BLOCK, d), F32), pltpu.SemaphoreType.DMA, pltpu.SemaphoreType.DMA],
    )
    return pl.pallas_call(
        functools.partial(_dispatch_kernel, n_experts=n_experts, tb=tb),
        grid_spec=grid_spec,
        out_shape=jax.ShapeDtypeStruct((n_rows, d), F32),
        compiler_params=_params("arbitrary"),
        name="moe_dispatch",
    )(zrow, dest3, h)


def _ffn_up_kernel(te_ref, trb_ref, nsub_ref, ntot_ref, x_ref, wg_ref, wu_ref, hid_ref, wgb_ref, wub_ref, *, tm):
    t = pl.program_id(0)
    nsub = nsub_ref[t]

    @pl.when(nsub > 0)
    def _():
        wgb_ref[...] = wg_ref[...].astype(BF16)
        wub_ref[...] = wu_ref[...].astype(BF16)

        def body(s, carry):
            r = pl.multiple_of(s * ROW_BLOCK, ROW_BLOCK)
            xb = x_ref[pl.ds(r, ROW_BLOCK), :].astype(BF16)
            g = jnp.dot(xb, wgb_ref[...], preferred_element_type=F32)
            u = jnp.dot(xb, wub_ref[...], preferred_element_type=F32)
            hid_ref[pl.ds(r, ROW_BLOCK), :] = (g * jax.nn.sigmoid(g) * u).astype(hid_ref.dtype)
            return carry

        lax.fori_loop(0, nsub, body, 0)

        def clear(s, carry):
            r = pl.multiple_of(s * ROW_BLOCK, ROW_BLOCK)
            hid_ref[pl.ds(r, ROW_BLOCK), :] = jnp.zeros((ROW_BLOCK, hid_ref.shape[1]), hid_ref.dtype)
            return carry

        lax.fori_loop(nsub, tm // ROW_BLOCK, clear, 0)


def _ffn_up(xs, w_gate, w_up, meta, n_tiles, tm, tf):
    d = xs.shape[1]
    f = w_gate.shape[2]
    tf = min(tf, f)
    nc = f // tf

    def col(t, c, ntot):
        return jnp.where(t < ntot[0], c, nc - 1)

    grid_spec = pltpu.PrefetchScalarGridSpec(
        num_scalar_prefetch=4,
        grid=(n_tiles, nc),
        in_specs=[
            pl.BlockSpec((tm, d), lambda t, c, te, trb, ns, ntot: (trb[t], 0)),
            pl.BlockSpec((None, d, tf), lambda t, c, te, trb, ns, ntot: (te[t], 0, col(t, c, ntot))),
            pl.BlockSpec((None, d, tf), lambda t, c, te, trb, ns, ntot: (te[t], 0, col(t, c, ntot))),
        ],
        out_specs=pl.BlockSpec((tm, tf), lambda t, c, te, trb, ns, ntot: (trb[t], col(t, c, ntot))),
        scratch_shapes=[pltpu.VMEM((d, tf), BF16), pltpu.VMEM((d, tf), BF16)],
    )
    return pl.pallas_call(
        functools.partial(_ffn_up_kernel, tm=tm),
        grid_spec=grid_spec,
        out_shape=jax.ShapeDtypeStruct((xs.shape[0], f), BF16),
        compiler_params=_params("arbitrary", "arbitrary"),
        name="expert_gate_up",
    )(*meta, xs, w_gate, w_up)


def _ffn_down_kernel(te_ref, trb_ref, nsub_ref, ntot_ref, hid_ref, wd_ref, y_ref, wdb_ref, *, tm):
    t = pl.program_id(0)
    nsub = nsub_ref[t]

    @pl.when(nsub > 0)
    def _():
        wdb_ref[...] = wd_ref[...].astype(BF16)

        def body(s, carry):
            r = pl.multiple_of(s * ROW_BLOCK, ROW_BLOCK)
            y_ref[pl.ds(r, ROW_BLOCK), :] = jnp.dot(hid_ref[pl.ds(r, ROW_BLOCK), :], wdb_ref[...],
                                                    preferred_element_type=F32)
            return carry

        lax.fori_loop(0, nsub, body, 0)

        def clear(s, carry):
            r = pl.multiple_of(s * ROW_BLOCK, ROW_BLOCK)
            y_ref[pl.ds(r, ROW_BLOCK), :] = jnp.zeros((ROW_BLOCK, y_ref.shape[1]), y_ref.dtype)
            return carry

        lax.fori_loop(nsub, tm // ROW_BLOCK, clear, 0)


def _ffn_down(hid, w_down, meta, n_tiles, tm, tn):
    f = hid.shape[1]
    d = w_down.shape[2]
    tn = min(tn, d)
    nc = d // tn

    def col(t, c, ntot):
        return jnp.where(t < ntot[0], c, nc - 1)

    grid_spec = pltpu.PrefetchScalarGridSpec(
        num_scalar_prefetch=4,
        grid=(n_tiles, nc),
        in_specs=[
            pl.BlockSpec((tm, f), lambda t, c, te, trb, ns, ntot: (trb[t], 0)),
            pl.BlockSpec((None, f, tn), lambda t, c, te, trb, ns, ntot: (te[t], 0, col(t, c, ntot))),
        ],
        out_specs=pl.BlockSpec((tm, tn), lambda t, c, te, trb, ns, ntot: (trb[t], col(t, c, ntot))),
        scratch_shapes=[pltpu.VMEM((f, tn), BF16)],
    )
    return pl.pallas_call(
        functools.partial(_ffn_down_kernel, tm=tm),
        grid_spec=grid_spec,
        out_shape=jax.ShapeDtypeStruct((hid.shape[0], d), F32),
        compiler_params=_params("arbitrary", "arbitrary"),
        name="expert_down",
    )(*meta, hid, w_down)


def _combine_kernel(dest_ref, wt_ref, x_ref, g_ref, y_hbm, x2_ref, h3_ref, buf_ref, sem, *, tb):
    def row_copy(src, k, r):
        return pltpu.make_async_copy(y_hbm.at[pl.ds(src, 1)], buf_ref.at[k, pl.ds(r, 1)], sem)

    def issue(r, carry):
        for k in range(TOP_K):
            row_copy(dest_ref[0, 0, TOP_K * r + k], k, r).start()
        return carry

    lax.fori_loop(0, tb, issue, 0)

    def drain(r, carry):
        for k in range(TOP_K):
            row_copy(0, k, r).wait()
        return carry

    lax.fori_loop(0, tb, drain, 0)

    wt = wt_ref[...]
    x2 = x_ref[...] + (wt[:, 0:1] * buf_ref[0] + wt[:, 1:2] * buf_ref[1])
    x2_ref[...] = x2
    ms = jnp.mean(x2 * x2, axis=-1, keepdims=True)
    h3_ref[...] = (x2 * lax.rsqrt(ms + EPS) * g_ref[...]).astype(h3_ref.dtype)


def _combine(x, y, dest, wts, gain, tb):
    t, d = x.shape
    tb = min(tb, t)
    dest3 = dest.reshape(t // tb, 1, TOP_K * tb)
    row = pl.BlockSpec((tb, d), lambda i: (i, 0))
    return pl.pallas_call(
        functools.partial(_combine_kernel, tb=tb),
        grid=(t // tb,),
        in_specs=[
            pl.BlockSpec((1, 1, TOP_K * tb), lambda i: (i, 0, 0), memory_space=pltpu.SMEM),
            pl.BlockSpec((tb, LANES), lambda i: (i, 0)),
            row,
            pl.BlockSpec((1, d), lambda i: (0, 0)),
            pl.BlockSpec(memory_space=pl.ANY),
        ],
        out_specs=[row, row],
        out_shape=[jax.ShapeDtypeStruct((t, d), F32), jax.ShapeDtypeStruct((t, d), BF16)],
        scratch_shapes=[pltpu.VMEM((TOP_K, tb, d), F32), pltpu.SemaphoreType.DMA],
        compiler_params=_params("arbitrary"),
        name="moe_combine_norm",
    )(dest3, wts, x, gain.reshape(1, d), y)


def _ple_kernel(h_ref, p_ref, wg_ref, wp_ref, x_ref, o_ref, wgb_ref, wpb_ref):
    i = pl.program_id(1)

    @pl.when(i == 0)
    def _():
        wgb_ref[...] = wg_ref[...].astype(BF16)
        wpb_ref[...] = wp_ref[...].astype(BF16)

    a = jnp.dot(h_ref[...], wgb_ref[...], preferred_element_type=F32)
    b = jnp.dot(p_ref[...].astype(BF16), wpb_ref[...], preferred_element_type=F32)
    o_ref[...] = x_ref[...] + jax.nn.sigmoid(a) * b


def _ple(h, p, w_gate, w_proj, x, tm, tn):
    m, k = h.shape
    kp = p.shape[1]
    n = w_gate.shape[1]
    tm = min(tm, m)
    tn = min(tn, n)
    return pl.pallas_call(
        _ple_kernel,
        grid=(n // tn, m // tm),
        in_specs=[
            pl.BlockSpec((tm, k), lambda j, i: (i, 0)),
            pl.BlockSpec((tm, kp), lambda j, i: (i, 0)),
            pl.BlockSpec((k, tn), lambda j, i: (0, j)),
            pl.BlockSpec((kp, tn), lambda j, i: (0, j)),
            pl.BlockSpec((tm, tn), lambda j, i: (i, j)),
        ],
        out_specs=pl.BlockSpec((tm, tn), lambda j, i: (i, j)),
        out_shape=jax.ShapeDtypeStruct((m, n), F32),
        scratch_shapes=[pltpu.VMEM((k, tn), BF16), pltpu.VMEM((kp, tn), BF16)],
        compiler_params=_params("arbitrary", "arbitrary"),
        name="ple_gate_residual",
    )(h, p, w_gate, w_proj, x)


EXPERT_TILE_ROWS = 768


def _to_residue_major(t, d):
    if d == 1:
        return t
    s, c = t.shape
    return t.reshape(s // d, d, c).transpose(1, 0, 2).reshape(s, c)


def _from_residue_major(t, d):
    if d == 1:
        return t
    s, c = t.shape
    return t.reshape(d, s // d, c).transpose(1, 0, 2).reshape(s, c)


def _rope_tables(s):
    half = HEAD_DIM // 2
    inv_freq = ROPE_THETA ** (-jnp.arange(half, dtype=F32) / half)
    ang = jnp.arange(s, dtype=jnp.int32).astype(F32)[:, None] * inv_freq[None, :]
    cos, sin = jnp.cos(ang), jnp.sin(ang)
    return jnp.concatenate([cos, cos], axis=-1), jnp.concatenate([-sin, sin], axis=-1)


def _expert_tiles(counts, n_assign, tm):
    n_experts = counts.shape[0]
    n_tiles = -(-n_assign // tm) + n_experts
    tiles_e = (counts + tm - 1) // tm
    tcum = jnp.cumsum(tiles_e)
    tstart = tcum - tiles_e
    pstart = tstart * tm
    total = tcum[-1]
    t = jnp.arange(n_tiles, dtype=jnp.int32)
    tc = jnp.minimum(t, total - 1)
    te = jnp.minimum(jnp.searchsorted(tcum, tc, side="right"), n_experts - 1).astype(jnp.int32)
    local = tc - tstart[te]
    rows_left = counts[te] - local * tm
    nsub = jnp.clip((rows_left + ROW_BLOCK - 1) // ROW_BLOCK, 0, tm // ROW_BLOCK)
    nsub = jnp.where(t < total, nsub, 0).astype(jnp.int32)
    trb = (tstart[te] + local).astype(jnp.int32)
    zrow = jnp.where(counts > 0, pstart + (counts - 1) // ROW_BLOCK * ROW_BLOCK, -1).astype(jnp.int32)
    meta = (te, trb, nsub, jnp.reshape(total, (1,)).astype(jnp.int32))
    return pstart.astype(jnp.int32), zrow, meta, n_tiles


def _layer(x, p, w_in, w_conv, w_up_a, w_out_b, w_o, g_mix, g_ffn, w_group, b_group, w_router, b_router,
           w_gate, w_up, w_down, g_ple, w_ple_gate, w_ple_proj):
    s, d = x.shape
    cw = w_conv.shape[1]
    n_experts = w_gate.shape[0]
    ng = len(ATTN_DILATIONS)
    a_out = HEADS_PER_GROUP * HEAD_DIM
    q_cols = ng * a_out
    col_k, col_v = q_cols, q_cols + a_out
    col_u = q_cols + 2 * a_out
    col_gb, col_gc = col_u + cw, col_u + 2 * cw
    col_ma = col_u + 3 * cw
    col_mb = col_ma + d

    h = _rmsnorm(x, g_mix, BF16, 256)
    rope_c, rope_s = _rope_tables(s)
    proj = _proj(h, w_in, rope_c, rope_s, q_cols + a_out, 1024, 512)

    k = proj[:, col_k:col_k + a_out]
    v = proj[:, col_v:col_v + a_out]
    q3 = jnp.stack([_to_residue_major(proj[:, g * a_out:(g + 1) * a_out], dl) for g, dl in enumerate(ATTN_DILATIONS)])
    k3 = jnp.stack([_to_residue_major(k, dl) for dl in ATTN_DILATIONS])
    v3 = jnp.stack([_to_residue_major(v, dl) for dl in ATTN_DILATIONS])
    seg_blocks = tuple(s // dl // ATTN_BLOCK for dl in ATTN_DILATIONS)
    o3, lse3 = _attention(q3, k3, v3, seg_blocks)
    o3 = jnp.stack([_from_residue_major(o3[g], dl) for g, dl in enumerate(ATTN_DILATIONS)])
    lse3 = jnp.stack([_from_residue_major(lse3[g], dl) for g, dl in enumerate(ATTN_DILATIONS)])
    y_a = _attn_mix(o3, lse3, 512)

    y_b = _short_conv(proj, w_conv, col_u, col_gb, col_gc, 512, 512)
    merged = _merge(y_a, y_b, proj, w_up_a, w_out_b, col_ma, col_mb, 1024, 512)
    x1 = _resid_mm(merged, w_o, x, 1024, 512)

    w_route = jnp.zeros((d, LANES), F32).at[:, :N_EXPERT_GROUPS].set(w_group)
    w_route = w_route.at[:, N_EXPERT_GROUPS:N_EXPERT_GROUPS + n_experts].set(w_router)
    b_route = jnp.zeros((1, LANES), F32).at[0, :N_EXPERT_GROUPS].set(b_group)
    b_route = b_route.at[0, N_EXPERT_GROUPS:N_EXPERT_GROUPS + n_experts].set(b_router)
    h2, eid, wts = _router(x1, g_ffn, w_route, b_route, N_EXPERT_GROUPS, n_experts, 256)
    rank, cnt = _rank(eid, 256)
    counts = cnt[0, :n_experts]
    tm_e = EXPERT_TILE_ROWS
    pstart, zrow, meta, n_tiles = _expert_tiles(counts, s * TOP_K, tm_e)
    dest = (pstart[eid[:, :TOP_K]] + rank[:, :TOP_K]).reshape(-1)
    xs = _dispatch(h2, dest, zrow, n_tiles * tm_e, 512)
    hid = _ffn_up(xs, w_gate, w_up, meta, n_tiles, tm_e, 256)
    y = _ffn_down(hid, w_down, meta, n_tiles, tm_e, 1024)
    x2, h3 = _combine(x1, y, dest, wts, g_ple, 256)

    return _ple(h3, p, w_ple_gate, w_ple_proj, x2, 1024, 512)


def kernel(x, p, w_in, w_conv, w_up_a, w_out_b, w_o, norm_mix, norm_ffn, w_group, b_group, w_router, b_router,
           w_gate, w_up, w_down, norm_ple, w_ple_gate, w_ple_proj, norm_final):
    b, s, d = x.shape
    depth = w_in.shape[0]
    outs = []
    for bi in range(b):
        xb = x[bi]
        for i in range(depth):
            xb = _layer(xb, p[i, bi], w_in[i], w_conv[i], w_up_a[i], w_out_b[i], w_o[i], norm_mix[i], norm_ffn[i],
                        w_group[i], b_group[i], w_router[i], b_router[i], w_gate[i], w_up[i], w_down[i],
                        norm_ple[i], w_ple_gate[i], w_ple_proj[i])
        outs.append(_rmsnorm(xb, norm_final, x.dtype, 256))
    return jnp.stack(outs)
```

```python
import functools

import jax
import jax.numpy as jnp
from jax import lax
from jax.experimental import pallas as pl
from jax.experimental.pallas import tpu as pltpu

F32 = jnp.float32
BF16 = jnp.bfloat16

HEAD_DIM = 128
ATTN_DILATIONS = (1, 4, 16)
HEADS_PER_GROUP = 4
ROPE_THETA = 10000.0
ATTN_BLOCK = 128
NEG_BIG = -1e30
N_EXPERT_GROUPS = 4
TOP_K = 2
EPS = 1e-6
ROW_BLOCK = 128
LANES = 128

VMEM_LIMIT_BYTES = 56 * 1024 * 1024


def _params(*sem):
    return pltpu.CompilerParams(dimension_semantics=sem, vmem_limit_bytes=VMEM_LIMIT_BYTES)


def _rmsnorm_kernel(x_ref, g_ref, o_ref):
    x = x_ref[...]
    ms = jnp.mean(x * x, axis=-1, keepdims=True)
    o_ref[...] = (x * lax.rsqrt(ms + EPS) * g_ref[...]).astype(o_ref.dtype)


def _rmsnorm(x, gain, out_dtype, tm):
    n, d = x.shape
    tm = min(tm, n)
    return pl.pallas_call(
        _rmsnorm_kernel,
        grid=(n // tm,),
        in_specs=[pl.BlockSpec((tm, d), lambda i: (i, 0)), pl.BlockSpec((1, d), lambda i: (0, 0))],
        out_specs=pl.BlockSpec((tm, d), lambda i: (i, 0)),
        out_shape=jax.ShapeDtypeStruct((n, d), out_dtype),
        compiler_params=_params("arbitrary"),
        name="rmsnorm",
    )(x, gain.reshape(1, d))


def _proj_kernel(h_ref, w_ref, c_ref, s_ref, o_ref, wb_ref, *, n_rope_tiles, tn):
    j = pl.program_id(0)
    i = pl.program_id(1)

    @pl.when(i == 0)
    def _():
        wb_ref[...] = w_ref[...].astype(BF16)

    acc = jnp.dot(h_ref[...], wb_ref[...], preferred_element_type=F32)

    @pl.when(j < n_rope_tiles)
    def _():
        c = c_ref[...]
        s = s_ref[...]
        for hh in range(tn // HEAD_DIM):
            sl = slice(hh * HEAD_DIM, (hh + 1) * HEAD_DIM)
            t = acc[:, sl]
            o_ref[:, sl] = (t * c + pltpu.roll(t, HEAD_DIM // 2, 1) * s).astype(o_ref.dtype)

    @pl.when(j >= n_rope_tiles)
    def _():
        o_ref[...] = acc.astype(o_ref.dtype)


def _proj(h, w, rope_c, rope_s, n_rope_cols, tm, tn):
    m, k = h.shape
    n = w.shape[1]
    tm = min(tm, m)
    assert n % tn == 0 and n_rope_cols % tn == 0 and m % tm == 0
    return pl.pallas_call(
        functools.partial(_proj_kernel, n_rope_tiles=n_rope_cols // tn, tn=tn),
        grid=(n // tn, m // tm),
        in_specs=[
            pl.BlockSpec((tm, k), lambda j, i: (i, 0)),
            pl.BlockSpec((k, tn), lambda j, i: (0, j)),
            pl.BlockSpec((tm, HEAD_DIM), lambda j, i: (i, 0)),
            pl.BlockSpec((tm, HEAD_DIM), lambda j, i: (i, 0)),
        ],
        out_specs=pl.BlockSpec((tm, tn), lambda j, i: (i, j)),
        out_shape=jax.ShapeDtypeStruct((m, n), BF16),
        scratch_shapes=[pltpu.VMEM((k, tn), BF16)],
        compiler_params=_params("arbitrary", "arbitrary"),
        name="in_proj",
    )(h, w, rope_c, rope_s)


def _attn_kernel(q_ref, kc_ref, kp_ref, vc_ref, vp_ref, o_ref, lse_ref, *, seg_blocks):
    g = pl.program_id(0)
    b = pl.program_id(1)
    seg = jnp.int32(seg_blocks[-1])
    for gi in range(len(seg_blocks) - 2, -1, -1):
        seg = jnp.where(g == gi, jnp.int32(seg_blocks[gi]), seg)
    has_prev = lax.rem(b, seg) != 0
    qi = lax.broadcasted_iota(jnp.int32, (ATTN_BLOCK, 2 * ATTN_BLOCK), 0)
    kj = lax.broadcasted_iota(jnp.int32, (ATTN_BLOCK, 2 * ATTN_BLOCK), 1)
    dist = qi + ATTN_BLOCK - kj
    valid = (dist >= 0) & (dist <= ATTN_BLOCK) & ((kj >= ATTN_BLOCK) | has_prev)
    scale = HEAD_DIM ** -0.5
    for hh in range(HEADS_PER_GROUP):
        sl = slice(hh * HEAD_DIM, (hh + 1) * HEAD_DIM)
        q = q_ref[:, sl]
        k = jnp.concatenate([kp_ref[:, sl], kc_ref[:, sl]], axis=0)
        v = jnp.concatenate([vp_ref[:, sl], vc_ref[:, sl]], axis=0)
        s = lax.dot_general(q, k, (((1,), (1,)), ((), ())), preferred_element_type=F32) * scale
        s = jnp.where(valid, s, NEG_BIG)
        m = jnp.max(s, axis=-1, keepdims=True)
        e = jnp.exp(s - m)
        ssum = jnp.sum(e, axis=-1, keepdims=True)
        p = (e / ssum).astype(BF16)
        o_ref[:, sl] = jnp.dot(p, v, preferred_element_type=F32)
        lse_ref[:, sl] = jnp.broadcast_to(m + jnp.log(ssum), (ATTN_BLOCK, HEAD_DIM))


def _attention(q3, k3, v3, seg_blocks):
    ng, s, c = q3.shape
    nb = s // ATTN_BLOCK
    cur = pl.BlockSpec((None, ATTN_BLOCK, c), lambda g, b: (g, b, 0))
    prev = pl.BlockSpec((None, ATTN_BLOCK, c), lambda g, b: (g, jnp.maximum(b - 1, 0), 0))
    return pl.pallas_call(
        functools.partial(_attn_kernel, seg_blocks=seg_blocks),
        grid=(ng, nb),
        in_specs=[cur, cur, prev, cur, prev],
        out_specs=[cur, cur],
        out_shape=[jax.ShapeDtypeStruct((ng, s, c), F32), jax.ShapeDtypeStruct((ng, s, c), F32)],
        compiler_params=_params("arbitrary", "arbitrary"),
        name="band_attention",
    )(q3, k3, k3, v3, v3)


def _attn_mix_kernel(o_ref, lse_ref, y_ref):
    lse = lse_ref[...]
    m = jnp.max(lse, axis=0)
    e = jnp.exp(lse - m[None])
    w = e / jnp.sum(e, axis=0)[None]
    y_ref[...] = jnp.sum(w * o_ref[...], axis=0).astype(y_ref.dtype)


def _attn_mix(o3, lse3, tm):
    ng, s, c = o3.shape
    tm = min(tm, s)
    spec = pl.BlockSpec((ng, tm, c), lambda i: (0, i, 0))
    return pl.pallas_call(
        _attn_mix_kernel,
        grid=(s // tm,),
        in_specs=[spec, spec],
        out_specs=pl.BlockSpec((tm, c), lambda i: (i, 0)),
        out_shape=jax.ShapeDtypeStruct((s, c), BF16),
        compiler_params=_params("arbitrary"),
        name="attn_group_mix",
    )(o3, lse3)


CONV_HALO = 16


def _conv_kernel(u_ref, gc_ref, gb_ref, up_ref, gcp_ref, w_ref, o_ref):
    i = pl.program_id(0)
    cur = u_ref[...].astype(F32) * gc_ref[...].astype(F32)
    prev = up_ref[...].astype(F32) * gcp_ref[...].astype(F32)
    prev = jnp.where(i == 0, 0.0, prev)
    big = jnp.concatenate([prev, cur], axis=0)
    back1 = pltpu.roll(big, 1, 0)[CONV_HALO:]
    back2 = pltpu.roll(big, 2, 0)[CONV_HALO:]
    w = w_ref[...]
    y = w[0:1] * back2 + w[1:2] * back1 + w[2:3] * cur
    o_ref[...] = (gb_ref[...].astype(F32) * y).astype(o_ref.dtype)


def _short_conv(proj, w_conv, col_u, col_gb, col_gc, tm, tc):
    s = proj.shape[0]
    kw, cw = w_conv.shape
    assert kw == 3
    tm = min(tm, s)
    tc = min(tc, cw)
    assert col_u % tc == 0 and col_gb % tc == 0 and col_gc % tc == 0 and tm % CONV_HALO == 0
    hb = tm // CONV_HALO

    def cur(col):
        return pl.BlockSpec((tm, tc), lambda i, j: (i, col // tc + j))

    def prev(col):
        return pl.BlockSpec((CONV_HALO, tc), lambda i, j: (jnp.maximum(i * hb - 1, 0), col // tc + j))

    return pl.pallas_call(
        _conv_kernel,
        grid=(s // tm, cw // tc),
        in_specs=[cur(col_u), cur(col_gc), cur(col_gb), prev(col_u), prev(col_gc),
                  pl.BlockSpec((kw, tc), lambda i, j: (0, j))],
        out_specs=pl.BlockSpec((tm, tc), lambda i, j: (i, j)),
        out_shape=jax.ShapeDtypeStruct((s, cw), BF16),
        compiler_params=_params("arbitrary", "arbitrary"),
        name="short_conv",
    )(proj, proj, proj, proj, proj, w_conv)


def _merge_kernel(ya_ref, yb_ref, wa_ref, wb_ref, ma_ref, mb_ref, o_ref, wab_ref, wbb_ref):
    i = pl.program_id(1)

    @pl.when(i == 0)
    def _():
        wab_ref[...] = wa_ref[...].astype(BF16)
        wbb_ref[...] = wb_ref[...].astype(BF16)

    a = jnp.dot(ya_ref[...], wab_ref[...], preferred_element_type=F32)
    b = jnp.dot(yb_ref[...], wbb_ref[...], preferred_element_type=F32)
    ga = jax.nn.sigmoid(ma_ref[...].astype(F32))
    gb = jax.nn.sigmoid(mb_ref[...].astype(F32))
    o_ref[...] = (ga * a + gb * b).astype(o_ref.dtype)


def _merge(y_a, y_b, proj, w_a, w_b, col_ma, col_mb, tm, tn):
    m, ka = y_a.shape
    kb = y_b.shape[1]
    n = w_a.shape[1]
    tm = min(tm, m)
    tn = min(tn, n)
    assert col_ma % tn == 0 and col_mb % tn == 0
    return pl.pallas_call(
        _merge_kernel,
        grid=(n // tn, m // tm),
        in_specs=[
            pl.BlockSpec((tm, ka), lambda j, i: (i, 0)),
            pl.BlockSpec((tm, kb), lambda j, i: (i, 0)),
            pl.BlockSpec((ka, tn), lambda j, i: (0, j)),
            pl.BlockSpec((kb, tn), lambda j, i: (0, j)),
            pl.BlockSpec((tm, tn), lambda j, i: (i, col_ma // tn + j)),
            pl.BlockSpec((tm, tn), lambda j, i: (i, col_mb // tn + j)),
        ],
        out_specs=pl.BlockSpec((tm, tn), lambda j, i: (i, j)),
        out_shape=jax.ShapeDtypeStruct((m, n), BF16),
        scratch_shapes=[pltpu.VMEM((ka, tn), BF16), pltpu.VMEM((kb, tn), BF16)],
        compiler_params=_params("arbitrary", "arbitrary"),
        name="mixer_merge",
    )(y_a, y_b, w_a, w_b, proj, proj)


def _resid_mm_kernel(a_ref, w_ref, x_ref, o_ref, wb_ref):
    i = pl.program_id(1)

    @pl.when(i == 0)
    def _():
        wb_ref[...] = w_ref[...].astype(BF16)

    o_ref[...] = x_ref[...] + jnp.dot(a_ref[...], wb_ref[...], preferred_element_type=F32)


def _resid_mm(a, w, x, tm, tn):
    m, k = a.shape
    n = w.shape[1]
    tm = min(tm, m)
    tn = min(tn, n)
    return pl.pallas_call(
        _resid_mm_kernel,
        grid=(n // tn, m // tm),
        in_specs=[
            pl.BlockSpec((tm, k), lambda j, i: (i, 0)),
            pl.BlockSpec((k, tn), lambda j, i: (0, j)),
            pl.BlockSpec((tm, tn), lambda j, i: (i, j)),
        ],
        out_specs=pl.BlockSpec((tm, tn), lambda j, i: (i, j)),
        out_shape=jax.ShapeDtypeStruct((m, n), F32),
        scratch_shapes=[pltpu.VMEM((k, tn), BF16)],
        compiler_params=_params("arbitrary", "arbitrary"),
        name="out_proj_residual",
    )(a, w, x)


def _split_bf16(a):
    hi = a.astype(BF16)
    lo = (a - hi.astype(F32)).astype(BF16)
    return hi, lo


def _router_kernel(x_ref, g_ref, w_ref, b_ref, h_ref, eid_ref, wt_ref, *, n_groups, n_experts):
    x = x_ref[...]
    ms = jnp.mean(x * x, axis=-1, keepdims=True)
    h = x * lax.rsqrt(ms + EPS) * g_ref[...]
    h_ref[...] = h
    hh, hl = _split_bf16(h)
    wh, wl = _split_bf16(w_ref[...])
    logits = (jnp.dot(hh, wh, preferred_element_type=F32) + jnp.dot(hh, wl, preferred_element_type=F32)
              + jnp.dot(hl, wh, preferred_element_type=F32)) + b_ref[...]
    tm = x.shape[0]
    per_group = n_experts // n_groups
    lane = lax.broadcasted_iota(jnp.int32, (tm, LANES), 1)
    is_g = lane < n_groups
    gl = jnp.where(is_g, logits, NEG_BIG)
    gmax = jnp.max(gl, axis=-1, keepdims=True)
    gidx = jnp.min(jnp.where(gl == gmax, lane, LANES), axis=-1, keepdims=True)
    gsum = jnp.sum(jnp.where(is_g, jnp.exp(gl - gmax), 0.0), axis=-1, keepdims=True)
    g_w = 1.0 / gsum
    lane_e = lane - n_groups
    sel = (lane_e >= 0) & (lane_e < n_experts) & (lane_e // per_group == gidx)
    el = jnp.where(sel, logits, NEG_BIG)
    emax = jnp.max(el, axis=-1, keepdims=True)
    i1 = jnp.min(jnp.where(el == emax, lane, LANES), axis=-1, keepdims=True)
    esum = jnp.sum(jnp.where(sel, jnp.exp(el - emax), 0.0), axis=-1, keepdims=True)
    el2 = jnp.where(lane == i1, NEG_BIG, el)
    emax2 = jnp.max(el2, axis=-1, keepdims=True)
    i2 = jnp.min(jnp.where(el2 == emax2, lane, LANES), axis=-1, keepdims=True)
    p1 = 1.0 / esum
    p2 = jnp.exp(emax2 - emax) / esum
    den = p1 + p2
    eid_ref[...] = jnp.where(lane == 0, i1 - n_groups, jnp.where(lane == 1, i2 - n_groups, 0))
    wt_ref[...] = jnp.where(lane == 0, g_w * (p1 / den), jnp.where(lane == 1, g_w * (p2 / den), 0.0))


def _router(x, gain, w_route, b_route, n_groups, n_experts, tm):
    t, d = x.shape
    tm = min(tm, t)
    row = pl.BlockSpec((tm, d), lambda i: (i, 0))
    narrow = pl.BlockSpec((tm, LANES), lambda i: (i, 0))
    return pl.pallas_call(
        functools.partial(_router_kernel, n_groups=n_groups, n_experts=n_experts),
        grid=(t // tm,),
        in_specs=[row, pl.BlockSpec((1, d), lambda i: (0, 0)), pl.BlockSpec((d, LANES), lambda i: (0, 0)),
                  pl.BlockSpec((1, LANES), lambda i: (0, 0))],
        out_specs=[row, narrow, narrow],
        out_shape=[jax.ShapeDtypeStruct((t, d), F32), jax.ShapeDtypeStruct((t, LANES), jnp.int32),
                   jax.ShapeDtypeStruct((t, LANES), F32)],
        compiler_params=_params("arbitrary"),
        name="ffn_norm_router",
    )(x, gain.reshape(1, d), w_route, b_route)


def _rank_kernel(eid_ref, rank_ref, cnt_ref, carry_ref):
    i = pl.program_id(0)

    @pl.when(i == 0)
    def _():
        carry_ref[...] = jnp.zeros_like(carry_ref)

    eid = eid_ref[...]
    tm = eid.shape[0]
    lane = lax.broadcasted_iota(jnp.int32, (tm, LANES), 1)
    oh0 = lane == eid[:, 0:1]
    oh1 = lane == eid[:, 1:2]
    ohs = oh0.astype(F32) + oh1.astype(F32)
    r = lax.broadcasted_iota(jnp.int32, (tm, tm), 0)
    c = lax.broadcasted_iota(jnp.int32, (tm, tm), 1)
    earlier = (c < r).astype(BF16)
    before = jnp.dot(earlier, ohs.astype(BF16), preferred_element_type=F32) + carry_ref[...]
    rank0 = jnp.sum(jnp.where(oh0, before, 0.0), axis=-1, keepdims=True)
    rank1 = jnp.sum(jnp.where(oh1, before, 0.0), axis=-1, keepdims=True)
    rank_ref[...] = jnp.where(lane == 0, rank0, jnp.where(lane == 1, rank1, 0.0)).astype(jnp.int32)
    total = carry_ref[...] + jnp.sum(ohs, axis=0, keepdims=True)
    carry_ref[...] = total
    cnt_ref[...] = jnp.broadcast_to(total, cnt_ref.shape).astype(jnp.int32)


def _rank(eid, tm):
    t = eid.shape[0]
    tm = min(tm, t)
    narrow = pl.BlockSpec((tm, LANES), lambda i: (i, 0))
    return pl.pallas_call(
        _rank_kernel,
        grid=(t // tm,),
        in_specs=[narrow],
        out_specs=[narrow, pl.BlockSpec((8, LANES), lambda i: (0, 0))],
        out_shape=[jax.ShapeDtypeStruct((t, LANES), jnp.int32), jax.ShapeDtypeStruct((8, LANES), jnp.int32)],
        scratch_shapes=[pltpu.VMEM((1, LANES), F32)],
        compiler_params=_params("arbitrary"),
        name="expert_rank",
    )(eid)


GATHER_UNROLL = 8


def _for_row_blocks(nsub, fn):
    pairs = lax.shift_right_logical(nsub, 1)

    def body(i, carry):
        fn(pl.multiple_of(i * (2 * ROW_BLOCK), 2 * ROW_BLOCK), 2 * ROW_BLOCK)
        return carry

    lax.fori_loop(0, pairs, body, 0)

    @pl.when((nsub & 1) == 1)
    def _():
        fn(pl.multiple_of((nsub - 1) * ROW_BLOCK, ROW_BLOCK), ROW_BLOCK)


def _expert_ffn_kernel(te_ref, trb_ref, nsub_ref, ntot_ref, tok_ref, h_hbm, wg_ref, wu_ref, wd_ref, y_ref,
                       xg_ref, wgb_ref, wub_ref, wdb_ref, hid_ref, sem, *, tm, n_up):
    t = pl.program_id(0)
    s = pl.program_id(1)
    nsub = nsub_ref[t]

    def row_copy(tok, r):
        return pltpu.make_async_copy(h_hbm.at[pl.ds(tok, 1)], xg_ref.at[pl.ds(r, 1)], sem)

    @pl.when(nsub > 0)
    def _():
        @pl.when(s == 0)
        def _():
            n_iter = nsub * (ROW_BLOCK // GATHER_UNROLL)

            def issue(i, carry):
                for u in range(GATHER_UNROLL):
                    r = i * GATHER_UNROLL + u
                    row_copy(tok_ref[0, 0, r], r).start()
                return carry

            lax.fori_loop(0, n_iter, issue, 0)

            def drain(i, carry):
                for u in range(GATHER_UNROLL):
                    row_copy(0, 0).wait()
                return carry

            lax.fori_loop(0, n_iter, drain, 0)

        @pl.when(s < n_up)
        def _():
            wgb_ref[...] = wg_ref[...].astype(BF16)
            wub_ref[...] = wu_ref[...].astype(BF16)

            def up(r, rows):
                xb = xg_ref[pl.ds(r, rows), :].astype(BF16)
                g = jnp.dot(xb, wgb_ref[...], preferred_element_type=F32)
                u = jnp.dot(xb, wub_ref[...], preferred_element_type=F32)
                hid_ref[s, pl.ds(r, rows), :] = (g * jax.nn.sigmoid(g) * u).astype(hid_ref.dtype)

            _for_row_blocks(nsub, up)

        @pl.when(s >= n_up)
        def _():
            wdb_ref[...] = wd_ref[...].astype(BF16)

            def down(r, rows):
                hid = jnp.concatenate([hid_ref[c, pl.ds(r, rows), :] for c in range(n_up)], axis=1)
                y_ref[pl.ds(r, rows), :] = jnp.dot(hid, wdb_ref[...], preferred_element_type=F32)

            _for_row_blocks(nsub, down)

            def clear(b, carry):
                r = pl.multiple_of(b * ROW_BLOCK, ROW_BLOCK)
                y_ref[pl.ds(r, ROW_BLOCK), :] = jnp.zeros((ROW_BLOCK, y_ref.shape[1]), y_ref.dtype)
                return carry

            lax.fori_loop(nsub, tm // ROW_BLOCK, clear, 0)


def _expert_ffn(h, row_tok, w_gate, w_up, w_down, meta, n_tiles, tm, tf, tn):
    d = h.shape[1]
    f = w_gate.shape[2]
    tf = min(tf, f)
    tn = min(tn, d)
    n_up = f // tf
    n_down = d // tn
    n_steps = n_up + n_down

    def step(t, s, ntot):
        return jnp.where(t < ntot[0], s, n_steps - 1)

    def up_idx(t, s, te, trb, ns, ntot):
        return (te[t], 0, jnp.minimum(step(t, s, ntot), n_up - 1))

    def down_col(t, s, ntot):
        return jnp.maximum(step(t, s, ntot) - n_up, 0)

    grid_spec = pltpu.PrefetchScalarGridSpec(
        num_scalar_prefetch=4,
        grid=(n_tiles, n_steps),
        in_specs=[
            pl.BlockSpec((1, 1, tm), lambda t, s, te, trb, ns, ntot: (trb[t], 0, 0), memory_space=pltpu.SMEM),
            pl.BlockSpec(memory_space=pl.ANY),
            pl.BlockSpec((None, d, tf), up_idx),
            pl.BlockSpec((None, d, tf), up_idx),
            pl.BlockSpec((None, f, tn), lambda t, s, te, trb, ns, ntot: (te[t], 0, down_col(t, s, ntot))),
        ],
        out_specs=pl.BlockSpec((tm, tn), lambda t, s, te, trb, ns, ntot: (trb[t], down_col(t, s, ntot))),
        scratch_shapes=[
            pltpu.VMEM((tm, d), F32),
            pltpu.VMEM((d, tf), BF16),
            pltpu.VMEM((d, tf), BF16),
            pltpu.VMEM((f, tn), BF16),
            pltpu.VMEM((n_up, tm, tf), BF16),
            pltpu.SemaphoreType.DMA,
        ],
    )
    return pl.pallas_call(
        functools.partial(_expert_ffn_kernel, tm=tm, n_up=n_up),
        grid_spec=grid_spec,
        out_shape=jax.ShapeDtypeStruct((n_tiles * tm, d), F32),
        compiler_params=_params("arbitrary", "arbitrary"),
        name="expert_ffn",
    )(*meta, row_tok.reshape(n_tiles, 1, tm), h, w_gate, w_up, w_down)


def _combine_kernel(dest_ref, wt_ref, x_ref, g_ref, y_hbm, x2_ref, h3_ref, buf_ref, sem, *, tb):
    def row_copy(src, k, r):
        return pltpu.make_async_copy(y_hbm.at[pl.ds(src, 1)], buf_ref.at[k, pl.ds(r, 1)], sem)

    def issue(r, carry):
        for k in range(TOP_K):
            row_copy(dest_ref[0, 0, TOP_K * r + k], k, r).start()
        return carry

    lax.fori_loop(0, tb, issue, 0)

    def drain(r, carry):
        for k in range(TOP_K):
            row_copy(0, k, r).wait()
        return carry

    lax.fori_loop(0, tb, drain, 0)

    wt = wt_ref[...]
    x2 = x_ref[...] + (wt[:, 0:1] * buf_ref[0] + wt[:, 1:2] * buf_ref[1])
    x2_ref[...] = x2
    ms = jnp.mean(x2 * x2, axis=-1, keepdims=True)
    h3_ref[...] = (x2 * lax.rsqrt(ms + EPS) * g_ref[...]).astype(h3_ref.dtype)


def _combine(x, y, dest, wts, gain, tb):
    t, d = x.shape
    tb = min(tb, t)
    dest3 = dest.reshape(t // tb, 1, TOP_K * tb)
    row = pl.BlockSpec((tb, d), lambda i: (i, 0))
    return pl.pallas_call(
        functools.partial(_combine_kernel, tb=tb),
        grid=(t // tb,),
        in_specs=[
            pl.BlockSpec((1, 1, TOP_K * tb), lambda i: (i, 0, 0), memory_space=pltpu.SMEM),
            pl.BlockSpec((tb, LANES), lambda i: (i, 0)),
            row,
            pl.BlockSpec((1, d), lambda i: (0, 0)),
            pl.BlockSpec(memory_space=pl.ANY),
        ],
        out_specs=[row, row],
        out_shape=[jax.ShapeDtypeStruct((t, d), F32), jax.ShapeDtypeStruct((t, d), BF16)],
        scratch_shapes=[pltpu.VMEM((TOP_K, tb, d), F32), pltpu.SemaphoreType.DMA],
        compiler_params=_params("arbitrary"),
        name="moe_combine_norm",
    )(dest3, wts, x, gain.reshape(1, d), y)


def _ple_kernel(h_ref, p_ref, wg_ref, wp_ref, x_ref, o_ref, wgb_ref, wpb_ref):
    i = pl.program_id(1)

    @pl.when(i == 0)
    def _():
        wgb_ref[...] = wg_ref[...].astype(BF16)
        wpb_ref[...] = wp_ref[...].astype(BF16)

    a = jnp.dot(h_ref[...], wgb_ref[...], preferred_element_type=F32)
    b = jnp.dot(p_ref[...].astype(BF16), wpb_ref[...], preferred_element_type=F32)
    o_ref[...] = x_ref[...] + jax.nn.sigmoid(a) * b


def _ple(h, p, w_gate, w_proj, x, tm, tn):
    m, k = h.shape
    kp = p.shape[1]
    n = w_gate.shape[1]
    tm = min(tm, m)
    tn = min(tn, n)
    return pl.pallas_call(
        _ple_kernel,
        grid=(n // tn, m // tm),
        in_specs=[
            pl.BlockSpec((tm, k), lambda j, i: (i, 0)),
            pl.BlockSpec((tm, kp), lambda j, i: (i, 0)),
            pl.BlockSpec((k, tn), lambda j, i: (0, j)),
            pl.BlockSpec((kp, tn), lambda j, i: (0, j)),
            pl.BlockSpec((tm, tn), lambda j, i: (i, j)),
        ],
        out_specs=pl.BlockSpec((tm, tn), lambda j, i: (i, j)),
        out_shape=jax.ShapeDtypeStruct((m, n), F32),
        scratch_shapes=[pltpu.VMEM((k, tn), BF16), pltpu.VMEM((kp, tn), BF16)],
        compiler_params=_params("arbitrary", "arbitrary"),
        name="ple_gate_residual",
    )(h, p, w_gate, w_proj, x)


EXPERT_TILE_ROWS = 768


def _to_residue_major(t, d):
    if d == 1:
        return t
    s, c = t.shape
    return t.reshape(s // d, d, c).transpose(1, 0, 2).reshape(s, c)


def _from_residue_major(t, d):
    if d == 1:
        return t
    s, c = t.shape
    return t.reshape(d, s // d, c).transpose(1, 0, 2).reshape(s, c)


def _rope_tables(s):
    half = HEAD_DIM // 2
    inv_freq = ROPE_THETA ** (-jnp.arange(half, dtype=F32) / half)
    ang = jnp.arange(s, dtype=jnp.int32).astype(F32)[:, None] * inv_freq[None, :]
    cos, sin = jnp.cos(ang), jnp.sin(ang)
    return jnp.concatenate([cos, cos], axis=-1), jnp.concatenate([-sin, sin], axis=-1)


def _expert_tiles(counts, n_assign, tm):
    n_experts = counts.shape[0]
    n_tiles = -(-n_assign // tm) + n_experts
    tiles_e = (counts + tm - 1) // tm
    tcum = jnp.cumsum(tiles_e)
    tstart = tcum - tiles_e
    pstart = tstart * tm
    total = tcum[-1]
    t = jnp.arange(n_tiles, dtype=jnp.int32)
    tc = jnp.minimum(t, total - 1)
    te = jnp.minimum(jnp.searchsorted(tcum, tc, side="right"), n_experts - 1).astype(jnp.int32)
    local = tc - tstart[te]
    rows_left = counts[te] - local * tm
    nsub = jnp.clip((rows_left + ROW_BLOCK - 1) // ROW_BLOCK, 0, tm // ROW_BLOCK)
    nsub = jnp.where(t < total, nsub, 0).astype(jnp.int32)
    trb = (tstart[te] + local).astype(jnp.int32)
    meta = (te, trb, nsub, jnp.reshape(total, (1,)).astype(jnp.int32))
    return pstart.astype(jnp.int32), meta, n_tiles


def _layer(x, p, w_in, w_conv, w_up_a, w_out_b, w_o, g_mix, g_ffn, w_group, b_group, w_router, b_router,
           w_gate, w_up, w_down, g_ple, w_ple_gate, w_ple_proj):
    s, d = x.shape
    cw = w_conv.shape[1]
    n_experts = w_gate.shape[0]
    ng = len(ATTN_DILATIONS)
    a_out = HEADS_PER_GROUP * HEAD_DIM
    q_cols = ng * a_out
    col_k, col_v = q_cols, q_cols + a_out
    col_u = q_cols + 2 * a_out
    col_gb, col_gc = col_u + cw, col_u + 2 * cw
    col_ma = col_u + 3 * cw
    col_mb = col_ma + d

    h = _rmsnorm(x, g_mix, BF16, 256)
    rope_c, rope_s = _rope_tables(s)
    proj = _proj(h, w_in, rope_c, rope_s, q_cols + a_out, 1024, 512)

    k = proj[:, col_k:col_k + a_out]
    v = proj[:, col_v:col_v + a_out]
    q3 = jnp.stack([_to_residue_major(proj[:, g * a_out:(g + 1) * a_out], dl) for g, dl in enumerate(ATTN_DILATIONS)])
    k3 = jnp.stack([_to_residue_major(k, dl) for dl in ATTN_DILATIONS])
    v3 = jnp.stack([_to_residue_major(v, dl) for dl in ATTN_DILATIONS])
    seg_blocks = tuple(s // dl // ATTN_BLOCK for dl in ATTN_DILATIONS)
    o3, lse3 = _attention(q3, k3, v3, seg_blocks)
    o3 = jnp.stack([_from_residue_major(o3[g], dl) for g, dl in enumerate(ATTN_DILATIONS)])
    lse3 = jnp.stack([_from_residue_major(lse3[g], dl) for g, dl in enumerate(ATTN_DILATIONS)])
    y_a = _attn_mix(o3, lse3, 512)

    y_b = _short_conv(proj, w_conv, col_u, col_gb, col_gc, 512, 512)
    merged = _merge(y_a, y_b, proj, w_up_a, w_out_b, col_ma, col_mb, 1024, 512)
    x1 = _resid_mm(merged, w_o, x, 1024, 512)

    w_route = jnp.zeros((d, LANES), F32).at[:, :N_EXPERT_GROUPS].set(w_group)
    w_route = w_route.at[:, N_EXPERT_GROUPS:N_EXPERT_GROUPS + n_experts].set(w_router)
    b_route = jnp.zeros((1, LANES), F32).at[0, :N_EXPERT_GROUPS].set(b_group)
    b_route = b_route.at[0, N_EXPERT_GROUPS:N_EXPERT_GROUPS + n_experts].set(b_router)
    h2, eid, wts = _router(x1, g_ffn, w_route, b_route, N_EXPERT_GROUPS, n_experts, 256)
    rank, cnt = _rank(eid, 256)
    counts = cnt[0, :n_experts]
    tm_e = EXPERT_TILE_ROWS
    pstart, meta, n_tiles = _expert_tiles(counts, s * TOP_K, tm_e)
    dest = (pstart[eid[:, :TOP_K]] + rank[:, :TOP_K]).reshape(-1)
    tokens = jnp.repeat(jnp.arange(s, dtype=jnp.int32), TOP_K)
    row_tok = jnp.zeros((n_tiles * tm_e,), jnp.int32).at[dest].set(tokens)
    y = _expert_ffn(h2, row_tok, w_gate, w_up, w_down, meta, n_tiles, tm_e, 256, 512)
    x2, h3 = _combine(x1, y, dest, wts, g_ple, 256)

    return _ple(h3, p, w_ple_gate, w_ple_proj, x2, 1024, 512)


def kernel(x, p, w_in, w_conv, w_up_a, w_out_b, w_o, norm_mix, norm_ffn, w_group, b_group, w_router, b_router,
           w_gate, w_up, w_down, norm_ple, w_ple_gate, w_ple_proj, norm_final):
    b, s, d = x.shape
    depth = w_in.shape[0]
    outs = []
    for bi in range(b):
        xb = x[bi]
        for i in range(depth):
            xb = _layer(xb, p[i, bi], w_in[i], w_conv[i], w_up_a[i], w_out_b[i], w_o[i], norm_mix[i], norm_ffn[i],
                        w_group[i], b_group[i], w_router[i], b_router[i], w_gate[i], w_up[i], w_down[i],
                        norm_ple[i], w_ple_gate[i], w_ple_proj[i])
        outs.append(_rmsnorm(xb, norm_final, x.dtype, 256))
    return jnp.stack(outs)
```

```python
import functools

import jax
import jax.numpy as jnp
from jax import lax
from jax.experimental import pallas as pl
from jax.experimental.pallas import tpu as pltpu

F32 = jnp.float32
BF16 = jnp.bfloat16

HEAD_DIM = 128
ATTN_DILATIONS = (1, 4, 16)
HEADS_PER_GROUP = 4
ROPE_THETA = 10000.0
ATTN_BLOCK = 128
NEG_BIG = -1e30
N_EXPERT_GROUPS = 4
TOP_K = 2
EPS = 1e-6
ROW_BLOCK = 128
LANES = 128

VMEM_LIMIT_BYTES = 56 * 1024 * 1024


def _params(*sem):
    return pltpu.CompilerParams(dimension_semantics=sem, vmem_limit_bytes=VMEM_LIMIT_BYTES)


def _rmsnorm_kernel(x_ref, g_ref, o_ref):
    x = x_ref[...]
    ms = jnp.mean(x * x, axis=-1, keepdims=True)
    o_ref[...] = (x * lax.rsqrt(ms + EPS) * g_ref[...]).astype(o_ref.dtype)


def _rmsnorm(x, gain, out_dtype, tm):
    n, d = x.shape
    tm = min(tm, n)
    return pl.pallas_call(
        _rmsnorm_kernel,
        grid=(n // tm,),
        in_specs=[pl.BlockSpec((tm, d), lambda i: (i, 0)), pl.BlockSpec((1, d), lambda i: (0, 0))],
        out_specs=pl.BlockSpec((tm, d), lambda i: (i, 0)),
        out_shape=jax.ShapeDtypeStruct((n, d), out_dtype),
        compiler_params=_params("arbitrary"),
        name="rmsnorm",
    )(x, gain.reshape(1, d))


def _qkv_proj_kernel(h_ref, w_ref, c_ref, s_ref, o_ref, wb_ref, *, n_rope_tiles):
    j = pl.program_id(0)
    i = pl.program_id(1)

    @pl.when(i == 0)
    def _():
        wb_ref[...] = w_ref[...].astype(BF16)

    acc = jnp.dot(h_ref[...], wb_ref[...], preferred_element_type=F32)
    heads = o_ref.shape[0]

    @pl.when(j < n_rope_tiles)
    def _():
        c = c_ref[...]
        s = s_ref[...]
        for hh in range(heads):
            t = acc[:, hh * HEAD_DIM:(hh + 1) * HEAD_DIM]
            o_ref[hh] = t * c + pltpu.roll(t, HEAD_DIM // 2, 1) * s

    @pl.when(j >= n_rope_tiles)
    def _():
        for hh in range(heads):
            o_ref[hh] = acc[:, hh * HEAD_DIM:(hh + 1) * HEAD_DIM]


def _qkv_proj(h, w, rope_c, rope_s, n_cols, n_rope_cols, tm, tn):
    m, k = h.shape
    tm = min(tm, m)
    assert n_cols % tn == 0 and n_rope_cols % tn == 0 and m % tm == 0 and tn % HEAD_DIM == 0
    heads = tn // HEAD_DIM
    return pl.pallas_call(
        functools.partial(_qkv_proj_kernel, n_rope_tiles=n_rope_cols // tn),
        grid=(n_cols // tn, m // tm),
        in_specs=[
            pl.BlockSpec((tm, k), lambda j, i: (i, 0)),
            pl.BlockSpec((k, tn), lambda j, i: (0, j)),
            pl.BlockSpec((tm, HEAD_DIM), lambda j, i: (i, 0)),
            pl.BlockSpec((tm, HEAD_DIM), lambda j, i: (i, 0)),
        ],
        out_specs=pl.BlockSpec((heads, tm, HEAD_DIM), lambda j, i: (j, i, 0)),
        out_shape=jax.ShapeDtypeStruct((n_cols // HEAD_DIM, m, HEAD_DIM), F32),
        scratch_shapes=[pltpu.VMEM((k, tn), BF16)],
        compiler_params=_params("arbitrary", "arbitrary"),
        name="qkv_proj_rope",
    )(h, w, rope_c, rope_s)


def _mm_kernel(a_ref, w_ref, o_ref, wb_ref):
    i = pl.program_id(1)

    @pl.when(i == 0)
    def _():
        wb_ref[...] = w_ref[...].astype(BF16)

    o_ref[...] = jnp.dot(a_ref[...], wb_ref[...], preferred_element_type=F32).astype(o_ref.dtype)


def _mm_cols(a, w, col0, tm, tn):
    m, k = a.shape
    n = w.shape[1] - col0
    tm = min(tm, m)
    assert n % tn == 0 and col0 % tn == 0 and m % tm == 0
    return pl.pallas_call(
        _mm_kernel,
        grid=(n // tn, m // tm),
        in_specs=[
            pl.BlockSpec((tm, k), lambda j, i: (i, 0)),
            pl.BlockSpec((k, tn), lambda j, i: (0, col0 // tn + j)),
        ],
        out_specs=pl.BlockSpec((tm, tn), lambda j, i: (i, j)),
        out_shape=jax.ShapeDtypeStruct((m, n), BF16),
        scratch_shapes=[pltpu.VMEM((k, tn), BF16)],
        compiler_params=_params("arbitrary", "arbitrary"),
        name="in_proj",
    )(a, w)


ATTN_SUPER = ATTN_BLOCK * max(ATTN_DILATIONS)
ATTN_UNROLL = 4


def _rows(start, size, stride):
    return pl.ds(start, size) if stride == 1 else pl.ds(start, size, stride=stride)


def _dilated_attn_kernel(q0_ref, q1_ref, q2_ref, kc_ref, kp_ref, vc_ref, vp_ref, y_ref,
                         kcat_ref, vcat_ref, o_ref, lse_ref):
    sb = pl.program_id(1)
    sup = ATTN_SUPER
    kcat_ref[pl.ds(0, sup), :] = kp_ref[...]
    kcat_ref[pl.ds(sup, sup), :] = kc_ref[...]
    vcat_ref[pl.ds(0, sup), :] = vp_ref[...]
    vcat_ref[pl.ds(sup, sup), :] = vc_ref[...]
    qi = lax.broadcasted_iota(jnp.int32, (ATTN_BLOCK, 2 * ATTN_BLOCK), 0)
    kj = lax.broadcasted_iota(jnp.int32, (ATTN_BLOCK, 2 * ATTN_BLOCK), 1)
    dist = qi + ATTN_BLOCK - kj
    band = (dist >= 0) & (dist <= ATTN_BLOCK)
    in_cur = kj >= ATTN_BLOCK
    scale = HEAD_DIM ** -0.5

    for g, (q_ref, d) in enumerate(zip((q0_ref, q1_ref, q2_ref), ATTN_DILATIONS)):
        span = ATTN_BLOCK * d
        shift = d.bit_length() - 1

        def unit(u, carry, g=g, q_ref=q_ref, d=d, span=span, shift=shift):
            j = lax.shift_right_logical(u, shift)
            row0 = j * span + (u & (d - 1))
            q = q_ref[_rows(row0, ATTN_BLOCK, d), :].astype(BF16)
            k = kcat_ref[_rows(sup - span + row0, 2 * ATTN_BLOCK, d), :].astype(BF16)
            v = vcat_ref[_rows(sup - span + row0, 2 * ATTN_BLOCK, d), :].astype(BF16)
            s = lax.dot_general(q, k, (((1,), (1,)), ((), ())), preferred_element_type=F32) * scale
            has_prev = (sb > 0) | (j > 0)
            s = jnp.where(band & (in_cur | has_prev), s, NEG_BIG)
            m = jnp.max(s, axis=-1, keepdims=True)
            e = jnp.exp(s - m)
            ssum = jnp.sum(e, axis=-1, keepdims=True)
            p = (e / ssum).astype(BF16)
            o_ref[g, _rows(row0, ATTN_BLOCK, d), :] = jnp.dot(p, v, preferred_element_type=F32)
            lse_ref[g, _rows(row0, ATTN_BLOCK, d), :] = jnp.broadcast_to(m + jnp.log(ssum), (ATTN_BLOCK, HEAD_DIM))
            return carry

        lax.fori_loop(0, sup // ATTN_BLOCK, unit, 0, unroll=ATTN_UNROLL)

    lse = lse_ref[...]
    m = jnp.max(lse, axis=0)
    e = jnp.exp(lse - m[None])
    w = e / jnp.sum(e, axis=0)[None]
    y_ref[...] = jnp.sum(w * o_ref[...], axis=0).astype(y_ref.dtype)


def _dilated_attention(qkv):
    nh, s, hd = qkv.shape
    ng = len(ATTN_DILATIONS)
    hpg = HEADS_PER_GROUP
    assert nh == (ng + 2) * hpg and hd == HEAD_DIM and s % ATTN_SUPER == 0
    sup = ATTN_SUPER

    def q_spec(g):
        return pl.BlockSpec((None, sup, hd), lambda h, b: (g * hpg + h, b, 0))

    def kv_spec(first_head, prev):
        if prev:
            return pl.BlockSpec((None, sup, hd), lambda h, b: (first_head + h, jnp.maximum(b - 1, 0), 0))
        return pl.BlockSpec((None, sup, hd), lambda h, b: (first_head + h, b, 0))

    k0, v0 = ng * hpg, (ng + 1) * hpg
    return pl.pallas_call(
        _dilated_attn_kernel,
        grid=(hpg, s // sup),
        in_specs=[q_spec(0), q_spec(1), q_spec(2), kv_spec(k0, False), kv_spec(k0, True),
                  kv_spec(v0, False), kv_spec(v0, True)],
        out_specs=pl.BlockSpec((sup, hd), lambda h, b: (b, h)),
        out_shape=jax.ShapeDtypeStruct((s, hpg * hd), BF16),
        scratch_shapes=[pltpu.VMEM((2 * sup, hd), F32), pltpu.VMEM((2 * sup, hd), F32),
                        pltpu.VMEM((ng, sup, hd), F32), pltpu.VMEM((ng, sup, hd), F32)],
        compiler_params=_params("arbitrary", "arbitrary"),
        name="dilated_attention",
    )(qkv, qkv, qkv, qkv, qkv, qkv, qkv)


CONV_HALO = 16


def _conv_kernel(u_ref, gc_ref, gb_ref, up_ref, gcp_ref, w_ref, o_ref):
    i = pl.program_id(0)
    cur = u_ref[...].astype(F32) * gc_ref[...].astype(F32)
    prev = up_ref[...].astype(F32) * gcp_ref[...].astype(F32)
    prev = jnp.where(i == 0, 0.0, prev)
    big = jnp.concatenate([prev, cur], axis=0)
    back1 = pltpu.roll(big, 1, 0)[CONV_HALO:]
    back2 = pltpu.roll(big, 2, 0)[CONV_HALO:]
    w = w_ref[...]
    y = w[0:1] * back2 + w[1:2] * back1 + w[2:3] * cur
    o_ref[...] = (gb_ref[...].astype(F32) * y).astype(o_ref.dtype)


def _short_conv(proj, w_conv, col_u, col_gb, col_gc, tm, tc):
    s = proj.shape[0]
    kw, cw = w_conv.shape
    assert kw == 3
    tm = min(tm, s)
    tc = min(tc, cw)
    assert col_u % tc == 0 and col_gb % tc == 0 and col_gc % tc == 0 and tm % CONV_HALO == 0
    hb = tm // CONV_HALO

    def cur(col):
        return pl.BlockSpec((tm, tc), lambda i, j: (i, col // tc + j))

    def prev(col):
        return pl.BlockSpec((CONV_HALO, tc), lambda i, j: (jnp.maximum(i * hb - 1, 0), col // tc + j))

    return pl.pallas_call(
        _conv_kernel,
        grid=(s // tm, cw // tc),
        in_specs=[cur(col_u), cur(col_gc), cur(col_gb), prev(col_u), prev(col_gc),
                  pl.BlockSpec((kw, tc), lambda i, j: (0, j))],
        out_specs=pl.BlockSpec((tm, tc), lambda i, j: (i, j)),
        out_shape=jax.ShapeDtypeStruct((s, cw), BF16),
        compiler_params=_params("arbitrary", "arbitrary"),
        name="short_conv",
    )(proj, proj, proj, proj, proj, w_conv)


def _merge_kernel(ya_ref, yb_ref, wa_ref, wb_ref, ma_ref, mb_ref, o_ref, wab_ref, wbb_ref):
    i = pl.program_id(1)

    @pl.when(i == 0)
    def _():
        wab_ref[...] = wa_ref[...].astype(BF16)
        wbb_ref[...] = wb_ref[...].astype(BF16)

    a = jnp.dot(ya_ref[...], wab_ref[...], preferred_element_type=F32)
    b = jnp.dot(yb_ref[...], wbb_ref[...], preferred_element_type=F32)
    ga = jax.nn.sigmoid(ma_ref[...].astype(F32))
    gb = jax.nn.sigmoid(mb_ref[...].astype(F32))
    o_ref[...] = (ga * a + gb * b).astype(o_ref.dtype)


def _merge(y_a, y_b, proj, w_a, w_b, col_ma, col_mb, tm, tn):
    m, ka = y_a.shape
    kb = y_b.shape[1]
    n = w_a.shape[1]
    tm = min(tm, m)
    tn = min(tn, n)
    assert col_ma % tn == 0 and col_mb % tn == 0
    return pl.pallas_call(
        _merge_kernel,
        grid=(n // tn, m // tm),
        in_specs=[
            pl.BlockSpec((tm, ka), lambda j, i: (i, 0)),
            pl.BlockSpec((tm, kb), lambda j, i: (i, 0)),
            pl.BlockSpec((ka, tn), lambda j, i: (0, j)),
            pl.BlockSpec((kb, tn), lambda j, i: (0, j)),
            pl.BlockSpec((tm, tn), lambda j, i: (i, col_ma // tn + j)),
            pl.BlockSpec((tm, tn), lambda j, i: (i, col_mb // tn + j)),
        ],
        out_specs=pl.BlockSpec((tm, tn), lambda j, i: (i, j)),
        out_shape=jax.ShapeDtypeStruct((m, n), BF16),
        scratch_shapes=[pltpu.VMEM((ka, tn), BF16), pltpu.VMEM((kb, tn), BF16)],
        compiler_params=_params("arbitrary", "arbitrary"),
        name="mixer_merge",
    )(y_a, y_b, w_a, w_b, proj, proj)


def _resid_mm_kernel(a_ref, w_ref, x_ref, o_ref, wb_ref):
    i = pl.program_id(1)

    @pl.when(i == 0)
    def _():
        wb_ref[...] = w_ref[...].astype(BF16)

    o_ref[...] = x_ref[...] + jnp.dot(a_ref[...], wb_ref[...], preferred_element_type=F32)


def _resid_mm(a, w, x, tm, tn):
    m, k = a.shape
    n = w.shape[1]
    tm = min(tm, m)
    tn = min(tn, n)
    return pl.pallas_call(
        _resid_mm_kernel,
        grid=(n // tn, m // tm),
        in_specs=[
            pl.BlockSpec((tm, k), lambda j, i: (i, 0)),
            pl.BlockSpec((k, tn), lambda j, i: (0, j)),
            pl.BlockSpec((tm, tn), lambda j, i: (i, j)),
        ],
        out_specs=pl.BlockSpec((tm, tn), lambda j, i: (i, j)),
        out_shape=jax.ShapeDtypeStruct((m, n), F32),
        scratch_shapes=[pltpu.VMEM((k, tn), BF16)],
        compiler_params=_params("arbitrary", "arbitrary"),
        name="out_proj_residual",
    )(a, w, x)


def _split_bf16(a):
    hi = a.astype(BF16)
    lo = (a - hi.astype(F32)).astype(BF16)
    return hi, lo


def _router_kernel(x_ref, g_ref, w_ref, b_ref, h_ref, eid_ref, wt_ref, *, n_groups, n_experts):
    x = x_ref[...]
    ms = jnp.mean(x * x, axis=-1, keepdims=True)
    h = x * lax.rsqrt(ms + EPS) * g_ref[...]
    h_ref[...] = h
    hh, hl = _split_bf16(h)
    wh, wl = _split_bf16(w_ref[...])
    logits = (jnp.dot(hh, wh, preferred_element_type=F32) + jnp.dot(hh, wl, preferred_element_type=F32)
              + jnp.dot(hl, wh, preferred_element_type=F32)) + b_ref[...]
    tm = x.shape[0]
    per_group = n_experts // n_groups
    lane = lax.broadcasted_iota(jnp.int32, (tm, LANES), 1)
    is_g = lane < n_groups
    gl = jnp.where(is_g, logits, NEG_BIG)
    gmax = jnp.max(gl, axis=-1, keepdims=True)
    gidx = jnp.min(jnp.where(gl == gmax, lane, LANES), axis=-1, keepdims=True)
    gsum = jnp.sum(jnp.where(is_g, jnp.exp(gl - gmax), 0.0), axis=-1, keepdims=True)
    g_w = 1.0 / gsum
    lane_e = lane - n_groups
    sel = (lane_e >= 0) & (lane_e < n_experts) & (lane_e // per_group == gidx)
    el = jnp.where(sel, logits, NEG_BIG)
    emax = jnp.max(el, axis=-1, keepdims=True)
    i1 = jnp.min(jnp.where(el == emax, lane, LANES), axis=-1, keepdims=True)
    esum = jnp.sum(jnp.where(sel, jnp.exp(el - emax), 0.0), axis=-1, keepdims=True)
    el2 = jnp.where(lane == i1, NEG_BIG, el)
    emax2 = jnp.max(el2, axis=-1, keepdims=True)
    i2 = jnp.min(jnp.where(el2 == emax2, lane, LANES), axis=-1, keepdims=True)
    p1 = 1.0 / esum
    p2 = jnp.exp(emax2 - emax) / esum
    den = p1 + p2
    eid_ref[...] = jnp.where(lane == 0, i1 - n_groups, jnp.where(lane == 1, i2 - n_groups, 0))
    wt_ref[...] = jnp.where(lane == 0, g_w * (p1 / den), jnp.where(lane == 1, g_w * (p2 / den), 0.0))


def _router(x, gain, w_route, b_route, n_groups, n_experts, tm):
    t, d = x.shape
    tm = min(tm, t)
    row = pl.BlockSpec((tm, d), lambda i: (i, 0))
    narrow = pl.BlockSpec((tm, LANES), lambda i: (i, 0))
    return pl.pallas_call(
        functools.partial(_router_kernel, n_groups=n_groups, n_experts=n_experts),
        grid=(t // tm,),
        in_specs=[row, pl.BlockSpec((1, d), lambda i: (0, 0)), pl.BlockSpec((d, LANES), lambda i: (0, 0)),
                  pl.BlockSpec((1, LANES), lambda i: (0, 0))],
        out_specs=[row, narrow, narrow],
        out_shape=[jax.ShapeDtypeStruct((t, d), F32), jax.ShapeDtypeStruct((t, LANES), jnp.int32),
                   jax.ShapeDtypeStruct((t, LANES), F32)],
        compiler_params=_params("arbitrary"),
        name="ffn_norm_router",
    )(x, gain.reshape(1, d), w_route, b_route)


def _rank_kernel(eid_ref, rank_ref, cnt_ref, carry_ref):
    i = pl.program_id(0)

    @pl.when(i == 0)
    def _():
        carry_ref[...] = jnp.zeros_like(carry_ref)

    eid = eid_ref[...]
    tm = eid.shape[0]
    lane = lax.broadcasted_iota(jnp.int32, (tm, LANES), 1)
    oh0 = lane == eid[:, 0:1]
    oh1 = lane == eid[:, 1:2]
    ohs = oh0.astype(F32) + oh1.astype(F32)
    r = lax.broadcasted_iota(jnp.int32, (tm, tm), 0)
    c = lax.broadcasted_iota(jnp.int32, (tm, tm), 1)
    earlier = (c < r).astype(BF16)
    before = jnp.dot(earlier, ohs.astype(BF16), preferred_element_type=F32) + carry_ref[...]
    rank0 = jnp.sum(jnp.where(oh0, before, 0.0), axis=-1, keepdims=True)
    rank1 = jnp.sum(jnp.where(oh1, before, 0.0), axis=-1, keepdims=True)
    rank_ref[...] = jnp.where(lane == 0, rank0, jnp.where(lane == 1, rank1, 0.0)).astype(jnp.int32)
    total = carry_ref[...] + jnp.sum(ohs, axis=0, keepdims=True)
    carry_ref[...] = total
    cnt_ref[...] = jnp.broadcast_to(total, cnt_ref.shape).astype(jnp.int32)


def _rank(eid, tm):
    t = eid.shape[0]
    tm = min(tm, t)
    narrow = pl.BlockSpec((tm, LANES), lambda i: (i, 0))
    return pl.pallas_call(
        _rank_kernel,
        grid=(t // tm,),
        in_specs=[narrow],
        out_specs=[narrow, pl.BlockSpec((8, LANES), lambda i: (0, 0))],
        out_shape=[jax.ShapeDtypeStruct((t, LANES), jnp.int32), jax.ShapeDtypeStruct((8, LANES), jnp.int32)],
        scratch_shapes=[pltpu.VMEM((1, LANES), F32)],
        compiler_params=_params("arbitrary"),
        name="expert_rank",
    )(eid)


GATHER_UNROLL = 8


def _for_row_blocks(nsub, fn):
    pairs = lax.shift_right_logical(nsub, 1)

    def body(i, carry):
        fn(pl.multiple_of(i * (2 * ROW_BLOCK), 2 * ROW_BLOCK), 2 * ROW_BLOCK)
        return carry

    lax.fori_loop(0, pairs, body, 0)

    @pl.when((nsub & 1) == 1)
    def _():
        fn(pl.multiple_of((nsub - 1) * ROW_BLOCK, ROW_BLOCK), ROW_BLOCK)


def _expert_ffn_kernel(te_ref, trb_ref, nsub_ref, ntot_ref, tok_ref, tok_next_ref, h_hbm, wg_ref, wu_ref, wd_ref,
                       y_ref, land_ref, xb_ref, wgb_ref, wub_ref, wdb_ref, hid_ref, sem, *, tm, n_up):
    t = pl.program_id(0)
    s = pl.program_id(1)
    n_tiles = pl.num_programs(0)
    nsub = nsub_ref[t]

    def row_copy(tok, r):
        return pltpu.make_async_copy(h_hbm.at[pl.ds(tok, 1)], land_ref.at[pl.ds(r, 1)], sem)

    def start_gather(toks_ref, blocks):
        def issue(i, carry):
            for u in range(GATHER_UNROLL):
                r = i * GATHER_UNROLL + u
                row_copy(toks_ref[0, 0, r], r).start(priority=u % 2)
            return carry

        lax.fori_loop(0, blocks * (ROW_BLOCK // GATHER_UNROLL), issue, 0)

    @pl.when(nsub > 0)
    def _():
        @pl.when(s == 0)
        def _():
            @pl.when(t == 0)
            def _():
                start_gather(tok_ref, nsub)

            def drain(i, carry):
                for u in range(GATHER_UNROLL):
                    row_copy(0, 0).wait()
                return carry

            lax.fori_loop(0, nsub * (ROW_BLOCK // GATHER_UNROLL), drain, 0)

            def to_bf16(b, carry):
                r = pl.multiple_of(b * ROW_BLOCK, ROW_BLOCK)
                xb_ref[pl.ds(r, ROW_BLOCK), :] = land_ref[pl.ds(r, ROW_BLOCK), :].astype(BF16)
                return carry

            lax.fori_loop(0, nsub, to_bf16, 0)

        @pl.when(s == 1)
        def _():
            nxt = jnp.minimum(t + 1, n_tiles - 1)
            n_next = jnp.where(t + 1 < n_tiles, nsub_ref[nxt], 0)
            start_gather(tok_next_ref, n_next)

        @pl.when(s < n_up)
        def _():
            wgb_ref[...] = wg_ref[...].astype(BF16)
            wub_ref[...] = wu_ref[...].astype(BF16)

            def up(r, rows):
                xb = xb_ref[pl.ds(r, rows), :]
                g = jnp.dot(xb, wgb_ref[...], preferred_element_type=F32)
                u = jnp.dot(xb, wub_ref[...], preferred_element_type=F32)
                hid_ref[s, pl.ds(r, rows), :] = (g * jax.nn.sigmoid(g) * u).astype(hid_ref.dtype)

            _for_row_blocks(nsub, up)

        @pl.when(s >= n_up)
        def _():
            wdb_ref[...] = wd_ref[...].astype(BF16)

            def down(r, rows):
                hid = jnp.concatenate([hid_ref[c, pl.ds(r, rows), :] for c in range(n_up)], axis=1)
                y_ref[pl.ds(r, rows), :] = jnp.dot(hid, wdb_ref[...], preferred_element_type=F32)

            _for_row_blocks(nsub, down)

            def clear(b, carry):
                r = pl.multiple_of(b * ROW_BLOCK, ROW_BLOCK)
                y_ref[pl.ds(r, ROW_BLOCK), :] = jnp.zeros((ROW_BLOCK, y_ref.shape[1]), y_ref.dtype)
                return carry

            lax.fori_loop(nsub, tm // ROW_BLOCK, clear, 0)


def _expert_ffn(h, row_tok, w_gate, w_up, w_down, meta, n_tiles, tm, tf, tn):
    d = h.shape[1]
    f = w_gate.shape[2]
    tf = min(tf, f)
    tn = min(tn, d)
    n_up = f // tf
    n_down = d // tn
    n_steps = n_up + n_down

    def step(t, s, ntot):
        return jnp.where(t < ntot[0], s, n_steps - 1)

    def up_idx(t, s, te, trb, ns, ntot):
        return (te[t], 0, jnp.minimum(step(t, s, ntot), n_up - 1))

    def down_col(t, s, ntot):
        return jnp.maximum(step(t, s, ntot) - n_up, 0)

    grid_spec = pltpu.PrefetchScalarGridSpec(
        num_scalar_prefetch=4,
        grid=(n_tiles, n_steps),
        in_specs=[
            pl.BlockSpec((1, 1, tm), lambda t, s, te, trb, ns, ntot: (trb[t], 0, 0), memory_space=pltpu.SMEM),
            pl.BlockSpec((1, 1, tm), lambda t, s, te, trb, ns, ntot: (trb[jnp.minimum(t + 1, n_tiles - 1)], 0, 0),
                         memory_space=pltpu.SMEM),
            pl.BlockSpec(memory_space=pl.ANY),
            pl.BlockSpec((None, d, tf), up_idx),
            pl.BlockSpec((None, d, tf), up_idx),
            pl.BlockSpec((None, f, tn), lambda t, s, te, trb, ns, ntot: (te[t], 0, down_col(t, s, ntot))),
        ],
        out_specs=pl.BlockSpec((tm, tn), lambda t, s, te, trb, ns, ntot: (trb[t], down_col(t, s, ntot))),
        scratch_shapes=[
            pltpu.VMEM((tm, d), F32),
            pltpu.VMEM((tm, d), BF16),
            pltpu.VMEM((d, tf), BF16),
            pltpu.VMEM((d, tf), BF16),
            pltpu.VMEM((f, tn), BF16),
            pltpu.VMEM((n_up, tm, tf), BF16),
            pltpu.SemaphoreType.DMA,
        ],
    )
    return pl.pallas_call(
        functools.partial(_expert_ffn_kernel, tm=tm, n_up=n_up),
        grid_spec=grid_spec,
        out_shape=jax.ShapeDtypeStruct((n_tiles * tm, d), F32),
        compiler_params=_params("arbitrary", "arbitrary"),
        name="expert_ffn",
    )(*meta, row_tok.reshape(n_tiles, 1, tm), row_tok.reshape(n_tiles, 1, tm), h, w_gate, w_up, w_down)


def _combine_kernel(dest_ref, wt_ref, x_ref, g_ref, y_hbm, x2_ref, h3_ref, buf_ref, sem, *, tb):
    def row_copy(src, k, r):
        return pltpu.make_async_copy(y_hbm.at[pl.ds(src, 1)], buf_ref.at[k, pl.ds(r, 1)], sem)

    def issue(r, carry):
        for k in range(TOP_K):
            row_copy(dest_ref[0, 0, TOP_K * r + k], k, r).start(priority=k % 2)
        return carry

    lax.fori_loop(0, tb, issue, 0)

    def drain(r, carry):
        for k in range(TOP_K):
            row_copy(0, k, r).wait()
        return carry

    lax.fori_loop(0, tb, drain, 0)

    wt = wt_ref[...]
    x2 = x_ref[...] + (wt[:, 0:1] * buf_ref[0] + wt[:, 1:2] * buf_ref[1])
    x2_ref[...] = x2
    ms = jnp.mean(x2 * x2, axis=-1, keepdims=True)
    h3_ref[...] = (x2 * lax.rsqrt(ms + EPS) * g_ref[...]).astype(h3_ref.dtype)


def _combine(x, y, dest, wts, gain, tb):
    t, d = x.shape
    tb = min(tb, t)
    dest3 = dest.reshape(t // tb, 1, TOP_K * tb)
    row = pl.BlockSpec((tb, d), lambda i: (i, 0))
    return pl.pallas_call(
        functools.partial(_combine_kernel, tb=tb),
        grid=(t // tb,),
        in_specs=[
            pl.BlockSpec((1, 1, TOP_K * tb), lambda i: (i, 0, 0), memory_space=pltpu.SMEM),
            pl.BlockSpec((tb, LANES), lambda i: (i, 0)),
            row,
            pl.BlockSpec((1, d), lambda i: (0, 0)),
            pl.BlockSpec(memory_space=pl.ANY),
        ],
        out_specs=[row, row],
        out_shape=[jax.ShapeDtypeStruct((t, d), F32), jax.ShapeDtypeStruct((t, d), BF16)],
        scratch_shapes=[pltpu.VMEM((TOP_K, tb, d), F32), pltpu.SemaphoreType.DMA],
        compiler_params=_params("arbitrary"),
        name="moe_combine_norm",
    )(dest3, wts, x, gain.reshape(1, d), y)


def _ple_kernel(h_ref, p_ref, wg_ref, wp_ref, x_ref, o_ref, wgb_ref, wpb_ref):
    i = pl.program_id(1)

    @pl.when(i == 0)
    def _():
        wgb_ref[...] = wg_ref[...].astype(BF16)
        wpb_ref[...] = wp_ref[...].astype(BF16)

    a = jnp.dot(h_ref[...], wgb_ref[...], preferred_element_type=F32)
    b = jnp.dot(p_ref[...].astype(BF16), wpb_ref[...], preferred_element_type=F32)
    o_ref[...] = x_ref[...] + jax.nn.sigmoid(a) * b


def _ple(h, p, w_gate, w_proj, x, tm, tn):
    m, k = h.shape
    kp = p.shape[1]
    n = w_gate.shape[1]
    tm = min(tm, m)
    tn = min(tn, n)
    return pl.pallas_call(
        _ple_kernel,
        grid=(n // tn, m // tm),
        in_specs=[
            pl.BlockSpec((tm, k), lambda j, i: (i, 0)),
            pl.BlockSpec((tm, kp), lambda j, i: (i, 0)),
            pl.BlockSpec((k, tn), lambda j, i: (0, j)),
            pl.BlockSpec((kp, tn), lambda j, i: (0, j)),
            pl.BlockSpec((tm, tn), lambda j, i: (i, j)),
        ],
        out_specs=pl.BlockSpec((tm, tn), lambda j, i: (i, j)),
        out_shape=jax.ShapeDtypeStruct((m, n), F32),
        scratch_shapes=[pltpu.VMEM((k, tn), BF16), pltpu.VMEM((kp, tn), BF16)],
        compiler_params=_params("arbitrary", "arbitrary"),
        name="ple_gate_residual",
    )(h, p, w_gate, w_proj, x)


EXPERT_TILE_ROWS = 768


def _rope_tables(s):
    half = HEAD_DIM // 2
    inv_freq = ROPE_THETA ** (-jnp.arange(half, dtype=F32) / half)
    ang = jnp.arange(s, dtype=jnp.int32).astype(F32)[:, None] * inv_freq[None, :]
    cos, sin = jnp.cos(ang), jnp.sin(ang)
    return jnp.concatenate([cos, cos], axis=-1), jnp.concatenate([-sin, sin], axis=-1)


def _expert_tiles(counts, n_assign, tm):
    n_experts = counts.shape[0]
    n_tiles = -(-n_assign // tm) + n_experts
    tiles_e = (counts + tm - 1) // tm
    tcum = jnp.cumsum(tiles_e)
    tstart = tcum - tiles_e
    pstart = tstart * tm
    total = tcum[-1]
    t = jnp.arange(n_tiles, dtype=jnp.int32)
    tc = jnp.minimum(t, total - 1)
    te = jnp.minimum(jnp.searchsorted(tcum, tc, side="right"), n_experts - 1).astype(jnp.int32)
    local = tc - tstart[te]
    rows_left = counts[te] - local * tm
    nsub = jnp.clip((rows_left + ROW_BLOCK - 1) // ROW_BLOCK, 0, tm // ROW_BLOCK)
    nsub = jnp.where(t < total, nsub, 0).astype(jnp.int32)
    trb = (tstart[te] + local).astype(jnp.int32)
    meta = (te, trb, nsub, jnp.reshape(total, (1,)).astype(jnp.int32))
    return pstart.astype(jnp.int32), meta, n_tiles


def _layer(x, p, w_in, w_conv, w_up_a, w_out_b, w_o, g_mix, g_ffn, w_group, b_group, w_router, b_router,
           w_gate, w_up, w_down, g_ple, w_ple_gate, w_ple_proj):
    s, d = x.shape
    cw = w_conv.shape[1]
    n_experts = w_gate.shape[0]
    ng = len(ATTN_DILATIONS)
    a_out = HEADS_PER_GROUP * HEAD_DIM
    q_cols = ng * a_out
    qkv_cols = q_cols + 2 * a_out
    col_u, col_gb, col_gc = 0, cw, 2 * cw
    col_ma = 3 * cw
    col_mb = col_ma + d

    h = _rmsnorm(x, g_mix, BF16, 256)
    rope_c, rope_s = _rope_tables(s)
    qkv = _qkv_proj(h, w_in, rope_c, rope_s, qkv_cols, q_cols + a_out, 1024, 512)
    proj = _mm_cols(h, w_in, qkv_cols, 1024, 512)
    y_a = _dilated_attention(qkv)
    y_b = _short_conv(proj, w_conv, col_u, col_gb, col_gc, 512, 512)
    merged = _merge(y_a, y_b, proj, w_up_a, w_out_b, col_ma, col_mb, 1024, 512)
    x1 = _resid_mm(merged, w_o, x, 1024, 512)

    w_route = jnp.zeros((d, LANES), F32).at[:, :N_EXPERT_GROUPS].set(w_group)
    w_route = w_route.at[:, N_EXPERT_GROUPS:N_EXPERT_GROUPS + n_experts].set(w_router)
    b_route = jnp.zeros((1, LANES), F32).at[0, :N_EXPERT_GROUPS].set(b_group)
    b_route = b_route.at[0, N_EXPERT_GROUPS:N_EXPERT_GROUPS + n_experts].set(b_router)
    h2, eid, wts = _router(x1, g_ffn, w_route, b_route, N_EXPERT_GROUPS, n_experts, 256)
    rank, cnt = _rank(eid, 256)
    counts = cnt[0, :n_experts]
    tm_e = EXPERT_TILE_ROWS
    pstart, meta, n_tiles = _expert_tiles(counts, s * TOP_K, tm_e)
    dest = (pstart[eid[:, :TOP_K]] + rank[:, :TOP_K]).reshape(-1)
    tokens = jnp.repeat(jnp.arange(s, dtype=jnp.int32), TOP_K)
    row_tok = jnp.zeros((n_tiles * tm_e,), jnp.int32).at[dest].set(tokens)
    y = _expert_ffn(h2, row_tok, w_gate, w_up, w_down, meta, n_tiles, tm_e, 256, 512)
    x2, h3 = _combine(x1, y, dest, wts, g_ple, 256)

    return _ple(h3, p, w_ple_gate, w_ple_proj, x2, 1024, 512)


def kernel(x, p, w_in, w_conv, w_up_a, w_out_b, w_o, norm_mix, norm_ffn, w_group, b_group, w_router, b_router,
           w_gate, w_up, w_down, norm_ple, w_ple_gate, w_ple_proj, norm_final):
    b, s, d = x.shape
    depth = w_in.shape[0]
    outs = []
    for bi in range(b):
        xb = x[bi]
        for i in range(depth):
            xb = _layer(xb, p[i, bi], w_in[i], w_conv[i], w_up_a[i], w_out_b[i], w_o[i], norm_mix[i], norm_ffn[i],
                        w_group[i], b_group[i], w_router[i], b_router[i], w_gate[i], w_up[i], w_down[i],
                        norm_ple[i], w_ple_gate[i], w_ple_proj[i])
        outs.append(_rmsnorm(xb, norm_final, x.dtype, 256))
    return jnp.stack(outs)
```

```python
import functools

import jax
import jax.numpy as jnp
from jax import lax
from jax.experimental import pallas as pl
from jax.experimental.pallas import tpu as pltpu

F32 = jnp.float32
BF16 = jnp.bfloat16

HEAD_DIM = 128
ATTN_DILATIONS = (1, 4, 16)
HEADS_PER_GROUP = 4
ROPE_THETA = 10000.0
ATTN_BLOCK = 128
NEG_BIG = -1e30
N_EXPERT_GROUPS = 4
TOP_K = 2
EPS = 1e-6
ROW_BLOCK = 128
LANES = 128

VMEM_LIMIT_BYTES = 56 * 1024 * 1024


def _params(*sem):
    return pltpu.CompilerParams(dimension_semantics=sem, vmem_limit_bytes=VMEM_LIMIT_BYTES)


def _rmsnorm_kernel(x_ref, g_ref, o_ref):
    x = x_ref[...]
    ms = jnp.mean(x * x, axis=-1, keepdims=True)
    o_ref[...] = (x * lax.rsqrt(ms + EPS) * g_ref[...]).astype(o_ref.dtype)


def _rmsnorm(x, gain, out_dtype, tm):
    n, d = x.shape
    tm = min(tm, n)
    return pl.pallas_call(
        _rmsnorm_kernel,
        grid=(n // tm,),
        in_specs=[pl.BlockSpec((tm, d), lambda i: (i, 0)), pl.BlockSpec((1, d), lambda i: (0, 0))],
        out_specs=pl.BlockSpec((tm, d), lambda i: (i, 0)),
        out_shape=jax.ShapeDtypeStruct((n, d), out_dtype),
        compiler_params=_params("arbitrary"),
        name="rmsnorm",
    )(x, gain.reshape(1, d))


def _qkv_proj_kernel(h_ref, w_ref, c_ref, s_ref, o_ref, wb_ref, *, n_rope_tiles):
    j = pl.program_id(0)
    i = pl.program_id(1)

    @pl.when(i == 0)
    def _():
        wb_ref[...] = w_ref[...].astype(BF16)

    acc = jnp.dot(h_ref[...], wb_ref[...], preferred_element_type=F32)
    heads = o_ref.shape[0]

    @pl.when(j < n_rope_tiles)
    def _():
        c = c_ref[...]
        s = s_ref[...]
        for hh in range(heads):
            t = acc[:, hh * HEAD_DIM:(hh + 1) * HEAD_DIM]
            o_ref[hh] = t * c + pltpu.roll(t, HEAD_DIM // 2, 1) * s

    @pl.when(j >= n_rope_tiles)
    def _():
        for hh in range(heads):
            o_ref[hh] = acc[:, hh * HEAD_DIM:(hh + 1) * HEAD_DIM]


def _qkv_proj(h, w, rope_c, rope_s, n_cols, n_rope_cols, tm, tn):
    m, k = h.shape
    tm = min(tm, m)
    assert n_cols % tn == 0 and n_rope_cols % tn == 0 and m % tm == 0 and tn % HEAD_DIM == 0
    heads = tn // HEAD_DIM
    return pl.pallas_call(
        functools.partial(_qkv_proj_kernel, n_rope_tiles=n_rope_cols // tn),
        grid=(n_cols // tn, m // tm),
        in_specs=[
            pl.BlockSpec((tm, k), lambda j, i: (i, 0)),
            pl.BlockSpec((k, tn), lambda j, i: (0, j)),
            pl.BlockSpec((tm, HEAD_DIM), lambda j, i: (i, 0)),
            pl.BlockSpec((tm, HEAD_DIM), lambda j, i: (i, 0)),
        ],
        out_specs=pl.BlockSpec((heads, tm, HEAD_DIM), lambda j, i: (j, i, 0)),
        out_shape=jax.ShapeDtypeStruct((n_cols // HEAD_DIM, m, HEAD_DIM), F32),
        scratch_shapes=[pltpu.VMEM((k, tn), BF16)],
        compiler_params=_params("arbitrary", "arbitrary"),
        name="qkv_proj_rope",
    )(h, w, rope_c, rope_s)


def _mm_kernel(a_ref, w_ref, o_ref, wb_ref):
    i = pl.program_id(1)

    @pl.when(i == 0)
    def _():
        wb_ref[...] = w_ref[...].astype(BF16)

    o_ref[...] = jnp.dot(a_ref[...], wb_ref[...], preferred_element_type=F32).astype(o_ref.dtype)


def _mm_cols(a, w, col0, tm, tn):
    m, k = a.shape
    n = w.shape[1] - col0
    tm = min(tm, m)
    assert n % tn == 0 and col0 % tn == 0 and m % tm == 0
    return pl.pallas_call(
        _mm_kernel,
        grid=(n // tn, m // tm),
        in_specs=[
            pl.BlockSpec((tm, k), lambda j, i: (i, 0)),
            pl.BlockSpec((k, tn), lambda j, i: (0, col0 // tn + j)),
        ],
        out_specs=pl.BlockSpec((tm, tn), lambda j, i: (i, j)),
        out_shape=jax.ShapeDtypeStruct((m, n), BF16),
        scratch_shapes=[pltpu.VMEM((k, tn), BF16)],
        compiler_params=_params("arbitrary", "arbitrary"),
        name="in_proj",
    )(a, w)


ATTN_SUPER = ATTN_BLOCK * max(ATTN_DILATIONS)
ATTN_UNROLL = 16


def _rows(start, size, stride):
    return pl.ds(start, size) if stride == 1 else pl.ds(start, size, stride=stride)


def _dilated_attn_kernel(q0_ref, q1_ref, q2_ref, kc_ref, kp_ref, vc_ref, vp_ref, y_ref,
                         kcat_ref, vcat_ref, o_ref, lse_ref):
    sb = pl.program_id(1)
    sup = ATTN_SUPER
    kcat_ref[pl.ds(0, sup), :] = kp_ref[...]
    kcat_ref[pl.ds(sup, sup), :] = kc_ref[...]
    vcat_ref[pl.ds(0, sup), :] = vp_ref[...]
    vcat_ref[pl.ds(sup, sup), :] = vc_ref[...]
    qi = lax.broadcasted_iota(jnp.int32, (ATTN_BLOCK, 2 * ATTN_BLOCK), 0)
    kj = lax.broadcasted_iota(jnp.int32, (ATTN_BLOCK, 2 * ATTN_BLOCK), 1)
    dist = qi + ATTN_BLOCK - kj
    band = (dist >= 0) & (dist <= ATTN_BLOCK)
    in_cur = kj >= ATTN_BLOCK
    scale = HEAD_DIM ** -0.5

    for g, (q_ref, d) in enumerate(zip((q0_ref, q1_ref, q2_ref), ATTN_DILATIONS)):
        span = ATTN_BLOCK * d
        shift = d.bit_length() - 1

        def unit(u, carry, g=g, q_ref=q_ref, d=d, span=span, shift=shift):
            j = lax.shift_right_logical(u, shift)
            row0 = j * span + (u & (d - 1))
            q = q_ref[_rows(row0, ATTN_BLOCK, d), :].astype(BF16)
            k = kcat_ref[_rows(sup - span + row0, 2 * ATTN_BLOCK, d), :].astype(BF16)
            v = vcat_ref[_rows(sup - span + row0, 2 * ATTN_BLOCK, d), :].astype(BF16)
            s = lax.dot_general(q, k, (((1,), (1,)), ((), ())), preferred_element_type=F32) * scale
            has_prev = (sb > 0) | (j > 0)
            s = jnp.where(band & (in_cur | has_prev), s, NEG_BIG)
            m = jnp.max(s, axis=-1, keepdims=True)
            e = jnp.exp(s - m)
            ssum = jnp.sum(e, axis=-1, keepdims=True)
            p = (e * (1.0 / ssum)).astype(BF16)
            o_ref[g, _rows(row0, ATTN_BLOCK, d), :] = jnp.dot(p, v, preferred_element_type=F32)
            lse_ref[g, _rows(row0, ATTN_BLOCK, d), :] = jnp.broadcast_to(m + jnp.log(ssum), (ATTN_BLOCK, HEAD_DIM))
            return carry

        lax.fori_loop(0, sup // ATTN_BLOCK, unit, 0, unroll=ATTN_UNROLL)

    lse = lse_ref[...]
    m = jnp.max(lse, axis=0)
    e = jnp.exp(lse - m[None])
    w = e / jnp.sum(e, axis=0)[None]
    y_ref[...] = jnp.sum(w * o_ref[...], axis=0).astype(y_ref.dtype)


def _dilated_attention(qkv):
    nh, s, hd = qkv.shape
    ng = len(ATTN_DILATIONS)
    hpg = HEADS_PER_GROUP
    assert nh == (ng + 2) * hpg and hd == HEAD_DIM and s % ATTN_SUPER == 0
    sup = ATTN_SUPER

    def q_spec(g):
        return pl.BlockSpec((None, sup, hd), lambda h, b: (g * hpg + h, b, 0))

    def kv_spec(first_head, prev):
        if prev:
            return pl.BlockSpec((None, sup, hd), lambda h, b: (first_head + h, jnp.maximum(b - 1, 0), 0))
        return pl.BlockSpec((None, sup, hd), lambda h, b: (first_head + h, b, 0))

    k0, v0 = ng * hpg, (ng + 1) * hpg
    return pl.pallas_call(
        _dilated_attn_kernel,
        grid=(hpg, s // sup),
        in_specs=[q_spec(0), q_spec(1), q_spec(2), kv_spec(k0, False), kv_spec(k0, True),
                  kv_spec(v0, False), kv_spec(v0, True)],
        out_specs=pl.BlockSpec((sup, hd), lambda h, b: (b, h)),
        out_shape=jax.ShapeDtypeStruct((s, hpg * hd), BF16),
        scratch_shapes=[pltpu.VMEM((2 * sup, hd), F32), pltpu.VMEM((2 * sup, hd), F32),
                        pltpu.VMEM((ng, sup, hd), F32), pltpu.VMEM((ng, sup, hd), F32)],
        compiler_params=_params("arbitrary", "arbitrary"),
        name="dilated_attention",
    )(qkv, qkv, qkv, qkv, qkv, qkv, qkv)


CONV_HALO = 16


def _conv_kernel(u_ref, gc_ref, gb_ref, up_ref, gcp_ref, w_ref, o_ref):
    i = pl.program_id(0)
    cur = u_ref[...].astype(F32) * gc_ref[...].astype(F32)
    prev = up_ref[...].astype(F32) * gcp_ref[...].astype(F32)
    prev = jnp.where(i == 0, 0.0, prev)
    big = jnp.concatenate([prev, cur], axis=0)
    back1 = pltpu.roll(big, 1, 0)[CONV_HALO:]
    back2 = pltpu.roll(big, 2, 0)[CONV_HALO:]
    w = w_ref[...]
    y = w[0:1] * back2 + w[1:2] * back1 + w[2:3] * cur
    o_ref[...] = (gb_ref[...].astype(F32) * y).astype(o_ref.dtype)


def _short_conv(proj, w_conv, col_u, col_gb, col_gc, tm, tc):
    s = proj.shape[0]
    kw, cw = w_conv.shape
    assert kw == 3
    tm = min(tm, s)
    tc = min(tc, cw)
    assert col_u % tc == 0 and col_gb % tc == 0 and col_gc % tc == 0 and tm % CONV_HALO == 0
    hb = tm // CONV_HALO

    def cur(col):
        return pl.BlockSpec((tm, tc), lambda i, j: (i, col // tc + j))

    def prev(col):
        return pl.BlockSpec((CONV_HALO, tc), lambda i, j: (jnp.maximum(i * hb - 1, 0), col // tc + j))

    return pl.pallas_call(
        _conv_kernel,
        grid=(s // tm, cw // tc),
        in_specs=[cur(col_u), cur(col_gc), cur(col_gb), prev(col_u), prev(col_gc),
                  pl.BlockSpec((kw, tc), lambda i, j: (0, j))],
        out_specs=pl.BlockSpec((tm, tc), lambda i, j: (i, j)),
        out_shape=jax.ShapeDtypeStruct((s, cw), BF16),
        compiler_params=_params("arbitrary", "arbitrary"),
        name="short_conv",
    )(proj, proj, proj, proj, proj, w_conv)


def _merge_kernel(ya_ref, yb_ref, wa_ref, wb_ref, ma_ref, mb_ref, o_ref, wab_ref, wbb_ref):
    i = pl.program_id(1)

    @pl.when(i == 0)
    def _():
        wab_ref[...] = wa_ref[...].astype(BF16)
        wbb_ref[...] = wb_ref[...].astype(BF16)

    a = jnp.dot(ya_ref[...], wab_ref[...], preferred_element_type=F32)
    b = jnp.dot(yb_ref[...], wbb_ref[...], preferred_element_type=F32)
    ga = jax.nn.sigmoid(ma_ref[...].astype(F32))
    gb = jax.nn.sigmoid(mb_ref[...].astype(F32))
    o_ref[...] = (ga * a + gb * b).astype(o_ref.dtype)


def _merge(y_a, y_b, proj, w_a, w_b, col_ma, col_mb, tm, tn):
    m, ka = y_a.shape
    kb = y_b.shape[1]
    n = w_a.shape[1]
    tm = min(tm, m)
    tn = min(tn, n)
    assert col_ma % tn == 0 and col_mb % tn == 0
    return pl.pallas_call(
        _merge_kernel,
        grid=(n // tn, m // tm),
        in_specs=[
            pl.BlockSpec((tm, ka), lambda j, i: (i, 0)),
            pl.BlockSpec((tm, kb), lambda j, i: (i, 0)),
            pl.BlockSpec((ka, tn), lambda j, i: (0, j)),
            pl.BlockSpec((kb, tn), lambda j, i: (0, j)),
            pl.BlockSpec((tm, tn), lambda j, i: (i, col_ma // tn + j)),
            pl.BlockSpec((tm, tn), lambda j, i: (i, col_mb // tn + j)),
        ],
        out_specs=pl.BlockSpec((tm, tn), lambda j, i: (i, j)),
        out_shape=jax.ShapeDtypeStruct((m, n), BF16),
        scratch_shapes=[pltpu.VMEM((ka, tn), BF16), pltpu.VMEM((kb, tn), BF16)],
        compiler_params=_params("arbitrary", "arbitrary"),
        name="mixer_merge",
    )(y_a, y_b, w_a, w_b, proj, proj)


def _resid_mm_kernel(a_ref, w_ref, x_ref, o_ref, wb_ref):
    i = pl.program_id(1)

    @pl.when(i == 0)
    def _():
        wb_ref[...] = w_ref[...].astype(BF16)

    o_ref[...] = x_ref[...] + jnp.dot(a_ref[...], wb_ref[...], preferred_element_type=F32)


def _resid_mm(a, w, x, tm, tn):
    m, k = a.shape
    n = w.shape[1]
    tm = min(tm, m)
    tn = min(tn, n)
    return pl.pallas_call(
        _resid_mm_kernel,
        grid=(n // tn, m // tm),
        in_specs=[
            pl.BlockSpec((tm, k), lambda j, i: (i, 0)),
            pl.BlockSpec((k, tn), lambda j, i: (0, j)),
            pl.BlockSpec((tm, tn), lambda j, i: (i, j)),
        ],
        out_specs=pl.BlockSpec((tm, tn), lambda j, i: (i, j)),
        out_shape=jax.ShapeDtypeStruct((m, n), F32),
        scratch_shapes=[pltpu.VMEM((k, tn), BF16)],
        compiler_params=_params("arbitrary", "arbitrary"),
        name="out_proj_residual",
    )(a, w, x)


def _split_bf16(a):
    hi = a.astype(BF16)
    lo = (a - hi.astype(F32)).astype(BF16)
    return hi, lo


def _router_kernel(x_ref, g_ref, w_ref, b_ref, h_ref, eid_ref, wt_ref, *, n_groups, n_experts):
    x = x_ref[...]
    ms = jnp.mean(x * x, axis=-1, keepdims=True)
    h = x * lax.rsqrt(ms + EPS) * g_ref[...]
    h_ref[...] = h
    hh, hl = _split_bf16(h)
    wh, wl = _split_bf16(w_ref[...])
    logits = (jnp.dot(hh, wh, preferred_element_type=F32) + jnp.dot(hh, wl, preferred_element_type=F32)
              + jnp.dot(hl, wh, preferred_element_type=F32)) + b_ref[...]
    tm = x.shape[0]
    per_group = n_experts // n_groups
    lane = lax.broadcasted_iota(jnp.int32, (tm, LANES), 1)
    is_g = lane < n_groups
    gl = jnp.where(is_g, logits, NEG_BIG)
    gmax = jnp.max(gl, axis=-1, keepdims=True)
    gidx = jnp.min(jnp.where(gl == gmax, lane, LANES), axis=-1, keepdims=True)
    gsum = jnp.sum(jnp.where(is_g, jnp.exp(gl - gmax), 0.0), axis=-1, keepdims=True)
    g_w = 1.0 / gsum
    lane_e = lane - n_groups
    sel = (lane_e >= 0) & (lane_e < n_experts) & (lane_e // per_group == gidx)
    el = jnp.where(sel, logits, NEG_BIG)
    emax = jnp.max(el, axis=-1, keepdims=True)
    i1 = jnp.min(jnp.where(el == emax, lane, LANES), axis=-1, keepdims=True)
    esum = jnp.sum(jnp.where(sel, jnp.exp(el - emax), 0.0), axis=-1, keepdims=True)
    el2 = jnp.where(lane == i1, NEG_BIG, el)
    emax2 = jnp.max(el2, axis=-1, keepdims=True)
    i2 = jnp.min(jnp.where(el2 == emax2, lane, LANES), axis=-1, keepdims=True)
    p1 = 1.0 / esum
    p2 = jnp.exp(emax2 - emax) / esum
    den = p1 + p2
    eid_ref[...] = jnp.where(lane == 0, i1 - n_groups, jnp.where(lane == 1, i2 - n_groups, 0))
    wt_ref[...] = jnp.where(lane == 0, g_w * (p1 / den), jnp.where(lane == 1, g_w * (p2 / den), 0.0))


def _router(x, gain, w_route, b_route, n_groups, n_experts, tm):
    t, d = x.shape
    tm = min(tm, t)
    row = pl.BlockSpec((tm, d), lambda i: (i, 0))
    narrow = pl.BlockSpec((tm, LANES), lambda i: (i, 0))
    return pl.pallas_call(
        functools.partial(_router_kernel, n_groups=n_groups, n_experts=n_experts),
        grid=(t // tm,),
        in_specs=[row, pl.BlockSpec((1, d), lambda i: (0, 0)), pl.BlockSpec((d, LANES), lambda i: (0, 0)),
                  pl.BlockSpec((1, LANES), lambda i: (0, 0))],
        out_specs=[row, narrow, narrow],
        out_shape=[jax.ShapeDtypeStruct((t, d), F32), jax.ShapeDtypeStruct((t, LANES), jnp.int32),
                   jax.ShapeDtypeStruct((t, LANES), F32)],
        compiler_params=_params("arbitrary"),
        name="ffn_norm_router",
    )(x, gain.reshape(1, d), w_route, b_route)


def _rank_kernel(eid_ref, rank_ref, cnt_ref, carry_ref):
    i = pl.program_id(0)

    @pl.when(i == 0)
    def _():
        carry_ref[...] = jnp.zeros_like(carry_ref)

    eid = eid_ref[...]
    tm = eid.shape[0]
    lane = lax.broadcasted_iota(jnp.int32, (tm, LANES), 1)
    oh0 = lane == eid[:, 0:1]
    oh1 = lane == eid[:, 1:2]
    ohs = oh0.astype(F32) + oh1.astype(F32)
    r = lax.broadcasted_iota(jnp.int32, (tm, tm), 0)
    c = lax.broadcasted_iota(jnp.int32, (tm, tm), 1)
    earlier = (c < r).astype(BF16)
    before = jnp.dot(earlier, ohs.astype(BF16), preferred_element_type=F32) + carry_ref[...]
    rank0 = jnp.sum(jnp.where(oh0, before, 0.0), axis=-1, keepdims=True)
    rank1 = jnp.sum(jnp.where(oh1, before, 0.0), axis=-1, keepdims=True)
    rank_ref[...] = jnp.where(lane == 0, rank0, jnp.where(lane == 1, rank1, 0.0)).astype(jnp.int32)
    total = carry_ref[...] + jnp.sum(ohs, axis=0, keepdims=True)
    carry_ref[...] = total
    cnt_ref[...] = jnp.broadcast_to(total, cnt_ref.shape).astype(jnp.int32)


def _rank(eid, tm):
    t = eid.shape[0]
    tm = min(tm, t)
    narrow = pl.BlockSpec((tm, LANES), lambda i: (i, 0))
    return pl.pallas_call(
        _rank_kernel,
        grid=(t // tm,),
        in_specs=[narrow],
        out_specs=[narrow, pl.BlockSpec((8, LANES), lambda i: (0, 0))],
        out_shape=[jax.ShapeDtypeStruct((t, LANES), jnp.int32), jax.ShapeDtypeStruct((8, LANES), jnp.int32)],
        scratch_shapes=[pltpu.VMEM((1, LANES), F32)],
        compiler_params=_params("arbitrary"),
        name="expert_rank",
    )(eid)


def _dest_kernel(eid_ref, rank_ref, ps_ref, o_ref):
    eid = eid_ref[...]
    lane = lax.broadcasted_iota(jnp.int32, eid.shape, 1)
    ps = ps_ref[...]
    s0 = jnp.sum(jnp.where(lane == eid[:, 0:1], ps, 0.0), axis=-1, keepdims=True)
    s1 = jnp.sum(jnp.where(lane == eid[:, 1:2], ps, 0.0), axis=-1, keepdims=True)
    start = jnp.where(lane == 0, s0, jnp.where(lane == 1, s1, 0.0))
    o_ref[...] = start.astype(jnp.int32) + rank_ref[...]


def _dest_rows(eid, rank, pstart, tm):
    t = eid.shape[0]
    tm = min(tm, t)
    ps = jnp.zeros((1, LANES), F32).at[0, :pstart.shape[0]].set(pstart.astype(F32))
    narrow = pl.BlockSpec((tm, LANES), lambda i: (i, 0))
    return pl.pallas_call(
        _dest_kernel,
        grid=(t // tm,),
        in_specs=[narrow, narrow, pl.BlockSpec((1, LANES), lambda i: (0, 0))],
        out_specs=narrow,
        out_shape=jax.ShapeDtypeStruct((t, LANES), jnp.int32),
        compiler_params=_params("arbitrary"),
        name="expert_dest_rows",
    )(eid, rank, ps)


GATHER_UNROLL = 8


def _expert_ffn_kernel(te_ref, trb_ref, nsub_ref, ntot_ref, tok_ref, tok_next_ref, h_hbm, wg_ref, wu_ref, wd_ref,
                       y_ref, land_ref, hid_ref, sem, *, tm, n_up):
    t = pl.program_id(0)
    s = pl.program_id(1)
    n_tiles = pl.num_programs(0)
    nsub = nsub_ref[t]
    slot = t & 1

    def row_copy(tok, r, sl):
        return pltpu.make_async_copy(h_hbm.at[pl.ds(tok, 1)], land_ref.at[sl, pl.ds(r, 1)], sem.at[sl])

    def start_gather(toks_ref, blocks, sl):
        def issue(i, carry):
            for u in range(GATHER_UNROLL):
                r = i * GATHER_UNROLL + u
                row_copy(toks_ref[0, 0, r], r, sl).start()
            return carry

        lax.fori_loop(0, blocks * (ROW_BLOCK // GATHER_UNROLL), issue, 0)

    @pl.when(nsub > 0)
    def _():
        @pl.when(s == 0)
        def _():
            @pl.when(t == 0)
            def _():
                start_gather(tok_ref, nsub, slot)

            def drain(i, carry):
                for u in range(GATHER_UNROLL):
                    row_copy(0, 0, slot).wait()
                return carry

            lax.fori_loop(0, nsub * (ROW_BLOCK // GATHER_UNROLL), drain, 0)

        @pl.when(s == 1)
        def _():
            nxt = jnp.minimum(t + 1, n_tiles - 1)
            n_next = jnp.where(t + 1 < n_tiles, nsub_ref[nxt], 0)
            start_gather(tok_next_ref, n_next, 1 - slot)

        for n in range(1, tm // ROW_BLOCK + 1):
            rows = n * ROW_BLOCK

            @pl.when((nsub == n) & (s < n_up))
            def _(rows=rows):
                x = land_ref[slot, pl.ds(0, rows), :]
                g = jnp.dot(x, wg_ref[...], preferred_element_type=F32)
                u = jnp.dot(x, wu_ref[...], preferred_element_type=F32)
                hid_ref[s, pl.ds(0, rows), :] = g * jax.nn.sigmoid(g) * u

            @pl.when((nsub == n) & (s >= n_up))
            def _(rows=rows):
                hid = jnp.concatenate([hid_ref[c, pl.ds(0, rows), :] for c in range(n_up)], axis=1)
                y_ref[pl.ds(0, rows), :] = jnp.dot(hid, wd_ref[...], preferred_element_type=F32)
                if rows < tm:
                    y_ref[pl.ds(rows, tm - rows), :] = jnp.zeros((tm - rows, y_ref.shape[1]), y_ref.dtype)


def _expert_ffn(h, row_tok, w_gate, w_up, w_down, meta, n_tiles, tm, tf, tn):
    d = h.shape[1]
    f = w_gate.shape[2]
    tf = min(tf, f)
    tn = min(tn, d)
    n_up = f // tf
    n_down = d // tn
    n_steps = n_up + n_down

    def step(t, s, ntot):
        return jnp.where(t < ntot[0], s, n_steps - 1)

    def up_idx(t, s, te, trb, ns, ntot):
        return (te[t], 0, jnp.minimum(step(t, s, ntot), n_up - 1))

    def down_col(t, s, ntot):
        return jnp.maximum(step(t, s, ntot) - n_up, 0)

    grid_spec = pltpu.PrefetchScalarGridSpec(
        num_scalar_prefetch=4,
        grid=(n_tiles, n_steps),
        in_specs=[
            pl.BlockSpec((1, 1, tm), lambda t, s, te, trb, ns, ntot: (trb[t], 0, 0), memory_space=pltpu.SMEM),
            pl.BlockSpec((1, 1, tm), lambda t, s, te, trb, ns, ntot: (trb[jnp.minimum(t + 1, n_tiles - 1)], 0, 0),
                         memory_space=pltpu.SMEM),
            pl.BlockSpec(memory_space=pl.ANY),
            pl.BlockSpec((None, d, tf), up_idx),
            pl.BlockSpec((None, d, tf), up_idx),
            pl.BlockSpec((None, f, tn), lambda t, s, te, trb, ns, ntot: (te[t], 0, down_col(t, s, ntot))),
        ],
        out_specs=pl.BlockSpec((tm, tn), lambda t, s, te, trb, ns, ntot: (trb[t], down_col(t, s, ntot))),
        scratch_shapes=[
            pltpu.VMEM((2, tm, d), F32),
            pltpu.VMEM((n_up, tm, tf), F32),
            pltpu.SemaphoreType.DMA((2,)),
        ],
    )
    return pl.pallas_call(
        functools.partial(_expert_ffn_kernel, tm=tm, n_up=n_up),
        grid_spec=grid_spec,
        out_shape=jax.ShapeDtypeStruct((n_tiles * tm, d), F32),
        compiler_params=_params("arbitrary", "arbitrary"),
        name="expert_ffn",
    )(*meta, row_tok.reshape(n_tiles, 1, tm), row_tok.reshape(n_tiles, 1, tm), h, w_gate, w_up, w_down)


def _combine_kernel(dest_ref, dest_next_ref, wt_ref, x_ref, g_ref, y_hbm, x2_ref, h3_ref, buf_ref, sem, *, tb):
    i = pl.program_id(0)
    slot = i & 1

    def row_copy(src, k, r, sl):
        return pltpu.make_async_copy(y_hbm.at[pl.ds(src, 1)], buf_ref.at[sl, k, pl.ds(r, 1)], sem.at[sl])

    def start_gather(rows_ref, sl):
        def issue(r, carry):
            for k in range(TOP_K):
                row_copy(rows_ref[0, 0, TOP_K * r + k], k, r, sl).start()
            return carry

        lax.fori_loop(0, tb, issue, 0, unroll=GATHER_UNROLL // TOP_K)

    @pl.when(i == 0)
    def _():
        start_gather(dest_ref, slot)

    @pl.when(i + 1 < pl.num_programs(0))
    def _():
        start_gather(dest_next_ref, 1 - slot)

    def drain(r, carry):
        for k in range(TOP_K):
            row_copy(0, k, r, slot).wait()
        return carry

    lax.fori_loop(0, tb, drain, 0, unroll=GATHER_UNROLL // TOP_K)

    wt = wt_ref[...]
    x2 = x_ref[...] + (wt[:, 0:1] * buf_ref[slot, 0] + wt[:, 1:2] * buf_ref[slot, 1])
    x2_ref[...] = x2
    ms = jnp.mean(x2 * x2, axis=-1, keepdims=True)
    h3_ref[...] = (x2 * lax.rsqrt(ms + EPS) * g_ref[...]).astype(h3_ref.dtype)


def _combine(x, y, dest, wts, gain, tb):
    t, d = x.shape
    tb = min(tb, t)
    nb = t // tb
    dest3 = dest.reshape(nb, 1, TOP_K * tb)
    row = pl.BlockSpec((tb, d), lambda i: (i, 0))
    return pl.pallas_call(
        functools.partial(_combine_kernel, tb=tb),
        grid=(nb,),
        in_specs=[
            pl.BlockSpec((1, 1, TOP_K * tb), lambda i: (i, 0, 0), memory_space=pltpu.SMEM),
            pl.BlockSpec((1, 1, TOP_K * tb), lambda i: (jnp.minimum(i + 1, nb - 1), 0, 0), memory_space=pltpu.SMEM),
            pl.BlockSpec((tb, LANES), lambda i: (i, 0)),
            row,
            pl.BlockSpec((1, d), lambda i: (0, 0)),
            pl.BlockSpec(memory_space=pl.ANY),
        ],
        out_specs=[row, row],
        out_shape=[jax.ShapeDtypeStruct((t, d), F32), jax.ShapeDtypeStruct((t, d), BF16)],
        scratch_shapes=[pltpu.VMEM((2, TOP_K, tb, d), F32), pltpu.SemaphoreType.DMA((2,))],
        compiler_params=_params("arbitrary"),
        name="moe_combine_norm",
    )(dest3, dest3, wts, x, gain.reshape(1, d), y)


def _ple_kernel(h_ref, p_ref, wg_ref, wp_ref, x_ref, o_ref, wgb_ref, wpb_ref):
    i = pl.program_id(1)

    @pl.when(i == 0)
    def _():
        wgb_ref[...] = wg_ref[...].astype(BF16)
        wpb_ref[...] = wp_ref[...].astype(BF16)

    a = jnp.dot(h_ref[...], wgb_ref[...], preferred_element_type=F32)
    b = jnp.dot(p_ref[...].astype(BF16), wpb_ref[...], preferred_element_type=F32)
    o_ref[...] = x_ref[...] + jax.nn.sigmoid(a) * b


def _ple(h, p, w_gate, w_proj, x, tm, tn):
    m, k = h.shape
    kp = p.shape[1]
    n = w_gate.shape[1]
    tm = min(tm, m)
    tn = min(tn, n)
    return pl.pallas_call(
        _ple_kernel,
        grid=(n // tn, m // tm),
        in_specs=[
            pl.BlockSpec((tm, k), lambda j, i: (i, 0)),
            pl.BlockSpec((tm, kp), lambda j, i: (i, 0)),
            pl.BlockSpec((k, tn), lambda j, i: (0, j)),
            pl.BlockSpec((kp, tn), lambda j, i: (0, j)),
            pl.BlockSpec((tm, tn), lambda j, i: (i, j)),
        ],
        out_specs=pl.BlockSpec((tm, tn), lambda j, i: (i, j)),
        out_shape=jax.ShapeDtypeStruct((m, n), F32),
        scratch_shapes=[pltpu.VMEM((k, tn), BF16), pltpu.VMEM((kp, tn), BF16)],
        compiler_params=_params("arbitrary", "arbitrary"),
        name="ple_gate_residual",
    )(h, p, w_gate, w_proj, x)


EXPERT_TILE_ROWS = 640


def _rope_tables(s):
    half = HEAD_DIM // 2
    inv_freq = ROPE_THETA ** (-jnp.arange(half, dtype=F32) / half)
    ang = jnp.arange(s, dtype=jnp.int32).astype(F32)[:, None] * inv_freq[None, :]
    cos, sin = jnp.cos(ang), jnp.sin(ang)
    return jnp.concatenate([cos, cos], axis=-1), jnp.concatenate([-sin, sin], axis=-1)


def _expert_tiles(counts, n_assign, tm):
    n_experts = counts.shape[0]
    n_tiles = -(-n_assign // tm) + n_experts
    tiles_e = (counts + tm - 1) // tm
    tcum = jnp.cumsum(tiles_e)
    tstart = tcum - tiles_e
    pstart = tstart * tm
    total = tcum[-1]
    t = jnp.arange(n_tiles, dtype=jnp.int32)
    tc = jnp.minimum(t, total - 1)
    te = jnp.minimum(jnp.searchsorted(tcum, tc, side="right"), n_experts - 1).astype(jnp.int32)
    local = tc - tstart[te]
    rows_left = counts[te] - local * tm
    nsub = jnp.clip((rows_left + ROW_BLOCK - 1) // ROW_BLOCK, 0, tm // ROW_BLOCK)
    nsub = jnp.where(t < total, nsub, 0).astype(jnp.int32)
    trb = (tstart[te] + local).astype(jnp.int32)
    meta = (te, trb, nsub, jnp.reshape(total, (1,)).astype(jnp.int32))
    return pstart.astype(jnp.int32), meta, n_tiles


def _layer(x, p, w_in, w_conv, w_up_a, w_out_b, w_o, g_mix, g_ffn, w_group, b_group, w_router, b_router,
           w_gate, w_up, w_down, g_ple, w_ple_gate, w_ple_proj):
    s, d = x.shape
    cw = w_conv.shape[1]
    n_experts = w_gate.shape[0]
    ng = len(ATTN_DILATIONS)
    a_out = HEADS_PER_GROUP * HEAD_DIM
    q_cols = ng * a_out
    qkv_cols = q_cols + 2 * a_out
    col_u, col_gb, col_gc = 0, cw, 2 * cw
    col_ma = 3 * cw
    col_mb = col_ma + d

    h = _rmsnorm(x, g_mix, BF16, 256)
    rope_c, rope_s = _rope_tables(s)
    qkv = _qkv_proj(h, w_in, rope_c, rope_s, qkv_cols, q_cols + a_out, 1024, 512)
    proj = _mm_cols(h, w_in, qkv_cols, 1024, 512)
    y_a = _dilated_attention(qkv)
    y_b = _short_conv(proj, w_conv, col_u, col_gb, col_gc, 512, 512)
    merged = _merge(y_a, y_b, proj, w_up_a, w_out_b, col_ma, col_mb, 1024, 512)
    x1 = _resid_mm(merged, w_o, x, 1024, 512)

    n_route = N_EXPERT_GROUPS + n_experts
    w_route = jnp.pad(jnp.concatenate([w_group, w_router], axis=1), ((0, 0), (0, LANES - n_route)))
    b_route = jnp.pad(jnp.concatenate([b_group, b_router]), (0, LANES - n_route)).reshape(1, LANES)
    h2, eid, wts = _router(x1, g_ffn, w_route, b_route, N_EXPERT_GROUPS, n_experts, 256)
    rank, cnt = _rank(eid, 256)
    counts = cnt[0, :n_experts]
    tm_e = EXPERT_TILE_ROWS
    pstart, meta, n_tiles = _expert_tiles(counts, s * TOP_K, tm_e)
    dest = _dest_rows(eid, rank, pstart, 1024)[:, :TOP_K].reshape(-1)
    tokens = jnp.repeat(jnp.arange(s, dtype=jnp.int32), TOP_K)
    row_tok = jnp.zeros((n_tiles * tm_e,), jnp.int32).at[dest].set(tokens)
    y = _expert_ffn(h2, row_tok, w_gate, w_up, w_down, meta, n_tiles, tm_e, 256, 512)
    x2, h3 = _combine(x1, y, dest, wts, g_ple, 256)

    return _ple(h3, p, w_ple_gate, w_ple_proj, x2, 1024, 512)


def kernel(x, p, w_in, w_conv, w_up_a, w_out_b, w_o, norm_mix, norm_ffn, w_group, b_group, w_router, b_router,
           w_gate, w_up, w_down, norm_ple, w_ple_gate, w_ple_proj, norm_final):
    b, s, d = x.shape
    depth = w_in.shape[0]
    outs = []
    for bi in range(b):
        xb = x[bi]
        for i in range(depth):
            xb = _layer(xb, p[i, bi], w_in[i], w_conv[i], w_up_a[i], w_out_b[i], w_o[i], norm_mix[i], norm_ffn[i],
                        w_group[i], b_group[i], w_router[i], b_router[i], w_gate[i], w_up[i], w_down[i],
                        norm_ple[i], w_ple_gate[i], w_ple_proj[i])
        outs.append(_rmsnorm(xb, norm_final, x.dtype, 256))
    return jnp.stack(outs)
```

```python
import functools

import jax
import jax.numpy as jnp
from jax import lax
from jax.experimental import pallas as pl
from jax.experimental.pallas import tpu as pltpu

F32 = jnp.float32
BF16 = jnp.bfloat16

HEAD_DIM = 128
ATTN_DILATIONS = (1, 4, 16)
HEADS_PER_GROUP = 4
ROPE_THETA = 10000.0
ATTN_BLOCK = 128
NEG_BIG = -1e30
N_EXPERT_GROUPS = 4
TOP_K = 2
EPS = 1e-6
ROW_BLOCK = 128
LANES = 128

VMEM_LIMIT_BYTES = 56 * 1024 * 1024


def _params(*sem):
    return pltpu.CompilerParams(dimension_semantics=sem, vmem_limit_bytes=VMEM_LIMIT_BYTES)


ACT_SPLIT = 4


def _col_piece_specs(tm, k, row_of):
    dk = k // ACT_SPLIT
    assert dk * ACT_SPLIT == k and dk % LANES == 0
    return [pl.BlockSpec((tm, dk), lambda j, i, c=c: (row_of(j, i), c)) for c in range(ACT_SPLIT)]


def _join_cols(refs):
    return jnp.concatenate([r[...] for r in refs], axis=1)


def _rmsnorm_kernel(x_ref, g_ref, o_ref):
    x = x_ref[...]
    ms = jnp.mean(x * x, axis=-1, keepdims=True)
    o_ref[...] = (x * lax.rsqrt(ms + EPS) * g_ref[...]).astype(o_ref.dtype)


def _rmsnorm(x, gain, out_dtype, tm):
    n, d = x.shape
    tm = min(tm, n)
    return pl.pallas_call(
        _rmsnorm_kernel,
        grid=(n // tm,),
        in_specs=[pl.BlockSpec((tm, d), lambda i: (i, 0)), pl.BlockSpec((1, d), lambda i: (0, 0))],
        out_specs=pl.BlockSpec((tm, d), lambda i: (i, 0)),
        out_shape=jax.ShapeDtypeStruct((n, d), out_dtype),
        compiler_params=_params("arbitrary"),
        name="rmsnorm",
    )(x, gain.reshape(1, d))


def _qkv_proj_kernel(*refs, n_rope_tiles):
    h_refs = refs[:ACT_SPLIT]
    w_ref, c_ref, s_ref, o_ref, wb_ref = refs[ACT_SPLIT:]
    j = pl.program_id(0)
    i = pl.program_id(1)

    @pl.when(i == 0)
    def _():
        wb_ref[...] = w_ref[...].astype(BF16)

    acc = jnp.dot(_join_cols(h_refs), wb_ref[...], preferred_element_type=F32)
    heads = o_ref.shape[0]

    @pl.when(j < n_rope_tiles)
    def _():
        c = c_ref[...]
        s = s_ref[...]
        for hh in range(heads):
            t = acc[:, hh * HEAD_DIM:(hh + 1) * HEAD_DIM]
            o_ref[hh] = t * c + pltpu.roll(t, HEAD_DIM // 2, 1) * s

    @pl.when(j >= n_rope_tiles)
    def _():
        for hh in range(heads):
            o_ref[hh] = acc[:, hh * HEAD_DIM:(hh + 1) * HEAD_DIM]


def _qkv_proj(h, w, rope_c, rope_s, n_cols, n_rope_cols, tm, tn):
    m, k = h.shape
    tm = min(tm, m)
    assert n_cols % tn == 0 and n_rope_cols % tn == 0 and m % tm == 0 and tn % HEAD_DIM == 0
    heads = tn // HEAD_DIM
    return pl.pallas_call(
        functools.partial(_qkv_proj_kernel, n_rope_tiles=n_rope_cols // tn),
        grid=(n_cols // tn, m // tm),
        in_specs=_col_piece_specs(tm, k, lambda j, i: i) + [
            pl.BlockSpec((k, tn), lambda j, i: (0, j)),
            pl.BlockSpec((tm, HEAD_DIM), lambda j, i: (i, 0)),
            pl.BlockSpec((tm, HEAD_DIM), lambda j, i: (i, 0)),
        ],
        out_specs=pl.BlockSpec((heads, tm, HEAD_DIM), lambda j, i: (j, i, 0)),
        out_shape=jax.ShapeDtypeStruct((n_cols // HEAD_DIM, m, HEAD_DIM), F32),
        scratch_shapes=[pltpu.VMEM((k, tn), BF16)],
        compiler_params=_params("arbitrary", "arbitrary"),
        name="qkv_proj_rope",
    )(*([h] * ACT_SPLIT), w, rope_c, rope_s)


def _mm_kernel(*refs):
    a_refs = refs[:ACT_SPLIT]
    w_ref, o_ref, wb_ref = refs[ACT_SPLIT:]
    i = pl.program_id(1)

    @pl.when(i == 0)
    def _():
        wb_ref[...] = w_ref[...].astype(BF16)

    o_ref[...] = jnp.dot(_join_cols(a_refs), wb_ref[...], preferred_element_type=F32).astype(o_ref.dtype)


def _mm_cols(a, w, col0, tm, tn):
    m, k = a.shape
    n = w.shape[1] - col0
    tm = min(tm, m)
    assert n % tn == 0 and col0 % tn == 0 and m % tm == 0
    return pl.pallas_call(
        _mm_kernel,
        grid=(n // tn, m // tm),
        in_specs=_col_piece_specs(tm, k, lambda j, i: i) + [
            pl.BlockSpec((k, tn), lambda j, i: (0, col0 // tn + j)),
        ],
        out_specs=pl.BlockSpec((tm, tn), lambda j, i: (i, j)),
        out_shape=jax.ShapeDtypeStruct((m, n), BF16),
        scratch_shapes=[pltpu.VMEM((k, tn), BF16)],
        compiler_params=_params("arbitrary", "arbitrary"),
        name="in_proj",
    )(*([a] * ACT_SPLIT), w)


ATTN_SUPER = ATTN_BLOCK * max(ATTN_DILATIONS)
ATTN_UNROLL = 16


def _rows(start, size, stride):
    return pl.ds(start, size) if stride == 1 else pl.ds(start, size, stride=stride)


def _dilated_attn_kernel(q0_ref, q1_ref, q2_ref, kc_ref, kp_ref, vc_ref, vp_ref, y_ref,
                         kcat_ref, vcat_ref, o_ref, lse_ref):
    sb = pl.program_id(1)
    sup = ATTN_SUPER
    kcat_ref[pl.ds(0, sup), :] = kp_ref[...]
    kcat_ref[pl.ds(sup, sup), :] = kc_ref[...]
    vcat_ref[pl.ds(0, sup), :] = vp_ref[...]
    vcat_ref[pl.ds(sup, sup), :] = vc_ref[...]
    qi = lax.broadcasted_iota(jnp.int32, (ATTN_BLOCK, 2 * ATTN_BLOCK), 0)
    kj = lax.broadcasted_iota(jnp.int32, (ATTN_BLOCK, 2 * ATTN_BLOCK), 1)
    dist = qi + ATTN_BLOCK - kj
    band = (dist >= 0) & (dist <= ATTN_BLOCK)
    in_cur = kj >= ATTN_BLOCK
    scale = HEAD_DIM ** -0.5

    for g, (q_ref, d) in enumerate(zip((q0_ref, q1_ref, q2_ref), ATTN_DILATIONS)):
        span = ATTN_BLOCK * d
        shift = d.bit_length() - 1

        def unit(u, carry, g=g, q_ref=q_ref, d=d, span=span, shift=shift):
            j = lax.shift_right_logical(u, shift)
            row0 = j * span + (u & (d - 1))
            q = q_ref[_rows(row0, ATTN_BLOCK, d), :].astype(BF16)
            k = kcat_ref[_rows(sup - span + row0, 2 * ATTN_BLOCK, d), :].astype(BF16)
            v = vcat_ref[_rows(sup - span + row0, 2 * ATTN_BLOCK, d), :].astype(BF16)
            s = lax.dot_general(q, k, (((1,), (1,)), ((), ())), preferred_element_type=F32) * scale
            has_prev = (sb > 0) | (j > 0)
            s = jnp.where(band & (in_cur | has_prev), s, NEG_BIG)
            m = jnp.max(s, axis=-1, keepdims=True)
            e = jnp.exp(s - m)
            ssum = jnp.sum(e, axis=-1, keepdims=True)
            p = (e * (1.0 / ssum)).astype(BF16)
            o_ref[g, _rows(row0, ATTN_BLOCK, d), :] = jnp.dot(p, v, preferred_element_type=F32)
            lse_ref[g, _rows(row0, ATTN_BLOCK, d), :] = jnp.broadcast_to(m + jnp.log(ssum), (ATTN_BLOCK, HEAD_DIM))
            return carry

        lax.fori_loop(0, sup // ATTN_BLOCK, unit, 0, unroll=ATTN_UNROLL)

    lse = lse_ref[...]
    m = jnp.max(lse, axis=0)
    e = jnp.exp(lse - m[None])
    w = e / jnp.sum(e, axis=0)[None]
    y_ref[...] = jnp.sum(w * o_ref[...], axis=0).astype(y_ref.dtype)


def _dilated_attention(qkv):
    nh, s, hd = qkv.shape
    ng = len(ATTN_DILATIONS)
    hpg = HEADS_PER_GROUP
    assert nh == (ng + 2) * hpg and hd == HEAD_DIM and s % ATTN_SUPER == 0
    sup = ATTN_SUPER

    def q_spec(g):
        return pl.BlockSpec((None, sup, hd), lambda h, b: (g * hpg + h, b, 0))

    def kv_spec(first_head, prev):
        if prev:
            return pl.BlockSpec((None, sup, hd), lambda h, b: (first_head + h, jnp.maximum(b - 1, 0), 0))
        return pl.BlockSpec((None, sup, hd), lambda h, b: (first_head + h, b, 0))

    k0, v0 = ng * hpg, (ng + 1) * hpg
    return pl.pallas_call(
        _dilated_attn_kernel,
        grid=(hpg, s // sup),
        in_specs=[q_spec(0), q_spec(1), q_spec(2), kv_spec(k0, False), kv_spec(k0, True),
                  kv_spec(v0, False), kv_spec(v0, True)],
        out_specs=pl.BlockSpec((sup, hd), lambda h, b: (b, h)),
        out_shape=jax.ShapeDtypeStruct((s, hpg * hd), BF16),
        scratch_shapes=[pltpu.VMEM((2 * sup, hd), F32), pltpu.VMEM((2 * sup, hd), F32),
                        pltpu.VMEM((ng, sup, hd), F32), pltpu.VMEM((ng, sup, hd), F32)],
        compiler_params=_params("arbitrary", "arbitrary"),
        name="dilated_attention",
    )(qkv, qkv, qkv, qkv, qkv, qkv, qkv)


CONV_HALO = 16


def _conv_kernel(u_ref, gc_ref, gb_ref, up_ref, gcp_ref, w_ref, o_ref):
    i = pl.program_id(0)
    cur = u_ref[...].astype(F32) * gc_ref[...].astype(F32)
    prev = up_ref[...].astype(F32) * gcp_ref[...].astype(F32)
    prev = jnp.where(i == 0, 0.0, prev)
    big = jnp.concatenate([prev, cur], axis=0)
    back1 = pltpu.roll(big, 1, 0)[CONV_HALO:]
    back2 = pltpu.roll(big, 2, 0)[CONV_HALO:]
    w = w_ref[...]
    y = w[0:1] * back2 + w[1:2] * back1 + w[2:3] * cur
    o_ref[...] = (gb_ref[...].astype(F32) * y).astype(o_ref.dtype)


def _short_conv(proj, w_conv, col_u, col_gb, col_gc, tm, tc):
    s = proj.shape[0]
    kw, cw = w_conv.shape
    assert kw == 3
    tm = min(tm, s)
    tc = min(tc, cw)
    assert col_u % tc == 0 and col_gb % tc == 0 and col_gc % tc == 0 and tm % CONV_HALO == 0
    hb = tm // CONV_HALO

    def cur(col):
        return pl.BlockSpec((tm, tc), lambda i, j: (i, col // tc + j))

    def prev(col):
        return pl.BlockSpec((CONV_HALO, tc), lambda i, j: (jnp.maximum(i * hb - 1, 0), col // tc + j))

    return pl.pallas_call(
        _conv_kernel,
        grid=(s // tm, cw // tc),
        in_specs=[cur(col_u), cur(col_gc), cur(col_gb), prev(col_u), prev(col_gc),
                  pl.BlockSpec((kw, tc), lambda i, j: (0, j))],
        out_specs=pl.BlockSpec((tm, tc), lambda i, j: (i, j)),
        out_shape=jax.ShapeDtypeStruct((s, cw), BF16),
        compiler_params=_params("arbitrary", "arbitrary"),
        name="short_conv",
    )(proj, proj, proj, proj, proj, w_conv)


def _merge_kernel(*refs):
    yb_refs = refs[:ACT_SPLIT]
    ya_ref, wa_ref, wb_ref, ma_ref, mb_ref, o_ref, wab_ref, wbb_ref = refs[ACT_SPLIT:]
    i = pl.program_id(1)

    @pl.when(i == 0)
    def _():
        wab_ref[...] = wa_ref[...].astype(BF16)
        wbb_ref[...] = wb_ref[...].astype(BF16)

    a = jnp.dot(ya_ref[...], wab_ref[...], preferred_element_type=F32)
    b = jnp.dot(_join_cols(yb_refs), wbb_ref[...], preferred_element_type=F32)
    ga = jax.nn.sigmoid(ma_ref[...].astype(F32))
    gb = jax.nn.sigmoid(mb_ref[...].astype(F32))
    o_ref[...] = (ga * a + gb * b).astype(o_ref.dtype)


def _merge(y_a, y_b, proj, w_a, w_b, col_ma, col_mb, tm, tn):
    m, ka = y_a.shape
    kb = y_b.shape[1]
    n = w_a.shape[1]
    tm = min(tm, m)
    tn = min(tn, n)
    assert col_ma % tn == 0 and col_mb % tn == 0
    return pl.pallas_call(
        _merge_kernel,
        grid=(n // tn, m // tm),
        in_specs=_col_piece_specs(tm, kb, lambda j, i: i) + [
            pl.BlockSpec((tm, ka), lambda j, i: (i, 0)),
            pl.BlockSpec((ka, tn), lambda j, i: (0, j)),
            pl.BlockSpec((kb, tn), lambda j, i: (0, j)),
            pl.BlockSpec((tm, tn), lambda j, i: (i, col_ma // tn + j)),
            pl.BlockSpec((tm, tn), lambda j, i: (i, col_mb // tn + j)),
        ],
        out_specs=pl.BlockSpec((tm, tn), lambda j, i: (i, j)),
        out_shape=jax.ShapeDtypeStruct((m, n), BF16),
        scratch_shapes=[pltpu.VMEM((ka, tn), BF16), pltpu.VMEM((kb, tn), BF16)],
        compiler_params=_params("arbitrary", "arbitrary"),
        name="mixer_merge",
    )(*([y_b] * ACT_SPLIT), y_a, w_a, w_b, proj, proj)


def _resid_mm_kernel(*refs):
    a_refs = refs[:ACT_SPLIT]
    w_ref, x_ref, o_ref, wb_ref = refs[ACT_SPLIT:]
    i = pl.program_id(1)

    @pl.when(i == 0)
    def _():
        wb_ref[...] = w_ref[...].astype(BF16)

    o_ref[...] = x_ref[...] + jnp.dot(_join_cols(a_refs), wb_ref[...], preferred_element_type=F32)


def _resid_mm(a, w, x, tm, tn):
    m, k = a.shape
    n = w.shape[1]
    tm = min(tm, m)
    tn = min(tn, n)
    return pl.pallas_call(
        _resid_mm_kernel,
        grid=(n // tn, m // tm),
        in_specs=_col_piece_specs(tm, k, lambda j, i: i) + [
            pl.BlockSpec((k, tn), lambda j, i: (0, j)),
            pl.BlockSpec((tm, tn), lambda j, i: (i, j)),
        ],
        out_specs=pl.BlockSpec((tm, tn), lambda j, i: (i, j)),
        out_shape=jax.ShapeDtypeStruct((m, n), F32),
        scratch_shapes=[pltpu.VMEM((k, tn), BF16)],
        compiler_params=_params("arbitrary", "arbitrary"),
        name="out_proj_residual",
    )(*([a] * ACT_SPLIT), w, x)


def _split_bf16(a):
    hi = a.astype(BF16)
    lo = (a - hi.astype(F32)).astype(BF16)
    return hi, lo


def _router_kernel(x_ref, g_ref, w_ref, b_ref, h_ref, eid_ref, wt_ref, *, n_groups, n_experts):
    x = x_ref[...]
    ms = jnp.mean(x * x, axis=-1, keepdims=True)
    h = x * lax.rsqrt(ms + EPS) * g_ref[...]
    h_ref[...] = h
    hh, hl = _split_bf16(h)
    wh, wl = _split_bf16(w_ref[...])
    logits = (jnp.dot(hh, wh, preferred_element_type=F32) + jnp.dot(hh, wl, preferred_element_type=F32)
              + jnp.dot(hl, wh, preferred_element_type=F32)) + b_ref[...]
    tm = x.shape[0]
    per_group = n_experts // n_groups
    lane = lax.broadcasted_iota(jnp.int32, (tm, LANES), 1)
    is_g = lane < n_groups
    gl = jnp.where(is_g, logits, NEG_BIG)
    gmax = jnp.max(gl, axis=-1, keepdims=True)
    gidx = jnp.min(jnp.where(gl == gmax, lane, LANES), axis=-1, keepdims=True)
    gsum = jnp.sum(jnp.where(is_g, jnp.exp(gl - gmax), 0.0), axis=-1, keepdims=True)
    g_w = 1.0 / gsum
    lane_e = lane - n_groups
    sel = (lane_e >= 0) & (lane_e < n_experts) & (lane_e // per_group == gidx)
    el = jnp.where(sel, logits, NEG_BIG)
    emax = jnp.max(el, axis=-1, keepdims=True)
    i1 = jnp.min(jnp.where(el == emax, lane, LANES), axis=-1, keepdims=True)
    esum = jnp.sum(jnp.where(sel, jnp.exp(el - emax), 0.0), axis=-1, keepdims=True)
    el2 = jnp.where(lane == i1, NEG_BIG, el)
    emax2 = jnp.max(el2, axis=-1, keepdims=True)
    i2 = jnp.min(jnp.where(el2 == emax2, lane, LANES), axis=-1, keepdims=True)
    p1 = 1.0 / esum
    p2 = jnp.exp(emax2 - emax) / esum
    den = p1 + p2
    eid_ref[...] = jnp.where(lane == 0, i1 - n_groups, jnp.where(lane == 1, i2 - n_groups, 0))
    wt_ref[...] = jnp.where(lane == 0, g_w * (p1 / den), jnp.where(lane == 1, g_w * (p2 / den), 0.0))


def _router(x, gain, w_route, b_route, n_groups, n_experts, tm):
    t, d = x.shape
    tm = min(tm, t)
    row = pl.BlockSpec((tm, d), lambda i: (i, 0))
    narrow = pl.BlockSpec((tm, LANES), lambda i: (i, 0))
    return pl.pallas_call(
        functools.partial(_router_kernel, n_groups=n_groups, n_experts=n_experts),
        grid=(t // tm,),
        in_specs=[row, pl.BlockSpec((1, d), lambda i: (0, 0)), pl.BlockSpec((d, LANES), lambda i: (0, 0)),
                  pl.BlockSpec((1, LANES), lambda i: (0, 0))],
        out_specs=[row, narrow, narrow],
        out_shape=[jax.ShapeDtypeStruct((t, d), F32), jax.ShapeDtypeStruct((t, LANES), jnp.int32),
                   jax.ShapeDtypeStruct((t, LANES), F32)],
        compiler_params=_params("arbitrary"),
        name="ffn_norm_router",
    )(x, gain.reshape(1, d), w_route, b_route)


def _rank_kernel(eid_ref, rank_ref, cnt_ref, carry_ref):
    i = pl.program_id(0)

    @pl.when(i == 0)
    def _():
        carry_ref[...] = jnp.zeros_like(carry_ref)

    eid = eid_ref[...]
    tm = eid.shape[0]
    lane = lax.broadcasted_iota(jnp.int32, (tm, LANES), 1)
    oh0 = lane == eid[:, 0:1]
    oh1 = lane == eid[:, 1:2]
    ohs = oh0.astype(F32) + oh1.astype(F32)
    r = lax.broadcasted_iota(jnp.int32, (tm, tm), 0)
    c = lax.broadcasted_iota(jnp.int32, (tm, tm), 1)
    earlier = (c < r).astype(BF16)
    before = jnp.dot(earlier, ohs.astype(BF16), preferred_element_type=F32) + carry_ref[...]
    rank0 = jnp.sum(jnp.where(oh0, before, 0.0), axis=-1, keepdims=True)
    rank1 = jnp.sum(jnp.where(oh1, before, 0.0), axis=-1, keepdims=True)
    rank_ref[...] = jnp.where(lane == 0, rank0, jnp.where(lane == 1, rank1, 0.0)).astype(jnp.int32)
    total = carry_ref[...] + jnp.sum(ohs, axis=0, keepdims=True)
    carry_ref[...] = total
    cnt_ref[...] = jnp.broadcast_to(total, cnt_ref.shape).astype(jnp.int32)


def _rank(eid, tm):
    t = eid.shape[0]
    tm = min(tm, t)
    narrow = pl.BlockSpec((tm, LANES), lambda i: (i, 0))
    return pl.pallas_call(
        _rank_kernel,
        grid=(t // tm,),
        in_specs=[narrow],
        out_specs=[narrow, pl.BlockSpec((8, LANES), lambda i: (0, 0))],
        out_shape=[jax.ShapeDtypeStruct((t, LANES), jnp.int32), jax.ShapeDtypeStruct((8, LANES), jnp.int32)],
        scratch_shapes=[pltpu.VMEM((1, LANES), F32)],
        compiler_params=_params("arbitrary"),
        name="expert_rank",
    )(eid)


def _dest_kernel(eid_ref, rank_ref, ps_ref, o_ref):
    eid = eid_ref[...]
    lane = lax.broadcasted_iota(jnp.int32, eid.shape, 1)
    ps = ps_ref[...]
    s0 = jnp.sum(jnp.where(lane == eid[:, 0:1], ps, 0.0), axis=-1, keepdims=True)
    s1 = jnp.sum(jnp.where(lane == eid[:, 1:2], ps, 0.0), axis=-1, keepdims=True)
    start = jnp.where(lane == 0, s0, jnp.where(lane == 1, s1, 0.0))
    o_ref[...] = start.astype(jnp.int32) + rank_ref[...]


def _dest_rows(eid, rank, pstart, tm):
    t = eid.shape[0]
    tm = min(tm, t)
    ps = jnp.zeros((1, LANES), F32).at[0, :pstart.shape[0]].set(pstart.astype(F32))
    narrow = pl.BlockSpec((tm, LANES), lambda i: (i, 0))
    return pl.pallas_call(
        _dest_kernel,
        grid=(t // tm,),
        in_specs=[narrow, narrow, pl.BlockSpec((1, LANES), lambda i: (0, 0))],
        out_specs=narrow,
        out_shape=jax.ShapeDtypeStruct((t, LANES), jnp.int32),
        compiler_params=_params("arbitrary"),
        name="expert_dest_rows",
    )(eid, rank, ps)


GATHER_UNROLL = 8


W_SPLIT = 4


def _join_rows(refs):
    return jnp.concatenate([r[...] for r in refs], axis=0)


def _expert_ffn_kernel(te_ref, trb_ref, nsub_ref, ntot_ref, tok_ref, tok_next_ref, h_hbm, *refs, tm, n_up):
    wg_refs, wu_refs, wd_refs = refs[:W_SPLIT], refs[W_SPLIT:2 * W_SPLIT], refs[2 * W_SPLIT:3 * W_SPLIT]
    y_ref, land_ref, hid_ref, sem = refs[3 * W_SPLIT:]
    t = pl.program_id(0)
    s = pl.program_id(1)
    n_tiles = pl.num_programs(0)
    nsub = nsub_ref[t]
    slot = t & 1

    def row_copy(tok, r, sl):
        return pltpu.make_async_copy(h_hbm.at[pl.ds(tok, 1)], land_ref.at[sl, pl.ds(r, 1)], sem.at[sl])

    def start_gather(toks_ref, blocks, sl):
        def issue(i, carry):
            for u in range(GATHER_UNROLL):
                r = i * GATHER_UNROLL + u
                row_copy(toks_ref[0, 0, r], r, sl).start()
            return carry

        lax.fori_loop(0, blocks * (ROW_BLOCK // GATHER_UNROLL), issue, 0)

    @pl.when(nsub > 0)
    def _():
        @pl.when(s == 0)
        def _():
            @pl.when(t == 0)
            def _():
                start_gather(tok_ref, nsub, slot)

            def drain(i, carry):
                for u in range(GATHER_UNROLL):
                    row_copy(0, 0, slot).wait()
                return carry

            lax.fori_loop(0, nsub * (ROW_BLOCK // GATHER_UNROLL), drain, 0)

        @pl.when(s == 1)
        def _():
            nxt = jnp.minimum(t + 1, n_tiles - 1)
            n_next = jnp.where(t + 1 < n_tiles, nsub_ref[nxt], 0)
            start_gather(tok_next_ref, n_next, 1 - slot)

        for n in range(1, tm // ROW_BLOCK + 1):
            rows = n * ROW_BLOCK

            @pl.when((nsub == n) & (s < n_up))
            def _(rows=rows):
                x = land_ref[slot, pl.ds(0, rows), :]
                g = jnp.dot(x, _join_rows(wg_refs), preferred_element_type=F32)
                u = jnp.dot(x, _join_rows(wu_refs), preferred_element_type=F32)
                hid_ref[s, pl.ds(0, rows), :] = g * jax.nn.sigmoid(g) * u

            @pl.when((nsub == n) & (s >= n_up))
            def _(rows=rows):
                hid = jnp.concatenate([hid_ref[c, pl.ds(0, rows), :] for c in range(n_up)], axis=1)
                y_ref[pl.ds(0, rows), :] = jnp.dot(hid, _join_rows(wd_refs), preferred_element_type=F32)
                if rows < tm:
                    y_ref[pl.ds(rows, tm - rows), :] = jnp.zeros((tm - rows, y_ref.shape[1]), y_ref.dtype)


def _expert_ffn(h, row_tok, w_gate, w_up, w_down, meta, n_tiles, tm, tf, tn):
    d = h.shape[1]
    f = w_gate.shape[2]
    tf = min(tf, f)
    tn = min(tn, d)
    n_up = f // tf
    n_down = d // tn
    n_steps = n_up + n_down

    def step(t, s, ntot):
        return jnp.where(t < ntot[0], s, n_steps - 1)

    def up_spec(piece):
        return pl.BlockSpec((None, d // W_SPLIT, tf), lambda t, s, te, trb, ns, ntot:
                            (te[t], piece, jnp.minimum(step(t, s, ntot), n_up - 1)))

    def down_spec(piece):
        return pl.BlockSpec((None, f // W_SPLIT, tn), lambda t, s, te, trb, ns, ntot:
                            (te[t], piece, down_col(t, s, ntot)))

    def down_col(t, s, ntot):
        return jnp.maximum(step(t, s, ntot) - n_up, 0)

    grid_spec = pltpu.PrefetchScalarGridSpec(
        num_scalar_prefetch=4,
        grid=(n_tiles, n_steps),
        in_specs=[
            pl.BlockSpec((1, 1, tm), lambda t, s, te, trb, ns, ntot: (trb[t], 0, 0), memory_space=pltpu.SMEM),
            pl.BlockSpec((1, 1, tm), lambda t, s, te, trb, ns, ntot: (trb[jnp.minimum(t + 1, n_tiles - 1)], 0, 0),
                         memory_space=pltpu.SMEM),
            pl.BlockSpec(memory_space=pl.ANY),
        ] + [up_spec(c) for c in range(W_SPLIT)] * 2 + [down_spec(c) for c in range(W_SPLIT)],
        out_specs=pl.BlockSpec((tm, tn), lambda t, s, te, trb, ns, ntot: (trb[t], down_col(t, s, ntot))),
        scratch_shapes=[
            pltpu.VMEM((2, tm, d), F32),
            pltpu.VMEM((n_up, tm, tf), F32),
            pltpu.SemaphoreType.DMA((2,)),
        ],
    )
    return pl.pallas_call(
        functools.partial(_expert_ffn_kernel, tm=tm, n_up=n_up),
        grid_spec=grid_spec,
        out_shape=jax.ShapeDtypeStruct((n_tiles * tm, d), F32),
        compiler_params=_params("arbitrary", "arbitrary"),
        name="expert_ffn",
    )(*meta, row_tok.reshape(n_tiles, 1, tm), row_tok.reshape(n_tiles, 1, tm), h,
      *([w_gate] * W_SPLIT), *([w_up] * W_SPLIT), *([w_down] * W_SPLIT))


def _combine_kernel(dest_ref, dest_next_ref, wt_ref, x_ref, g_ref, y_hbm, x2_ref, h3_ref, buf_ref, sem, *, tb):
    i = pl.program_id(0)
    slot = i & 1

    def row_copy(src, k, r, sl):
        return pltpu.make_async_copy(y_hbm.at[pl.ds(src, 1)], buf_ref.at[sl, k, pl.ds(r, 1)], sem.at[sl])

    def start_gather(rows_ref, sl):
        def issue(r, carry):
            for k in range(TOP_K):
                row_copy(rows_ref[0, 0, TOP_K * r + k], k, r, sl).start()
            return carry

        lax.fori_loop(0, tb, issue, 0, unroll=GATHER_UNROLL // TOP_K)

    @pl.when(i == 0)
    def _():
        start_gather(dest_ref, slot)

    @pl.when(i + 1 < pl.num_programs(0))
    def _():
        start_gather(dest_next_ref, 1 - slot)

    def drain(r, carry):
        for k in range(TOP_K):
            row_copy(0, k, r, slot).wait()
        return carry

    lax.fori_loop(0, tb, drain, 0, unroll=GATHER_UNROLL // TOP_K)

    wt = wt_ref[...]
    x2 = x_ref[...] + (wt[:, 0:1] * buf_ref[slot, 0] + wt[:, 1:2] * buf_ref[slot, 1])
    x2_ref[...] = x2
    ms = jnp.mean(x2 * x2, axis=-1, keepdims=True)
    h3_ref[...] = (x2 * lax.rsqrt(ms + EPS) * g_ref[...]).astype(h3_ref.dtype)


def _combine(x, y, dest, wts, gain, tb):
    t, d = x.shape
    tb = min(tb, t)
    nb = t // tb
    dest3 = dest.reshape(nb, 1, TOP_K * tb)
    row = pl.BlockSpec((tb, d), lambda i: (i, 0))
    return pl.pallas_call(
        functools.partial(_combine_kernel, tb=tb),
        grid=(nb,),
        in_specs=[
            pl.BlockSpec((1, 1, TOP_K * tb), lambda i: (i, 0, 0), memory_space=pltpu.SMEM),
            pl.BlockSpec((1, 1, TOP_K * tb), lambda i: (jnp.minimum(i + 1, nb - 1), 0, 0), memory_space=pltpu.SMEM),
            pl.BlockSpec((tb, LANES), lambda i: (i, 0)),
            row,
            pl.BlockSpec((1, d), lambda i: (0, 0)),
            pl.BlockSpec(memory_space=pl.ANY),
        ],
        out_specs=[row, row],
        out_shape=[jax.ShapeDtypeStruct((t, d), F32), jax.ShapeDtypeStruct((t, d), BF16)],
        scratch_shapes=[pltpu.VMEM((2, TOP_K, tb, d), F32), pltpu.SemaphoreType.DMA((2,))],
        compiler_params=_params("arbitrary"),
        name="moe_combine_norm",
    )(dest3, dest3, wts, x, gain.reshape(1, d), y)


def _ple_kernel(*refs):
    h_refs = refs[:ACT_SPLIT]
    p_ref, wg_ref, wp_ref, x_ref, o_ref, wgb_ref, wpb_ref = refs[ACT_SPLIT:]
    i = pl.program_id(1)

    @pl.when(i == 0)
    def _():
        wgb_ref[...] = wg_ref[...].astype(BF16)
        wpb_ref[...] = wp_ref[...].astype(BF16)

    a = jnp.dot(_join_cols(h_refs), wgb_ref[...], preferred_element_type=F32)
    b = jnp.dot(p_ref[...].astype(BF16), wpb_ref[...], preferred_element_type=F32)
    o_ref[...] = x_ref[...] + jax.nn.sigmoid(a) * b


def _ple(h, p, w_gate, w_proj, x, tm, tn):
    m, k = h.shape
    kp = p.shape[1]
    n = w_gate.shape[1]
    tm = min(tm, m)
    tn = min(tn, n)
    return pl.pallas_call(
        _ple_kernel,
        grid=(n // tn, m // tm),
        in_specs=_col_piece_specs(tm, k, lambda j, i: i) + [
            pl.BlockSpec((tm, kp), lambda j, i: (i, 0)),
            pl.BlockSpec((k, tn), lambda j, i: (0, j)),
            pl.BlockSpec((kp, tn), lambda j, i: (0, j)),
            pl.BlockSpec((tm, tn), lambda j, i: (i, j)),
        ],
        out_specs=pl.BlockSpec((tm, tn), lambda j, i: (i, j)),
        out_shape=jax.ShapeDtypeStruct((m, n), F32),
        scratch_shapes=[pltpu.VMEM((k, tn), BF16), pltpu.VMEM((kp, tn), BF16)],
        compiler_params=_params("arbitrary", "arbitrary"),
        name="ple_gate_residual",
    )(*([h] * ACT_SPLIT), p, w_gate, w_proj, x)


EXPERT_TILE_ROWS = 640


def _rope_tables(s):
    half = HEAD_DIM // 2
    inv_freq = ROPE_THETA ** (-jnp.arange(half, dtype=F32) / half)
    ang = jnp.arange(s, dtype=jnp.int32).astype(F32)[:, None] * inv_freq[None, :]
    cos, sin = jnp.cos(ang), jnp.sin(ang)
    return jnp.concatenate([cos, cos], axis=-1), jnp.concatenate([-sin, sin], axis=-1)


def _expert_tiles(counts, n_assign, tm):
    n_experts = counts.shape[0]
    n_tiles = -(-n_assign // tm) + n_experts
    tiles_e = (counts + tm - 1) // tm
    tcum = jnp.cumsum(tiles_e)
    tstart = tcum - tiles_e
    pstart = tstart * tm
    total = tcum[-1]
    t = jnp.arange(n_tiles, dtype=jnp.int32)
    tc = jnp.minimum(t, total - 1)
    te = jnp.minimum(jnp.searchsorted(tcum, tc, side="right"), n_experts - 1).astype(jnp.int32)
    local = tc - tstart[te]
    rows_left = counts[te] - local * tm
    nsub = jnp.clip((rows_left + ROW_BLOCK - 1) // ROW_BLOCK, 0, tm // ROW_BLOCK)
    nsub = jnp.where(t < total, nsub, 0).astype(jnp.int32)
    trb = (tstart[te] + local).astype(jnp.int32)
    meta = (te, trb, nsub, jnp.reshape(total, (1,)).astype(jnp.int32))
    return pstart.astype(jnp.int32), meta, n_tiles


def _layer(x, p, w_in, w_conv, w_up_a, w_out_b, w_o, g_mix, g_ffn, w_group, b_group, w_router, b_router,
           w_gate, w_up, w_down, g_ple, w_ple_gate, w_ple_proj):
    s, d = x.shape
    cw = w_conv.shape[1]
    n_experts = w_gate.shape[0]
    ng = len(ATTN_DILATIONS)
    a_out = HEADS_PER_GROUP * HEAD_DIM
    q_cols = ng * a_out
    qkv_cols = q_cols + 2 * a_out
    col_u, col_gb, col_gc = 0, cw, 2 * cw
    col_ma = 3 * cw
    col_mb = col_ma + d

    h = _rmsnorm(x, g_mix, BF16, 256)
    rope_c, rope_s = _rope_tables(s)
    qkv = _qkv_proj(h, w_in, rope_c, rope_s, qkv_cols, q_cols + a_out, 1024, 512)
    proj = _mm_cols(h, w_in, qkv_cols, 1024, 512)
    y_a = _dilated_attention(qkv)
    y_b = _short_conv(proj, w_conv, col_u, col_gb, col_gc, 512, 512)
    merged = _merge(y_a, y_b, proj, w_up_a, w_out_b, col_ma, col_mb, 1024, 512)
    x1 = _resid_mm(merged, w_o, x, 1024, 512)

    n_route = N_EXPERT_GROUPS + n_experts
    w_route = jnp.pad(jnp.concatenate([w_group, w_router], axis=1), ((0, 0), (0, LANES - n_route)))
    b_route = jnp.pad(jnp.concatenate([b_group, b_router]), (0, LANES - n_route)).reshape(1, LANES)
    h2, eid, wts = _router(x1, g_ffn, w_route, b_route, N_EXPERT_GROUPS, n_experts, 256)
    rank, cnt = _rank(eid, 256)
    counts = cnt[0, :n_experts]
    tm_e = EXPERT_TILE_ROWS
    pstart, meta, n_tiles = _expert_tiles(counts, s * TOP_K, tm_e)
    dest = _dest_rows(eid, rank, pstart, 1024)[:, :TOP_K].reshape(-1)
    tokens = jnp.repeat(jnp.arange(s, dtype=jnp.int32), TOP_K)
    row_tok = jnp.zeros((n_tiles * tm_e,), jnp.int32).at[dest].set(tokens)
    y = _expert_ffn(h2, row_tok, w_gate, w_up, w_down, meta, n_tiles, tm_e, 256, 512)
    x2, h3 = _combine(x1, y, dest, wts, g_ple, 256)

    return _ple(h3, p, w_ple_gate, w_ple_proj, x2, 1024, 512)


def kernel(x, p, w_in, w_conv, w_up_a, w_out_b, w_o, norm_mix, norm_ffn, w_group, b_group, w_router, b_router,
           w_gate, w_up, w_down, norm_ple, w_ple_gate, w_ple_proj, norm_final):
    b, s, d = x.shape
    depth = w_in.shape[0]
    outs = []
    for bi in range(b):
        xb = x[bi]
        for i in range(depth):
            xb = _layer(xb, p[i, bi], w_in[i], w_conv[i], w_up_a[i], w_out_b[i], w_o[i], norm_mix[i], norm_ffn[i],
                        w_group[i], b_group[i], w_router[i], b_router[i], w_gate[i], w_up[i], w_down[i],
                        norm_ple[i], w_ple_gate[i], w_ple_proj[i])
        outs.append(_rmsnorm(xb, norm_final, x.dtype, 256))
    return jnp.stack(outs)
```

```python
import functools

import jax
import jax.numpy as jnp
from jax import lax
from jax.experimental import pallas as pl
from jax.experimental.pallas import tpu as pltpu

F32 = jnp.float32
BF16 = jnp.bfloat16

HEAD_DIM = 128
ATTN_DILATIONS = (1, 4, 16)
HEADS_PER_GROUP = 4
ROPE_THETA = 10000.0
ATTN_BLOCK = 128
NEG_BIG = -1e30
N_EXPERT_GROUPS = 4
TOP_K = 2
EPS = 1e-6
ROW_BLOCK = 128
LANES = 128

VMEM_LIMIT_BYTES = 58 * 1024 * 1024


def _params(*sem):
    return pltpu.CompilerParams(dimension_semantics=sem, vmem_limit_bytes=VMEM_LIMIT_BYTES)


def _rmsnorm_kernel(x_ref, g_ref, o_ref):
    x = x_ref[...]
    ms = jnp.mean(x * x, axis=-1, keepdims=True)
    o_ref[...] = (x * lax.rsqrt(ms + EPS) * g_ref[...]).astype(o_ref.dtype)


def _rmsnorm(x, gain, out_dtype, tm):
    n, d = x.shape
    tm = min(tm, n)
    return pl.pallas_call(
        _rmsnorm_kernel,
        grid=(n // tm,),
        in_specs=[pl.BlockSpec((tm, d), lambda i: (i, 0)), pl.BlockSpec((1, d), lambda i: (0, 0))],
        out_specs=pl.BlockSpec((tm, d), lambda i: (i, 0)),
        out_shape=jax.ShapeDtypeStruct((n, d), out_dtype),
        compiler_params=_params("arbitrary"),
        name="rmsnorm",
    )(x, gain.reshape(1, d))


def _qkv_proj_kernel(h_ref, w_ref, c_ref, s_ref, o_ref, wb_ref, *, n_rope_tiles):
    j = pl.program_id(0)
    i = pl.program_id(1)

    @pl.when(i == 0)
    def _():
        wb_ref[...] = w_ref[...].astype(BF16)

    acc = jnp.dot(h_ref[...], wb_ref[...], preferred_element_type=F32)
    heads = o_ref.shape[0]

    @pl.when(j < n_rope_tiles)
    def _():
        c = c_ref[...]
        s = s_ref[...]
        for hh in range(heads):
            t = acc[:, hh * HEAD_DIM:(hh + 1) * HEAD_DIM]
            o_ref[hh] = t * c + pltpu.roll(t, HEAD_DIM // 2, 1) * s

    @pl.when(j >= n_rope_tiles)
    def _():
        for hh in range(heads):
            o_ref[hh] = acc[:, hh * HEAD_DIM:(hh + 1) * HEAD_DIM]


def _qkv_proj(h, w, rope_c, rope_s, n_cols, n_rope_cols, tm, tn):
    m, k = h.shape
    tm = min(tm, m)
    assert n_cols % tn == 0 and n_rope_cols % tn == 0 and m % tm == 0 and tn % HEAD_DIM == 0
    heads = tn // HEAD_DIM
    return pl.pallas_call(
        functools.partial(_qkv_proj_kernel, n_rope_tiles=n_rope_cols // tn),
        grid=(n_cols // tn, m // tm),
        in_specs=[
            pl.BlockSpec((tm, k), lambda j, i: (i, 0)),
            pl.BlockSpec((k, tn), lambda j, i: (0, j)),
            pl.BlockSpec((tm, HEAD_DIM), lambda j, i: (i, 0)),
            pl.BlockSpec((tm, HEAD_DIM), lambda j, i: (i, 0)),
        ],
        out_specs=pl.BlockSpec((heads, tm, HEAD_DIM), lambda j, i: (j, i, 0)),
        out_shape=jax.ShapeDtypeStruct((n_cols // HEAD_DIM, m, HEAD_DIM), F32),
        scratch_shapes=[pltpu.VMEM((k, tn), BF16)],
        compiler_params=_params("arbitrary", "arbitrary"),
        name="qkv_proj_rope",
    )(h, w, rope_c, rope_s)


def _staged_weight(w_hbm, stage_ref, wb_ref, sem, tn, col0=0):
    j = pl.program_id(0)
    i = pl.program_id(1)

    def fetch(jj):
        cols = pl.ds(pl.multiple_of(col0 + jj * tn, LANES), tn)
        return pltpu.make_async_copy(w_hbm.at[:, cols], stage_ref, sem)

    @pl.when(i == 0)
    def _():
        @pl.when(j == 0)
        def _():
            fetch(0).start()

        fetch(j).wait()
        wb_ref[...] = stage_ref[...].astype(BF16)

        @pl.when(j + 1 < pl.num_programs(0))
        def _():
            fetch(j + 1).start()


def _mm_kernel(a_ref, w_hbm, o_ref, stage_ref, wb_ref, sem, *, tn, col0):
    _staged_weight(w_hbm, stage_ref, wb_ref, sem, tn, col0)
    o_ref[...] = jnp.dot(a_ref[...], wb_ref[...], preferred_element_type=F32).astype(o_ref.dtype)


def _mm_cols(a, w, col0, tm, tn):
    m, k = a.shape
    n = w.shape[1] - col0
    tm = min(tm, m)
    while n % tn:
        tn //= 2
    assert tn % LANES == 0 and col0 % LANES == 0 and m % tm == 0
    return pl.pallas_call(
        functools.partial(_mm_kernel, tn=tn, col0=col0),
        grid=(n // tn, m // tm),
        in_specs=[
            pl.BlockSpec((tm, k), lambda j, i: (i, 0)),
            pl.BlockSpec(memory_space=pl.ANY),
        ],
        out_specs=pl.BlockSpec((tm, tn), lambda j, i: (i, j)),
        out_shape=jax.ShapeDtypeStruct((m, n), BF16),
        scratch_shapes=[pltpu.VMEM((k, tn), F32), pltpu.VMEM((k, tn), BF16), pltpu.SemaphoreType.DMA],
        compiler_params=_params("arbitrary", "arbitrary"),
        name="in_proj",
    )(a, w)


ATTN_SUPER = ATTN_BLOCK * max(ATTN_DILATIONS)
ATTN_UNROLL = 16


def _rows(start, size, stride):
    return pl.ds(start, size) if stride == 1 else pl.ds(start, size, stride=stride)


def _dilated_attn_kernel(q0_ref, q1_ref, q2_ref, kc_ref, kp_ref, vc_ref, vp_ref, y_ref,
                         kcat_ref, vcat_ref, o_ref, lse_ref):
    sb = pl.program_id(1)
    sup = ATTN_SUPER
    kcat_ref[pl.ds(0, sup), :] = kp_ref[...]
    kcat_ref[pl.ds(sup, sup), :] = kc_ref[...]
    vcat_ref[pl.ds(0, sup), :] = vp_ref[...]
    vcat_ref[pl.ds(sup, sup), :] = vc_ref[...]
    qi = lax.broadcasted_iota(jnp.int32, (ATTN_BLOCK, 2 * ATTN_BLOCK), 0)
    kj = lax.broadcasted_iota(jnp.int32, (ATTN_BLOCK, 2 * ATTN_BLOCK), 1)
    dist = qi + ATTN_BLOCK - kj
    band = (dist >= 0) & (dist <= ATTN_BLOCK)
    in_cur = kj >= ATTN_BLOCK
    scale = HEAD_DIM ** -0.5

    for g, (q_ref, d) in enumerate(zip((q0_ref, q1_ref, q2_ref), ATTN_DILATIONS)):
        span = ATTN_BLOCK * d
        shift = d.bit_length() - 1

        def unit(u, carry, g=g, q_ref=q_ref, d=d, span=span, shift=shift):
            j = lax.shift_right_logical(u, shift)
            row0 = j * span + (u & (d - 1))
            q = q_ref[_rows(row0, ATTN_BLOCK, d), :].astype(BF16)
            k = kcat_ref[_rows(sup - span + row0, 2 * ATTN_BLOCK, d), :].astype(BF16)
            v = vcat_ref[_rows(sup - span + row0, 2 * ATTN_BLOCK, d), :].astype(BF16)
            s = lax.dot_general(q, k, (((1,), (1,)), ((), ())), preferred_element_type=F32) * scale
            has_prev = (sb > 0) | (j > 0)
            s = jnp.where(band & (in_cur | has_prev), s, NEG_BIG)
            m = jnp.max(s, axis=-1, keepdims=True)
            e = jnp.exp(s - m)
            ssum = jnp.sum(e, axis=-1, keepdims=True)
            p = (e * (1.0 / ssum)).astype(BF16)
            o_ref[g, _rows(row0, ATTN_BLOCK, d), :] = jnp.dot(p, v, preferred_element_type=F32)
            lse_ref[g, _rows(row0, ATTN_BLOCK, d), :] = jnp.broadcast_to(m + jnp.log(ssum), (ATTN_BLOCK, HEAD_DIM))
            return carry

        lax.fori_loop(0, sup // ATTN_BLOCK, unit, 0, unroll=ATTN_UNROLL)

    lse = lse_ref[...]
    m = jnp.max(lse, axis=0)
    e = jnp.exp(lse - m[None])
    w = e / jnp.sum(e, axis=0)[None]
    y_ref[...] = jnp.sum(w * o_ref[...], axis=0).astype(y_ref.dtype)


def _dilated_attention(qkv):
    nh, s, hd = qkv.shape
    ng = len(ATTN_DILATIONS)
    hpg = HEADS_PER_GROUP
    assert nh == (ng + 2) * hpg and hd == HEAD_DIM and s % ATTN_SUPER == 0
    sup = ATTN_SUPER

    def q_spec(g):
        return pl.BlockSpec((None, sup, hd), lambda h, b: (g * hpg + h, b, 0))

    def kv_spec(first_head, prev):
        if prev:
            return pl.BlockSpec((None, sup, hd), lambda h, b: (first_head + h, jnp.maximum(b - 1, 0), 0))
        return pl.BlockSpec((None, sup, hd), lambda h, b: (first_head + h, b, 0))

    k0, v0 = ng * hpg, (ng + 1) * hpg
    return pl.pallas_call(
        _dilated_attn_kernel,
        grid=(hpg, s // sup),
        in_specs=[q_spec(0), q_spec(1), q_spec(2), kv_spec(k0, False), kv_spec(k0, True),
                  kv_spec(v0, False), kv_spec(v0, True)],
        out_specs=pl.BlockSpec((sup, hd), lambda h, b: (b, h)),
        out_shape=jax.ShapeDtypeStruct((s, hpg * hd), BF16),
        scratch_shapes=[pltpu.VMEM((2 * sup, hd), F32), pltpu.VMEM((2 * sup, hd), F32),
                        pltpu.VMEM((ng, sup, hd), F32), pltpu.VMEM((ng, sup, hd), F32)],
        compiler_params=_params("arbitrary", "arbitrary"),
        name="dilated_attention",
    )(qkv, qkv, qkv, qkv, qkv, qkv, qkv)


CONV_HALO = 16


def _conv_kernel(u_ref, gc_ref, gb_ref, up_ref, gcp_ref, w_ref, o_ref):
    i = pl.program_id(0)
    cur = u_ref[...].astype(F32) * gc_ref[...].astype(F32)
    prev = up_ref[...].astype(F32) * gcp_ref[...].astype(F32)
    prev = jnp.where(i == 0, 0.0, prev)
    big = jnp.concatenate([prev, cur], axis=0)
    back1 = pltpu.roll(big, 1, 0)[CONV_HALO:]
    back2 = pltpu.roll(big, 2, 0)[CONV_HALO:]
    w = w_ref[...]
    y = w[0:1] * back2 + w[1:2] * back1 + w[2:3] * cur
    o_ref[...] = (gb_ref[...].astype(F32) * y).astype(o_ref.dtype)


def _short_conv(proj, w_conv, col_u, col_gb, col_gc, tm, tc):
    s = proj.shape[0]
    kw, cw = w_conv.shape
    assert kw == 3
    tm = min(tm, s)
    tc = min(tc, cw)
    assert col_u % tc == 0 and col_gb % tc == 0 and col_gc % tc == 0 and tm % CONV_HALO == 0
    hb = tm // CONV_HALO

    def cur(col):
        return pl.BlockSpec((tm, tc), lambda i, j: (i, col // tc + j))

    def prev(col):
        return pl.BlockSpec((CONV_HALO, tc), lambda i, j: (jnp.maximum(i * hb - 1, 0), col // tc + j))

    return pl.pallas_call(
        _conv_kernel,
        grid=(s // tm, cw // tc),
        in_specs=[cur(col_u), cur(col_gc), cur(col_gb), prev(col_u), prev(col_gc),
                  pl.BlockSpec((kw, tc), lambda i, j: (0, j))],
        out_specs=pl.BlockSpec((tm, tc), lambda i, j: (i, j)),
        out_shape=jax.ShapeDtypeStruct((s, cw), BF16),
        compiler_params=_params("arbitrary", "arbitrary"),
        name="short_conv",
    )(proj, proj, proj, proj, proj, w_conv)


def _merge_kernel(ya_ref, yb_ref, wa_ref, wb_ref, ma_ref, mb_ref, o_ref, wab_ref, wbb_ref):
    i = pl.program_id(1)

    @pl.when(i == 0)
    def _():
        wab_ref[...] = wa_ref[...].astype(BF16)
        wbb_ref[...] = wb_ref[...].astype(BF16)

    a = jnp.dot(ya_ref[...], wab_ref[...], preferred_element_type=F32)
    b = jnp.dot(yb_ref[...], wbb_ref[...], preferred_element_type=F32)
    ga = jax.nn.sigmoid(ma_ref[...].astype(F32))
    gb = jax.nn.sigmoid(mb_ref[...].astype(F32))
    o_ref[...] = (ga * a + gb * b).astype(o_ref.dtype)


def _merge(y_a, y_b, proj, w_a, w_b, col_ma, col_mb, tm, tn):
    m, ka = y_a.shape
    kb = y_b.shape[1]
    n = w_a.shape[1]
    tm = min(tm, m)
    tn = min(tn, n)
    while col_ma % tn or col_mb % tn:
        tn //= 2
    assert tn % LANES == 0
    return pl.pallas_call(
        _merge_kernel,
        grid=(n // tn, m // tm),
        in_specs=[
            pl.BlockSpec((tm, ka), lambda j, i: (i, 0)),
            pl.BlockSpec((tm, kb), lambda j, i: (i, 0)),
            pl.BlockSpec((ka, tn), lambda j, i: (0, j)),
            pl.BlockSpec((kb, tn), lambda j, i: (0, j)),
            pl.BlockSpec((tm, tn), lambda j, i: (i, col_ma // tn + j)),
            pl.BlockSpec((tm, tn), lambda j, i: (i, col_mb // tn + j)),
        ],
        out_specs=pl.BlockSpec((tm, tn), lambda j, i: (i, j)),
        out_shape=jax.ShapeDtypeStruct((m, n), BF16),
        scratch_shapes=[pltpu.VMEM((ka, tn), BF16), pltpu.VMEM((kb, tn), BF16)],
        compiler_params=_params("arbitrary", "arbitrary"),
        name="mixer_merge",
    )(y_a, y_b, w_a, w_b, proj, proj)


def _resid_mm_kernel(a_ref, x_ref, w_hbm, o_ref, stage_ref, wb_ref, sem, *, tn):
    _staged_weight(w_hbm, stage_ref, wb_ref, sem, tn)
    o_ref[...] = x_ref[...] + jnp.dot(a_ref[...], wb_ref[...], preferred_element_type=F32)


def _resid_mm(a, w, x, tm, tn):
    m, k = a.shape
    n = w.shape[1]
    tm = min(tm, m)
    tn = min(tn, n)
    return pl.pallas_call(
        functools.partial(_resid_mm_kernel, tn=tn),
        grid=(n // tn, m // tm),
        in_specs=[
            pl.BlockSpec((tm, k), lambda j, i: (i, 0)),
            pl.BlockSpec((tm, tn), lambda j, i: (i, j)),
            pl.BlockSpec(memory_space=pl.ANY),
        ],
        out_specs=pl.BlockSpec((tm, tn), lambda j, i: (i, j)),
        out_shape=jax.ShapeDtypeStruct((m, n), F32),
        scratch_shapes=[pltpu.VMEM((k, tn), F32), pltpu.VMEM((k, tn), BF16), pltpu.SemaphoreType.DMA],
        compiler_params=_params("arbitrary", "arbitrary"),
        name="out_proj_residual",
    )(a, x, w)


def _split_bf16(a):
    hi = a.astype(BF16)
    lo = (a - hi.astype(F32)).astype(BF16)
    return hi, lo


def _router_kernel(x_ref, g_ref, w_ref, b_ref, h_ref, eid_ref, wt_ref, *, n_groups, n_experts):
    x = x_ref[...]
    ms = jnp.mean(x * x, axis=-1, keepdims=True)
    h = x * lax.rsqrt(ms + EPS) * g_ref[...]
    h_ref[...] = h
    hh, hl = _split_bf16(h)
    wh, wl = _split_bf16(w_ref[...])
    logits = (jnp.dot(hh, wh, preferred_element_type=F32) + jnp.dot(hh, wl, preferred_element_type=F32)
              + jnp.dot(hl, wh, preferred_element_type=F32)) + b_ref[...]
    tm = x.shape[0]
    per_group = n_experts // n_groups
    lane = lax.broadcasted_iota(jnp.int32, (tm, LANES), 1)
    is_g = lane < n_groups
    gl = jnp.where(is_g, logits, NEG_BIG)
    gmax = jnp.max(gl, axis=-1, keepdims=True)
    gidx = jnp.min(jnp.where(gl == gmax, lane, LANES), axis=-1, keepdims=True)
    gsum = jnp.sum(jnp.where(is_g, jnp.exp(gl - gmax), 0.0), axis=-1, keepdims=True)
    g_w = 1.0 / gsum
    lane_e = lane - n_groups
    sel = (lane_e >= 0) & (lane_e < n_experts) & (lane_e // per_group == gidx)
    el = jnp.where(sel, logits, NEG_BIG)
    emax = jnp.max(el, axis=-1, keepdims=True)
    i1 = jnp.min(jnp.where(el == emax, lane, LANES), axis=-1, keepdims=True)
    esum = jnp.sum(jnp.where(sel, jnp.exp(el - emax), 0.0), axis=-1, keepdims=True)
    el2 = jnp.where(lane == i1, NEG_BIG, el)
    emax2 = jnp.max(el2, axis=-1, keepdims=True)
    i2 = jnp.min(jnp.where(el2 == emax2, lane, LANES), axis=-1, keepdims=True)
    p1 = 1.0 / esum
    p2 = jnp.exp(emax2 - emax) / esum
    den = p1 + p2
    eid_ref[...] = jnp.where(lane == 0, i1 - n_groups, jnp.where(lane == 1, i2 - n_groups, 0))
    wt_ref[...] = jnp.where(lane == 0, g_w * (p1 / den), jnp.where(lane == 1, g_w * (p2 / den), 0.0))


def _router(x, gain, w_route, b_route, n_groups, n_experts, tm):
    t, d = x.shape
    tm = min(tm, t)
    row = pl.BlockSpec((tm, d), lambda i: (i, 0))
    narrow = pl.BlockSpec((tm, LANES), lambda i: (i, 0))
    return pl.pallas_call(
        functools.partial(_router_kernel, n_groups=n_groups, n_experts=n_experts),
        grid=(t // tm,),
        in_specs=[row, pl.BlockSpec((1, d), lambda i: (0, 0)), pl.BlockSpec((d, LANES), lambda i: (0, 0)),
                  pl.BlockSpec((1, LANES), lambda i: (0, 0))],
        out_specs=[row, narrow, narrow],
        out_shape=[jax.ShapeDtypeStruct((t, d), F32), jax.ShapeDtypeStruct((t, LANES), jnp.int32),
                   jax.ShapeDtypeStruct((t, LANES), F32)],
        compiler_params=_params("arbitrary"),
        name="ffn_norm_router",
    )(x, gain.reshape(1, d), w_route, b_route)


def _rank_kernel(eid_ref, rank_ref, cnt_ref, carry_ref):
    i = pl.program_id(0)

    @pl.when(i == 0)
    def _():
        carry_ref[...] = jnp.zeros_like(carry_ref)

    eid = eid_ref[...]
    tm = eid.shape[0]
    lane = lax.broadcasted_iota(jnp.int32, (tm, LANES), 1)
    oh0 = lane == eid[:, 0:1]
    oh1 = lane == eid[:, 1:2]
    ohs = oh0.astype(F32) + oh1.astype(F32)
    r = lax.broadcasted_iota(jnp.int32, (tm, tm), 0)
    c = lax.broadcasted_iota(jnp.int32, (tm, tm), 1)
    earlier = (c < r).astype(BF16)
    before = jnp.dot(earlier, ohs.astype(BF16), preferred_element_type=F32) + carry_ref[...]
    rank0 = jnp.sum(jnp.where(oh0, before, 0.0), axis=-1, keepdims=True)
    rank1 = jnp.sum(jnp.where(oh1, before, 0.0), axis=-1, keepdims=True)
    rank_ref[...] = jnp.where(lane == 0, rank0, jnp.where(lane == 1, rank1, 0.0)).astype(jnp.int32)
    total = carry_ref[...] + jnp.sum(ohs, axis=0, keepdims=True)
    carry_ref[...] = total
    cnt_ref[...] = jnp.broadcast_to(total, cnt_ref.shape).astype(jnp.int32)


def _rank(eid, tm):
    t = eid.shape[0]
    tm = min(tm, t)
    narrow = pl.BlockSpec((tm, LANES), lambda i: (i, 0))
    return pl.pallas_call(
        _rank_kernel,
        grid=(t // tm,),
        in_specs=[narrow],
        out_specs=[narrow, pl.BlockSpec((8, LANES), lambda i: (0, 0))],
        out_shape=[jax.ShapeDtypeStruct((t, LANES), jnp.int32), jax.ShapeDtypeStruct((8, LANES), jnp.int32)],
        scratch_shapes=[pltpu.VMEM((1, LANES), F32)],
        compiler_params=_params("arbitrary"),
        name="expert_rank",
    )(eid)


def _dest_kernel(eid_ref, rank_ref, ps_ref, o_ref):
    eid = eid_ref[...]
    lane = lax.broadcasted_iota(jnp.int32, eid.shape, 1)
    ps = ps_ref[...]
    s0 = jnp.sum(jnp.where(lane == eid[:, 0:1], ps, 0.0), axis=-1, keepdims=True)
    s1 = jnp.sum(jnp.where(lane == eid[:, 1:2], ps, 0.0), axis=-1, keepdims=True)
    start = jnp.where(lane == 0, s0, jnp.where(lane == 1, s1, 0.0))
    o_ref[...] = start.astype(jnp.int32) + rank_ref[...]


def _dest_rows(eid, rank, pstart, tm):
    t = eid.shape[0]
    tm = min(tm, t)
    ps = jnp.zeros((1, LANES), F32).at[0, :pstart.shape[0]].set(pstart.astype(F32))
    narrow = pl.BlockSpec((tm, LANES), lambda i: (i, 0))
    return pl.pallas_call(
        _dest_kernel,
        grid=(t // tm,),
        in_specs=[narrow, narrow, pl.BlockSpec((1, LANES), lambda i: (0, 0))],
        out_specs=narrow,
        out_shape=jax.ShapeDtypeStruct((t, LANES), jnp.int32),
        compiler_params=_params("arbitrary"),
        name="expert_dest_rows",
    )(eid, rank, ps)


GATHER_UNROLL = 8
GATHER_PRIORITY = 1


def _expert_ffn_kernel(te_ref, trb_ref, nsub_ref, ntot_ref, tok_ref, tok_next_ref, h_hbm, wg_ref, wu_ref, wd_ref,
                       y_ref, land_ref, hid_ref, sem, *, tm, n_up):
    t = pl.program_id(0)
    s = pl.program_id(1)
    n_tiles = pl.num_programs(0)
    nsub = nsub_ref[t]
    slot = t & 1

    def row_copy(tok, r, sl):
        return pltpu.make_async_copy(h_hbm.at[pl.ds(tok, 1)], land_ref.at[sl, pl.ds(r, 1)], sem.at[sl])

    def start_gather(toks_ref, blocks, sl):
        def issue(i, carry):
            for u in range(GATHER_UNROLL):
                r = i * GATHER_UNROLL + u
                row_copy(toks_ref[0, 0, r], r, sl).start(priority=GATHER_PRIORITY)
            return carry

        lax.fori_loop(0, blocks * (ROW_BLOCK // GATHER_UNROLL), issue, 0)

    @pl.when(nsub > 0)
    def _():
        @pl.when(s == 0)
        def _():
            @pl.when(t == 0)
            def _():
                start_gather(tok_ref, nsub, slot)

            def drain(i, carry):
                for u in range(GATHER_UNROLL):
                    row_copy(0, 0, slot).wait()
                return carry

            lax.fori_loop(0, nsub * (ROW_BLOCK // GATHER_UNROLL), drain, 0)

        @pl.when(s == 1)
        def _():
            nxt = jnp.minimum(t + 1, n_tiles - 1)
            n_next = jnp.where(t + 1 < n_tiles, nsub_ref[nxt], 0)
            start_gather(tok_next_ref, n_next, 1 - slot)

        for n in range(1, tm // ROW_BLOCK + 1):
            rows = n * ROW_BLOCK

            @pl.when((nsub == n) & (s < n_up))
            def _(rows=rows):
                x = land_ref[slot, pl.ds(0, rows), :]
                g = jnp.dot(x, wg_ref[...], preferred_element_type=F32)
                u = jnp.dot(x, wu_ref[...], preferred_element_type=F32)
                hid_ref[s, pl.ds(0, rows), :] = g * jax.nn.sigmoid(g) * u

            @pl.when((nsub == n) & (s >= n_up))
            def _(rows=rows):
                hid = jnp.concatenate([hid_ref[c, pl.ds(0, rows), :] for c in range(n_up)], axis=1)
                y_ref[pl.ds(0, rows), :] = jnp.dot(hid, wd_ref[...], preferred_element_type=F32)
                if rows < tm:
                    y_ref[pl.ds(rows, tm - rows), :] = jnp.zeros((tm - rows, y_ref.shape[1]), y_ref.dtype)


def _expert_ffn(h, row_tok, w_gate, w_up, w_down, meta, n_tiles, tm, tf, tn):
    d = h.shape[1]
    f = w_gate.shape[2]
    tf = min(tf, f)
    tn = min(tn, d)
    n_up = f // tf
    n_down = d // tn
    n_steps = n_up + n_down

    def step(t, s, ntot):
        return jnp.where(t < ntot[0], s, n_steps - 1)

    def up_idx(t, s, te, trb, ns, ntot):
        return (te[t], 0, jnp.minimum(step(t, s, ntot), n_up - 1))

    def down_col(t, s, ntot):
        return jnp.maximum(step(t, s, ntot) - n_up, 0)

    grid_spec = pltpu.PrefetchScalarGridSpec(
        num_scalar_prefetch=4,
        grid=(n_tiles, n_steps),
        in_specs=[
            pl.BlockSpec((1, 1, tm), lambda t, s, te, trb, ns, ntot: (trb[t], 0, 0), memory_space=pltpu.SMEM),
            pl.BlockSpec((1, 1, tm), lambda t, s, te, trb, ns, ntot: (trb[jnp.minimum(t + 1, n_tiles - 1)], 0, 0),
                         memory_space=pltpu.SMEM),
            pl.BlockSpec(memory_space=pl.ANY),
            pl.BlockSpec((None, d, tf), up_idx),
            pl.BlockSpec((None, d, tf), up_idx),
            pl.BlockSpec((None, f, tn), lambda t, s, te, trb, ns, ntot: (te[t], 0, down_col(t, s, ntot))),
        ],
        out_specs=pl.BlockSpec((tm, tn), lambda t, s, te, trb, ns, ntot: (trb[t], down_col(t, s, ntot))),
        scratch_shapes=[
            pltpu.VMEM((2, tm, d), F32),
            pltpu.VMEM((n_up, tm, tf), F32),
            pltpu.SemaphoreType.DMA((2,)),
        ],
    )
    return pl.pallas_call(
        functools.partial(_expert_ffn_kernel, tm=tm, n_up=n_up),
        grid_spec=grid_spec,
        out_shape=jax.ShapeDtypeStruct((n_tiles * tm, d), F32),
        compiler_params=_params("arbitrary", "arbitrary"),
        name="expert_ffn",
    )(*meta, row_tok.reshape(n_tiles, 1, tm), row_tok.reshape(n_tiles, 1, tm), h, w_gate, w_up, w_down)


def _combine_kernel(dest_ref, dest_next_ref, wt_ref, x_ref, g_ref, y_hbm, x2_ref, h3_ref, buf_ref, sem, *, tb):
    i = pl.program_id(0)
    slot = i & 1

    def row_copy(src, k, r, sl):
        return pltpu.make_async_copy(y_hbm.at[pl.ds(src, 1)], buf_ref.at[sl, k, pl.ds(r, 1)], sem.at[sl])

    def start_gather(rows_ref, sl):
        def issue(r, carry):
            for k in range(TOP_K):
                row_copy(rows_ref[0, 0, TOP_K * r + k], k, r, sl).start(priority=GATHER_PRIORITY)
            return carry

        lax.fori_loop(0, tb, issue, 0, unroll=GATHER_UNROLL // TOP_K)

    @pl.when(i == 0)
    def _():
        start_gather(dest_ref, slot)

    @pl.when(i + 1 < pl.num_programs(0))
    def _():
        start_gather(dest_next_ref, 1 - slot)

    def drain(r, carry):
        for k in range(TOP_K):
            row_copy(0, k, r, slot).wait()
        return carry

    lax.fori_loop(0, tb, drain, 0, unroll=GATHER_UNROLL // TOP_K)

    wt = wt_ref[...]
    x2 = x_ref[...] + (wt[:, 0:1] * buf_ref[slot, 0] + wt[:, 1:2] * buf_ref[slot, 1])
    x2_ref[...] = x2
    ms = jnp.mean(x2 * x2, axis=-1, keepdims=True)
    h3_ref[...] = (x2 * lax.rsqrt(ms + EPS) * g_ref[...]).astype(h3_ref.dtype)


def _combine(x, y, dest, wts, gain, tb):
    t, d = x.shape
    tb = min(tb, t)
    nb = t // tb
    dest3 = dest.reshape(nb, 1, TOP_K * tb)
    row = pl.BlockSpec((tb, d), lambda i: (i, 0))
    return pl.pallas_call(
        functools.partial(_combine_kernel, tb=tb),
        grid=(nb,),
        in_specs=[
            pl.BlockSpec((1, 1, TOP_K * tb), lambda i: (i, 0, 0), memory_space=pltpu.SMEM),
            pl.BlockSpec((1, 1, TOP_K * tb), lambda i: (jnp.minimum(i + 1, nb - 1), 0, 0), memory_space=pltpu.SMEM),
            pl.BlockSpec((tb, LANES), lambda i: (i, 0)),
            row,
            pl.BlockSpec((1, d), lambda i: (0, 0)),
            pl.BlockSpec(memory_space=pl.ANY),
        ],
        out_specs=[row, row],
        out_shape=[jax.ShapeDtypeStruct((t, d), F32), jax.ShapeDtypeStruct((t, d), BF16)],
        scratch_shapes=[pltpu.VMEM((2, TOP_K, tb, d), F32), pltpu.SemaphoreType.DMA((2,))],
        compiler_params=_params("arbitrary"),
        name="moe_combine_norm",
    )(dest3, dest3, wts, x, gain.reshape(1, d), y)


def _ple_kernel(h_ref, p_ref, wp_ref, x_ref, wg_hbm, o_ref, stage_ref, wgb_ref, wpb_ref, sem, *, tn):
    _staged_weight(wg_hbm, stage_ref, wgb_ref, sem, tn)

    @pl.when(pl.program_id(1) == 0)
    def _():
        wpb_ref[...] = wp_ref[...].astype(BF16)

    a = jnp.dot(h_ref[...], wgb_ref[...], preferred_element_type=F32)
    b = jnp.dot(p_ref[...].astype(BF16), wpb_ref[...], preferred_element_type=F32)
    o_ref[...] = x_ref[...] + jax.nn.sigmoid(a) * b


def _ple(h, p, w_gate, w_proj, x, tm, tn):
    m, k = h.shape
    kp = p.shape[1]
    n = w_gate.shape[1]
    tm = min(tm, m)
    tn = min(tn, n)
    return pl.pallas_call(
        functools.partial(_ple_kernel, tn=tn),
        grid=(n // tn, m // tm),
        in_specs=[
            pl.BlockSpec((tm, k), lambda j, i: (i, 0)),
            pl.BlockSpec((tm, kp), lambda j, i: (i, 0)),
            pl.BlockSpec((kp, tn), lambda j, i: (0, j)),
            pl.BlockSpec((tm, tn), lambda j, i: (i, j)),
            pl.BlockSpec(memory_space=pl.ANY),
        ],
        out_specs=pl.BlockSpec((tm, tn), lambda j, i: (i, j)),
        out_shape=jax.ShapeDtypeStruct((m, n), F32),
        scratch_shapes=[pltpu.VMEM((k, tn), F32), pltpu.VMEM((k, tn), BF16), pltpu.VMEM((kp, tn), BF16),
                        pltpu.SemaphoreType.DMA],
        compiler_params=_params("arbitrary", "arbitrary"),
        name="ple_gate_residual",
    )(h, p, w_proj, x, w_gate)


EXPERT_TILE_ROWS = 640


def _rope_tables(s):
    half = HEAD_DIM // 2
    inv_freq = ROPE_THETA ** (-jnp.arange(half, dtype=F32) / half)
    ang = jnp.arange(s, dtype=jnp.int32).astype(F32)[:, None] * inv_freq[None, :]
    cos, sin = jnp.cos(ang), jnp.sin(ang)
    return jnp.concatenate([cos, cos], axis=-1), jnp.concatenate([-sin, sin], axis=-1)


def _expert_tiles(counts, n_assign, tm):
    n_experts = counts.shape[0]
    n_tiles = -(-n_assign // tm) + n_experts
    tiles_e = (counts + tm - 1) // tm
    tcum = jnp.cumsum(tiles_e)
    tstart = tcum - tiles_e
    pstart = tstart * tm
    total = tcum[-1]
    t = jnp.arange(n_tiles, dtype=jnp.int32)
    tc = jnp.minimum(t, total - 1)
    te = jnp.minimum(jnp.searchsorted(tcum, tc, side="right"), n_experts - 1).astype(jnp.int32)
    local = tc - tstart[te]
    rows_left = counts[te] - local * tm
    nsub = jnp.clip((rows_left + ROW_BLOCK - 1) // ROW_BLOCK, 0, tm // ROW_BLOCK)
    nsub = jnp.where(t < total, nsub, 0).astype(jnp.int32)
    trb = (tstart[te] + local).astype(jnp.int32)
    meta = (te, trb, nsub, jnp.reshape(total, (1,)).astype(jnp.int32))
    return pstart.astype(jnp.int32), meta, n_tiles


def _layer(x, p, w_in, w_conv, w_up_a, w_out_b, w_o, g_mix, g_ffn, w_group, b_group, w_router, b_router,
           w_gate, w_up, w_down, g_ple, w_ple_gate, w_ple_proj):
    s, d = x.shape
    cw = w_conv.shape[1]
    n_experts = w_gate.shape[0]
    ng = len(ATTN_DILATIONS)
    a_out = HEADS_PER_GROUP * HEAD_DIM
    q_cols = ng * a_out
    qkv_cols = q_cols + 2 * a_out
    col_u, col_gb, col_gc = 0, cw, 2 * cw
    col_ma = 3 * cw
    col_mb = col_ma + d

    h = _rmsnorm(x, g_mix, BF16, 256)
    rope_c, rope_s = _rope_tables(s)
    qkv = _qkv_proj(h, w_in, rope_c, rope_s, qkv_cols, q_cols + a_out, 1024, 512)
    proj = _mm_cols(h, w_in, qkv_cols, 512, 1024)
    y_a = _dilated_attention(qkv)
    y_b = _short_conv(proj, w_conv, col_u, col_gb, col_gc, 512, 512)
    merged = _merge(y_a, y_b, proj, w_up_a, w_out_b, col_ma, col_mb, 1024, 1024)
    x1 = _resid_mm(merged, w_o, x, 512, 1024)

    n_route = N_EXPERT_GROUPS + n_experts
    w_route = jnp.pad(jnp.concatenate([w_group, w_router], axis=1), ((0, 0), (0, LANES - n_route)))
    b_route = jnp.pad(jnp.concatenate([b_group, b_router]), (0, LANES - n_route)).reshape(1, LANES)
    h2, eid, wts = _router(x1, g_ffn, w_route, b_route, N_EXPERT_GROUPS, n_experts, 256)
    rank, cnt = _rank(eid, 256)
    counts = cnt[0, :n_experts]
    tm_e = EXPERT_TILE_ROWS
    pstart, meta, n_tiles = _expert_tiles(counts, s * TOP_K, tm_e)
    dest = _dest_rows(eid, rank, pstart, 1024)[:, :TOP_K].reshape(-1)
    tokens = jnp.repeat(jnp.arange(s, dtype=jnp.int32), TOP_K)
    row_tok = jnp.zeros((n_tiles * tm_e,), jnp.int32).at[dest].set(tokens)
    y = _expert_ffn(h2, row_tok, w_gate, w_up, w_down, meta, n_tiles, tm_e, 256, 512)
    x2, h3 = _combine(x1, y, dest, wts, g_ple, 256)

    return _ple(h3, p, w_ple_gate, w_ple_proj, x2, 512, 1024)


def kernel(x, p, w_in, w_conv, w_up_a, w_out_b, w_o, norm_mix, norm_ffn, w_group, b_group, w_router, b_router,
           w_gate, w_up, w_down, norm_ple, w_ple_gate, w_ple_proj, norm_final):
    b, s, d = x.shape
    depth = w_in.shape[0]
    outs = []
    for bi in range(b):
        xb = x[bi]
        for i in range(depth):
            xb = _layer(xb, p[i, bi], w_in[i], w_conv[i], w_up_a[i], w_out_b[i], w_o[i], norm_mix[i], norm_ffn[i],
                        w_group[i], b_group[i], w_router[i], b_router[i], w_gate[i], w_up[i], w_down[i],
                        norm_ple[i], w_ple_gate[i], w_ple_proj[i])
        outs.append(_rmsnorm(xb, norm_final, x.dtype, 256))
    return jnp.stack(outs)
```

```python
import functools

import jax
import jax.numpy as jnp
from jax import lax
from jax.experimental import pallas as pl
from jax.experimental.pallas import tpu as pltpu

F32 = jnp.float32
BF16 = jnp.bfloat16

HEAD_DIM = 128
ATTN_DILATIONS = (1, 4, 16)
HEADS_PER_GROUP = 4
ROPE_THETA = 10000.0
ATTN_BLOCK = 128
NEG_BIG = -1e30
N_EXPERT_GROUPS = 4
TOP_K = 2
EPS = 1e-6
ROW_BLOCK = 128
LANES = 128

VMEM_LIMIT_BYTES = 58 * 1024 * 1024


def _params(*sem):
    return pltpu.CompilerParams(dimension_semantics=sem, vmem_limit_bytes=VMEM_LIMIT_BYTES)


def _rmsnorm_kernel(x_ref, g_ref, o_ref):
    x = x_ref[...]
    ms = jnp.mean(x * x, axis=-1, keepdims=True)
    o_ref[...] = (x * lax.rsqrt(ms + EPS) * g_ref[...]).astype(o_ref.dtype)


def _rmsnorm(x, gain, out_dtype, tm):
    n, d = x.shape
    tm = min(tm, n)
    return pl.pallas_call(
        _rmsnorm_kernel,
        grid=(n // tm,),
        in_specs=[pl.BlockSpec((tm, d), lambda i: (i, 0)), pl.BlockSpec((1, d), lambda i: (0, 0))],
        out_specs=pl.BlockSpec((tm, d), lambda i: (i, 0)),
        out_shape=jax.ShapeDtypeStruct((n, d), out_dtype),
        compiler_params=_params("arbitrary"),
        name="rmsnorm",
    )(x, gain.reshape(1, d))


def _qkv_proj_kernel(h_ref, w_ref, c_ref, s_ref, o_ref, wb_ref, *, n_rope_tiles):
    j = pl.program_id(0)
    i = pl.program_id(1)

    @pl.when(i == 0)
    def _():
        wb_ref[...] = w_ref[...].astype(BF16)

    acc = jnp.dot(h_ref[...], wb_ref[...], preferred_element_type=F32)
    heads = o_ref.shape[0]

    @pl.when(j < n_rope_tiles)
    def _():
        c = c_ref[...]
        s = s_ref[...]
        for hh in range(heads):
            t = acc[:, hh * HEAD_DIM:(hh + 1) * HEAD_DIM]
            o_ref[hh] = t * c + pltpu.roll(t, HEAD_DIM // 2, 1) * s

    @pl.when(j >= n_rope_tiles)
    def _():
        for hh in range(heads):
            o_ref[hh] = acc[:, hh * HEAD_DIM:(hh + 1) * HEAD_DIM]


def _qkv_proj(h, w, rope_c, rope_s, n_cols, n_rope_cols, tm, tn):
    m, k = h.shape
    tm = min(tm, m)
    assert n_cols % tn == 0 and n_rope_cols % tn == 0 and m % tm == 0 and tn % HEAD_DIM == 0
    heads = tn // HEAD_DIM
    return pl.pallas_call(
        functools.partial(_qkv_proj_kernel, n_rope_tiles=n_rope_cols // tn),
        grid=(n_cols // tn, m // tm),
        in_specs=[
            pl.BlockSpec((tm, k), lambda j, i: (i, 0)),
            pl.BlockSpec((k, tn), lambda j, i: (0, j)),
            pl.BlockSpec((tm, HEAD_DIM), lambda j, i: (i, 0)),
            pl.BlockSpec((tm, HEAD_DIM), lambda j, i: (i, 0)),
        ],
        out_specs=pl.BlockSpec((heads, tm, HEAD_DIM), lambda j, i: (j, i, 0)),
        out_shape=jax.ShapeDtypeStruct((n_cols // HEAD_DIM, m, HEAD_DIM), F32),
        scratch_shapes=[pltpu.VMEM((k, tn), BF16)],
        compiler_params=_params("arbitrary", "arbitrary"),
        name="qkv_proj_rope",
    )(h, w, rope_c, rope_s)


def _staged_weight(w_hbm, stage_ref, wb_ref, sem, tn, col0=0):
    j = pl.program_id(0)
    i = pl.program_id(1)

    def fetch(jj):
        cols = pl.ds(pl.multiple_of(col0 + jj * tn, LANES), tn)
        return pltpu.make_async_copy(w_hbm.at[:, cols], stage_ref, sem)

    @pl.when(i == 0)
    def _():
        @pl.when(j == 0)
        def _():
            fetch(0).start()

        fetch(j).wait()
        wb_ref[...] = stage_ref[...].astype(BF16)

        @pl.when(j + 1 < pl.num_programs(0))
        def _():
            fetch(j + 1).start()


def _mm_kernel(a_ref, w_hbm, o_ref, stage_ref, wb_ref, sem, *, tn, col0):
    _staged_weight(w_hbm, stage_ref, wb_ref, sem, tn, col0)
    o_ref[...] = jnp.dot(a_ref[...], wb_ref[...], preferred_element_type=F32).astype(o_ref.dtype)


def _mm_cols(a, w, col0, tm, tn):
    m, k = a.shape
    n = w.shape[1] - col0
    tm = min(tm, m)
    while n % tn:
        tn //= 2
    assert tn % LANES == 0 and col0 % LANES == 0 and m % tm == 0
    return pl.pallas_call(
        functools.partial(_mm_kernel, tn=tn, col0=col0),
        grid=(n // tn, m // tm),
        in_specs=[
            pl.BlockSpec((tm, k), lambda j, i: (i, 0)),
            pl.BlockSpec(memory_space=pl.ANY),
        ],
        out_specs=pl.BlockSpec((tm, tn), lambda j, i: (i, j)),
        out_shape=jax.ShapeDtypeStruct((m, n), BF16),
        scratch_shapes=[pltpu.VMEM((k, tn), F32), pltpu.VMEM((k, tn), BF16), pltpu.SemaphoreType.DMA],
        compiler_params=_params("arbitrary", "arbitrary"),
        name="in_proj",
    )(a, w)


ATTN_SUPER = ATTN_BLOCK * max(ATTN_DILATIONS)
ATTN_UNROLL = 16


def _rows(start, size, stride):
    return pl.ds(start, size) if stride == 1 else pl.ds(start, size, stride=stride)


def _dilated_attn_kernel(q0_ref, q1_ref, q2_ref, kc_ref, kp_ref, vc_ref, vp_ref, y_ref,
                         kcat_ref, vcat_ref, o_ref, lse_ref):
    sb = pl.program_id(1)
    sup = ATTN_SUPER
    kcat_ref[pl.ds(0, sup), :] = kp_ref[...]
    kcat_ref[pl.ds(sup, sup), :] = kc_ref[...]
    vcat_ref[pl.ds(0, sup), :] = vp_ref[...]
    vcat_ref[pl.ds(sup, sup), :] = vc_ref[...]
    qi = lax.broadcasted_iota(jnp.int32, (ATTN_BLOCK, 2 * ATTN_BLOCK), 0)
    kj = lax.broadcasted_iota(jnp.int32, (ATTN_BLOCK, 2 * ATTN_BLOCK), 1)
    dist = qi + ATTN_BLOCK - kj
    band = (dist >= 0) & (dist <= ATTN_BLOCK)
    in_cur = kj >= ATTN_BLOCK
    scale = HEAD_DIM ** -0.5

    for g, (q_ref, d) in enumerate(zip((q0_ref, q1_ref, q2_ref), ATTN_DILATIONS)):
        span = ATTN_BLOCK * d
        shift = d.bit_length() - 1

        def unit(u, carry, g=g, q_ref=q_ref, d=d, span=span, shift=shift):
            j = lax.shift_right_logical(u, shift)
            row0 = j * span + (u & (d - 1))
            q = q_ref[_rows(row0, ATTN_BLOCK, d), :].astype(BF16)
            k = kcat_ref[_rows(sup - span + row0, 2 * ATTN_BLOCK, d), :].astype(BF16)
            v = vcat_ref[_rows(sup - span + row0, 2 * ATTN_BLOCK, d), :].astype(BF16)
            s = lax.dot_general(q, k, (((1,), (1,)), ((), ())), preferred_element_type=F32) * scale
            has_prev = (sb > 0) | (j > 0)
            s = jnp.where(band & (in_cur | has_prev), s, NEG_BIG)
            m = jnp.max(s, axis=-1, keepdims=True)
            e = jnp.exp(s - m)
            ssum = jnp.sum(e, axis=-1, keepdims=True)
            p = (e * (1.0 / ssum)).astype(BF16)
            o_ref[g, _rows(row0, ATTN_BLOCK, d), :] = jnp.dot(p, v, preferred_element_type=F32)
            lse_ref[g, _rows(row0, ATTN_BLOCK, d), :] = jnp.broadcast_to(m + jnp.log(ssum), (ATTN_BLOCK, HEAD_DIM))
            return carry

        lax.fori_loop(0, sup // ATTN_BLOCK, unit, 0, unroll=ATTN_UNROLL)

    lse = lse_ref[...]
    m = jnp.max(lse, axis=0)
    e = jnp.exp(lse - m[None])
    w = e / jnp.sum(e, axis=0)[None]
    y_ref[...] = jnp.sum(w * o_ref[...], axis=0).astype(y_ref.dtype)


def _dilated_attention(qkv):
    nh, s, hd = qkv.shape
    ng = len(ATTN_DILATIONS)
    hpg = HEADS_PER_GROUP
    assert nh == (ng + 2) * hpg and hd == HEAD_DIM and s % ATTN_SUPER == 0
    sup = ATTN_SUPER

    def q_spec(g):
        return pl.BlockSpec((None, sup, hd), lambda h, b: (g * hpg + h, b, 0))

    def kv_spec(first_head, prev):
        if prev:
            return pl.BlockSpec((None, sup, hd), lambda h, b: (first_head + h, jnp.maximum(b - 1, 0), 0))
        return pl.BlockSpec((None, sup, hd), lambda h, b: (first_head + h, b, 0))

    k0, v0 = ng * hpg, (ng + 1) * hpg
    return pl.pallas_call(
        _dilated_attn_kernel,
        grid=(hpg, s // sup),
        in_specs=[q_spec(0), q_spec(1), q_spec(2), kv_spec(k0, False), kv_spec(k0, True),
                  kv_spec(v0, False), kv_spec(v0, True)],
        out_specs=pl.BlockSpec((sup, hd), lambda h, b: (b, h)),
        out_shape=jax.ShapeDtypeStruct((s, hpg * hd), BF16),
        scratch_shapes=[pltpu.VMEM((2 * sup, hd), F32), pltpu.VMEM((2 * sup, hd), F32),
                        pltpu.VMEM((ng, sup, hd), F32), pltpu.VMEM((ng, sup, hd), F32)],
        compiler_params=_params("arbitrary", "arbitrary"),
        name="dilated_attention",
    )(qkv, qkv, qkv, qkv, qkv, qkv, qkv)


CONV_HALO = 16


def _conv_kernel(u_ref, gc_ref, gb_ref, up_ref, gcp_ref, w_ref, o_ref):
    i = pl.program_id(0)
    cur = u_ref[...].astype(F32) * gc_ref[...].astype(F32)
    prev = up_ref[...].astype(F32) * gcp_ref[...].astype(F32)
    prev = jnp.where(i == 0, 0.0, prev)
    big = jnp.concatenate([prev, cur], axis=0)
    back1 = pltpu.roll(big, 1, 0)[CONV_HALO:]
    back2 = pltpu.roll(big, 2, 0)[CONV_HALO:]
    w = w_ref[...]
    y = w[0:1] * back2 + w[1:2] * back1 + w[2:3] * cur
    o_ref[...] = (gb_ref[...].astype(F32) * y).astype(o_ref.dtype)


def _short_conv(proj, w_conv, col_u, col_gb, col_gc, tm, tc):
    s = proj.shape[0]
    kw, cw = w_conv.shape
    assert kw == 3
    tm = min(tm, s)
    tc = min(tc, cw)
    assert col_u % tc == 0 and col_gb % tc == 0 and col_gc % tc == 0 and tm % CONV_HALO == 0
    hb = tm // CONV_HALO

    def cur(col):
        return pl.BlockSpec((tm, tc), lambda i, j: (i, col // tc + j))

    def prev(col):
        return pl.BlockSpec((CONV_HALO, tc), lambda i, j: (jnp.maximum(i * hb - 1, 0), col // tc + j))

    return pl.pallas_call(
        _conv_kernel,
        grid=(s // tm, cw // tc),
        in_specs=[cur(col_u), cur(col_gc), cur(col_gb), prev(col_u), prev(col_gc),
                  pl.BlockSpec((kw, tc), lambda i, j: (0, j))],
        out_specs=pl.BlockSpec((tm, tc), lambda i, j: (i, j)),
        out_shape=jax.ShapeDtypeStruct((s, cw), BF16),
        compiler_params=_params("arbitrary", "arbitrary"),
        name="short_conv",
    )(proj, proj, proj, proj, proj, w_conv)


def _merge_kernel(ya_ref, yb_ref, wa_ref, wb_ref, ma_ref, mb_ref, o_ref, wab_ref, wbb_ref):
    i = pl.program_id(1)

    @pl.when(i == 0)
    def _():
        wab_ref[...] = wa_ref[...].astype(BF16)
        wbb_ref[...] = wb_ref[...].astype(BF16)

    a = jnp.dot(ya_ref[...], wab_ref[...], preferred_element_type=F32)
    b = jnp.dot(yb_ref[...], wbb_ref[...], preferred_element_type=F32)
    ga = jax.nn.sigmoid(ma_ref[...].astype(F32))
    gb = jax.nn.sigmoid(mb_ref[...].astype(F32))
    o_ref[...] = (ga * a + gb * b).astype(o_ref.dtype)


def _merge(y_a, y_b, proj, w_a, w_b, col_ma, col_mb, tm, tn):
    m, ka = y_a.shape
    kb = y_b.shape[1]
    n = w_a.shape[1]
    tm = min(tm, m)
    tn = min(tn, n)
    while col_ma % tn or col_mb % tn:
        tn //= 2
    assert tn % LANES == 0
    return pl.pallas_call(
        _merge_kernel,
        grid=(n // tn, m // tm),
        in_specs=[
            pl.BlockSpec((tm, ka), lambda j, i: (i, 0)),
            pl.BlockSpec((tm, kb), lambda j, i: (i, 0)),
            pl.BlockSpec((ka, tn), lambda j, i: (0, j)),
            pl.BlockSpec((kb, tn), lambda j, i: (0, j)),
            pl.BlockSpec((tm, tn), lambda j, i: (i, col_ma // tn + j)),
            pl.BlockSpec((tm, tn), lambda j, i: (i, col_mb // tn + j)),
        ],
        out_specs=pl.BlockSpec((tm, tn), lambda j, i: (i, j)),
        out_shape=jax.ShapeDtypeStruct((m, n), BF16),
        scratch_shapes=[pltpu.VMEM((ka, tn), BF16), pltpu.VMEM((kb, tn), BF16)],
        compiler_params=_params("arbitrary", "arbitrary"),
        name="mixer_merge",
    )(y_a, y_b, w_a, w_b, proj, proj)


def _resid_mm_kernel(a_ref, x_ref, w_hbm, o_ref, stage_ref, wb_ref, sem, *, tn):
    _staged_weight(w_hbm, stage_ref, wb_ref, sem, tn)
    o_ref[...] = x_ref[...] + jnp.dot(a_ref[...], wb_ref[...], preferred_element_type=F32)


def _resid_mm(a, w, x, tm, tn):
    m, k = a.shape
    n = w.shape[1]
    tm = min(tm, m)
    tn = min(tn, n)
    return pl.pallas_call(
        functools.partial(_resid_mm_kernel, tn=tn),
        grid=(n // tn, m // tm),
        in_specs=[
            pl.BlockSpec((tm, k), lambda j, i: (i, 0)),
            pl.BlockSpec((tm, tn), lambda j, i: (i, j)),
            pl.BlockSpec(memory_space=pl.ANY),
        ],
        out_specs=pl.BlockSpec((tm, tn), lambda j, i: (i, j)),
        out_shape=jax.ShapeDtypeStruct((m, n), F32),
        scratch_shapes=[pltpu.VMEM((k, tn), F32), pltpu.VMEM((k, tn), BF16), pltpu.SemaphoreType.DMA],
        compiler_params=_params("arbitrary", "arbitrary"),
        name="out_proj_residual",
    )(a, x, w)


def _split_bf16(a):
    hi = a.astype(BF16)
    lo = (a - hi.astype(F32)).astype(BF16)
    return hi, lo


def _pack_bf16_pair(a):
    c = a.shape[1] // 2
    bits = lax.bitcast_convert_type(a.astype(BF16).astype(F32), jnp.uint32)
    return (bits[:, :c] >> 16) | (bits[:, c:] & jnp.uint32(0xFFFF0000))


def _unpack_bf16_pair(w):
    lo = lax.bitcast_convert_type(w << 16, F32)
    hi = lax.bitcast_convert_type(w & jnp.uint32(0xFFFF0000), F32)
    return jnp.concatenate([lo, hi], axis=1)


def _router_kernel(x_ref, g_ref, w_ref, b_ref, h_ref, eid_ref, wt_ref, *, n_groups, n_experts):
    x = x_ref[...]
    ms = jnp.mean(x * x, axis=-1, keepdims=True)
    h = x * lax.rsqrt(ms + EPS) * g_ref[...]
    h_ref[...] = _pack_bf16_pair(h)
    hh, hl = _split_bf16(h)
    wh, wl = _split_bf16(w_ref[...])
    logits = (jnp.dot(hh, wh, preferred_element_type=F32) + jnp.dot(hh, wl, preferred_element_type=F32)
              + jnp.dot(hl, wh, preferred_element_type=F32)) + b_ref[...]
    tm = x.shape[0]
    per_group = n_experts // n_groups
    lane = lax.broadcasted_iota(jnp.int32, (tm, LANES), 1)
    is_g = lane < n_groups
    gl = jnp.where(is_g, logits, NEG_BIG)
    gmax = jnp.max(gl, axis=-1, keepdims=True)
    gidx = jnp.min(jnp.where(gl == gmax, lane, LANES), axis=-1, keepdims=True)
    gsum = jnp.sum(jnp.where(is_g, jnp.exp(gl - gmax), 0.0), axis=-1, keepdims=True)
    g_w = 1.0 / gsum
    lane_e = lane - n_groups
    sel = (lane_e >= 0) & (lane_e < n_experts) & (lane_e // per_group == gidx)
    el = jnp.where(sel, logits, NEG_BIG)
    emax = jnp.max(el, axis=-1, keepdims=True)
    i1 = jnp.min(jnp.where(el == emax, lane, LANES), axis=-1, keepdims=True)
    esum = jnp.sum(jnp.where(sel, jnp.exp(el - emax), 0.0), axis=-1, keepdims=True)
    el2 = jnp.where(lane == i1, NEG_BIG, el)
    emax2 = jnp.max(el2, axis=-1, keepdims=True)
    i2 = jnp.min(jnp.where(el2 == emax2, lane, LANES), axis=-1, keepdims=True)
    p1 = 1.0 / esum
    p2 = jnp.exp(emax2 - emax) / esum
    den = p1 + p2
    eid_ref[...] = jnp.where(lane == 0, i1 - n_groups, jnp.where(lane == 1, i2 - n_groups, 0))
    wt_ref[...] = jnp.where(lane == 0, g_w * (p1 / den), jnp.where(lane == 1, g_w * (p2 / den), 0.0))


def _router(x, gain, w_route, b_route, n_groups, n_experts, tm):
    t, d = x.shape
    tm = min(tm, t)
    row = pl.BlockSpec((tm, d), lambda i: (i, 0))
    narrow = pl.BlockSpec((tm, LANES), lambda i: (i, 0))
    return pl.pallas_call(
        functools.partial(_router_kernel, n_groups=n_groups, n_experts=n_experts),
        grid=(t // tm,),
        in_specs=[row, pl.BlockSpec((1, d), lambda i: (0, 0)), pl.BlockSpec((d, LANES), lambda i: (0, 0)),
                  pl.BlockSpec((1, LANES), lambda i: (0, 0))],
        out_specs=[pl.BlockSpec((tm, d // 2), lambda i: (i, 0)), narrow, narrow],
        out_shape=[jax.ShapeDtypeStruct((t, d // 2), jnp.uint32), jax.ShapeDtypeStruct((t, LANES), jnp.int32),
                   jax.ShapeDtypeStruct((t, LANES), F32)],
        compiler_params=_params("arbitrary"),
        name="ffn_norm_router",
    )(x, gain.reshape(1, d), w_route, b_route)


def _rank_kernel(eid_ref, rank_ref, cnt_ref, carry_ref):
    i = pl.program_id(0)

    @pl.when(i == 0)
    def _():
        carry_ref[...] = jnp.zeros_like(carry_ref)

    eid = eid_ref[...]
    tm = eid.shape[0]
    lane = lax.broadcasted_iota(jnp.int32, (tm, LANES), 1)
    oh0 = lane == eid[:, 0:1]
    oh1 = lane == eid[:, 1:2]
    ohs = oh0.astype(F32) + oh1.astype(F32)
    r = lax.broadcasted_iota(jnp.int32, (tm, tm), 0)
    c = lax.broadcasted_iota(jnp.int32, (tm, tm), 1)
    earlier = (c < r).astype(BF16)
    before = jnp.dot(earlier, ohs.astype(BF16), preferred_element_type=F32) + carry_ref[...]
    rank0 = jnp.sum(jnp.where(oh0, before, 0.0), axis=-1, keepdims=True)
    rank1 = jnp.sum(jnp.where(oh1, before, 0.0), axis=-1, keepdims=True)
    rank_ref[...] = jnp.where(lane == 0, rank0, jnp.where(lane == 1, rank1, 0.0)).astype(jnp.int32)
    total = carry_ref[...] + jnp.sum(ohs, axis=0, keepdims=True)
    carry_ref[...] = total
    cnt_ref[...] = jnp.broadcast_to(total, cnt_ref.shape).astype(jnp.int32)


def _rank(eid, tm):
    t = eid.shape[0]
    tm = min(tm, t)
    narrow = pl.BlockSpec((tm, LANES), lambda i: (i, 0))
    return pl.pallas_call(
        _rank_kernel,
        grid=(t // tm,),
        in_specs=[narrow],
        out_specs=[narrow, pl.BlockSpec((8, LANES), lambda i: (0, 0))],
        out_shape=[jax.ShapeDtypeStruct((t, LANES), jnp.int32), jax.ShapeDtypeStruct((8, LANES), jnp.int32)],
        scratch_shapes=[pltpu.VMEM((1, LANES), F32)],
        compiler_params=_params("arbitrary"),
        name="expert_rank",
    )(eid)


def _dest_kernel(eid_ref, rank_ref, ps_ref, o_ref):
    eid = eid_ref[...]
    lane = lax.broadcasted_iota(jnp.int32, eid.shape, 1)
    ps = ps_ref[...]
    s0 = jnp.sum(jnp.where(lane == eid[:, 0:1], ps, 0.0), axis=-1, keepdims=True)
    s1 = jnp.sum(jnp.where(lane == eid[:, 1:2], ps, 0.0), axis=-1, keepdims=True)
    start = jnp.where(lane == 0, s0, jnp.where(lane == 1, s1, 0.0))
    o_ref[...] = start.astype(jnp.int32) + rank_ref[...]


def _dest_rows(eid, rank, pstart, tm):
    t = eid.shape[0]
    tm = min(tm, t)
    ps = jnp.zeros((1, LANES), F32).at[0, :pstart.shape[0]].set(pstart.astype(F32))
    narrow = pl.BlockSpec((tm, LANES), lambda i: (i, 0))
    return pl.pallas_call(
        _dest_kernel,
        grid=(t // tm,),
        in_specs=[narrow, narrow, pl.BlockSpec((1, LANES), lambda i: (0, 0))],
        out_specs=narrow,
        out_shape=jax.ShapeDtypeStruct((t, LANES), jnp.int32),
        compiler_params=_params("arbitrary"),
        name="expert_dest_rows",
    )(eid, rank, ps)


GATHER_UNROLL = 8
GATHER_PRIORITY = 1


def _expert_ffn_kernel(te_ref, trb_ref, nsub_ref, ntot_ref, tok_ref, tok_next_ref, h_hbm, wg_ref, wu_ref, wd_ref,
                       y_ref, land_ref, hid_ref, sem, *, tm, n_up):
    t = pl.program_id(0)
    s = pl.program_id(1)
    n_tiles = pl.num_programs(0)
    nsub = nsub_ref[t]
    slot = t & 1

    def row_copy(tok, r, sl):
        return pltpu.make_async_copy(h_hbm.at[pl.ds(tok, 1)], land_ref.at[sl, pl.ds(r, 1)], sem.at[sl])

    def start_gather(toks_ref, blocks, sl):
        def issue(i, carry):
            for u in range(GATHER_UNROLL):
                r = i * GATHER_UNROLL + u
                row_copy(toks_ref[0, 0, r], r, sl).start(priority=GATHER_PRIORITY)
            return carry

        lax.fori_loop(0, blocks * (ROW_BLOCK // GATHER_UNROLL), issue, 0)

    @pl.when(nsub > 0)
    def _():
        @pl.when(s == 0)
        def _():
            @pl.when(t == 0)
            def _():
                start_gather(tok_ref, nsub, slot)

            def drain(i, carry):
                for u in range(GATHER_UNROLL):
                    row_copy(0, 0, slot).wait()
                return carry

            lax.fori_loop(0, nsub * (ROW_BLOCK // GATHER_UNROLL), drain, 0)

        @pl.when(s == 1)
        def _():
            nxt = jnp.minimum(t + 1, n_tiles - 1)
            n_next = jnp.where(t + 1 < n_tiles, nsub_ref[nxt], 0)
            start_gather(tok_next_ref, n_next, 1 - slot)

        for n in range(1, tm // ROW_BLOCK + 1):
            rows = n * ROW_BLOCK

            @pl.when((nsub == n) & (s < n_up))
            def _(rows=rows):
                x = _unpack_bf16_pair(land_ref[slot, pl.ds(0, rows), :])
                g = jnp.dot(x, wg_ref[...], preferred_element_type=F32)
                u = jnp.dot(x, wu_ref[...], preferred_element_type=F32)
                hid_ref[s, pl.ds(0, rows), :] = g * jax.nn.sigmoid(g) * u

            @pl.when((nsub == n) & (s >= n_up))
            def _(rows=rows):
                hid = jnp.concatenate([hid_ref[c, pl.ds(0, rows), :] for c in range(n_up)], axis=1)
                y = jnp.dot(hid, wd_ref[...], preferred_element_type=F32)
                y_ref[pl.ds(0, rows), :] = _pack_bf16_pair(y)
                if rows < tm:
                    y_ref[pl.ds(rows, tm - rows), :] = jnp.zeros((tm - rows, y_ref.shape[1]), y_ref.dtype)


def _expert_ffn(h, row_tok, w_gate, w_up, w_down, meta, n_tiles, tm, tf, tn):
    d = w_gate.shape[1]
    f = w_gate.shape[2]
    tf = min(tf, f)
    tn = min(tn, d)
    n_up = f // tf
    n_down = d // tn
    n_steps = n_up + n_down

    def step(t, s, ntot):
        return jnp.where(t < ntot[0], s, n_steps - 1)

    def up_idx(t, s, te, trb, ns, ntot):
        return (te[t], 0, jnp.minimum(step(t, s, ntot), n_up - 1))

    def down_col(t, s, ntot):
        return jnp.maximum(step(t, s, ntot) - n_up, 0)

    grid_spec = pltpu.PrefetchScalarGridSpec(
        num_scalar_prefetch=4,
        grid=(n_tiles, n_steps),
        in_specs=[
            pl.BlockSpec((1, 1, tm), lambda t, s, te, trb, ns, ntot: (trb[t], 0, 0), memory_space=pltpu.SMEM),
            pl.BlockSpec((1, 1, tm), lambda t, s, te, trb, ns, ntot: (trb[jnp.minimum(t + 1, n_tiles - 1)], 0, 0),
                         memory_space=pltpu.SMEM),
            pl.BlockSpec(memory_space=pl.ANY),
            pl.BlockSpec((None, d, tf), up_idx),
            pl.BlockSpec((None, d, tf), up_idx),
            pl.BlockSpec((None, f, tn), lambda t, s, te, trb, ns, ntot: (te[t], 0, down_col(t, s, ntot))),
        ],
        out_specs=pl.BlockSpec((tm, tn // 2), lambda t, s, te, trb, ns, ntot: (trb[t], down_col(t, s, ntot))),
        scratch_shapes=[
            pltpu.VMEM((2, tm, d // 2), jnp.uint32),
            pltpu.VMEM((n_up, tm, tf), F32),
            pltpu.SemaphoreType.DMA((2,)),
        ],
    )
    return pl.pallas_call(
        functools.partial(_expert_ffn_kernel, tm=tm, n_up=n_up),
        grid_spec=grid_spec,
        out_shape=jax.ShapeDtypeStruct((n_tiles * tm, d // 2), jnp.uint32),
        compiler_params=_params("arbitrary", "arbitrary"),
        name="expert_ffn",
    )(*meta, row_tok.reshape(n_tiles, 1, tm), row_tok.reshape(n_tiles, 1, tm), h, w_gate, w_up, w_down)


def _combine_kernel(dest_ref, dest_next_ref, wt_ref, x_ref, g_ref, y_hbm, x2_ref, h3_ref, buf_ref, sem, *, tb, chunk):
    i = pl.program_id(0)
    slot = i & 1

    def row_copy(src, k, r, sl):
        return pltpu.make_async_copy(y_hbm.at[pl.ds(src, 1)], buf_ref.at[sl, k, pl.ds(r, 1)], sem.at[sl])

    def start_gather(rows_ref, sl):
        def issue(r, carry):
            for k in range(TOP_K):
                row_copy(rows_ref[0, 0, TOP_K * r + k], k, r, sl).start(priority=GATHER_PRIORITY)
            return carry

        lax.fori_loop(0, tb, issue, 0, unroll=GATHER_UNROLL // TOP_K)

    @pl.when(i == 0)
    def _():
        start_gather(dest_ref, slot)

    @pl.when(i + 1 < pl.num_programs(0))
    def _():
        start_gather(dest_next_ref, 1 - slot)

    def drain(r, carry):
        for k in range(TOP_K):
            row_copy(0, k, r, slot).wait()
        return carry

    lax.fori_loop(0, tb, drain, 0, unroll=GATHER_UNROLL // TOP_K)

    def expert_rows(k):
        words = buf_ref[slot, k]
        half = chunk // 2
        return jnp.concatenate([_unpack_bf16_pair(words[:, c * half:(c + 1) * half])
                                for c in range(words.shape[1] // half)], axis=1)

    wt = wt_ref[...]
    x2 = x_ref[...] + (wt[:, 0:1] * expert_rows(0) + wt[:, 1:2] * expert_rows(1))
    x2_ref[...] = x2
    ms = jnp.mean(x2 * x2, axis=-1, keepdims=True)
    h3_ref[...] = (x2 * lax.rsqrt(ms + EPS) * g_ref[...]).astype(h3_ref.dtype)


def _combine(x, y, dest, wts, gain, tb, chunk):
    t, d = x.shape
    tb = min(tb, t)
    nb = t // tb
    dest3 = dest.reshape(nb, 1, TOP_K * tb)
    row = pl.BlockSpec((tb, d), lambda i: (i, 0))
    return pl.pallas_call(
        functools.partial(_combine_kernel, tb=tb, chunk=chunk),
        grid=(nb,),
        in_specs=[
            pl.BlockSpec((1, 1, TOP_K * tb), lambda i: (i, 0, 0), memory_space=pltpu.SMEM),
            pl.BlockSpec((1, 1, TOP_K * tb), lambda i: (jnp.minimum(i + 1, nb - 1), 0, 0), memory_space=pltpu.SMEM),
            pl.BlockSpec((tb, LANES), lambda i: (i, 0)),
            row,
            pl.BlockSpec((1, d), lambda i: (0, 0)),
            pl.BlockSpec(memory_space=pl.ANY),
        ],
        out_specs=[row, row],
        out_shape=[jax.ShapeDtypeStruct((t, d), F32), jax.ShapeDtypeStruct((t, d), BF16)],
        scratch_shapes=[pltpu.VMEM((2, TOP_K, tb, d // 2), jnp.uint32), pltpu.SemaphoreType.DMA((2,))],
        compiler_params=_params("arbitrary"),
        name="moe_combine_norm",
    )(dest3, dest3, wts, x, gain.reshape(1, d), y)


def _ple_kernel(h_ref, p_ref, wp_ref, x_ref, wg_hbm, o_ref, stage_ref, wgb_ref, wpb_ref, sem, *, tn):
    _staged_weight(wg_hbm, stage_ref, wgb_ref, sem, tn)

    @pl.when(pl.program_id(1) == 0)
    def _():
        wpb_ref[...] = wp_ref[...].astype(BF16)

    a = jnp.dot(h_ref[...], wgb_ref[...], preferred_element_type=F32)
    b = jnp.dot(p_ref[...].astype(BF16), wpb_ref[...], preferred_element_type=F32)
    o_ref[...] = x_ref[...] + jax.nn.sigmoid(a) * b


def _ple(h, p, w_gate, w_proj, x, tm, tn):
    m, k = h.shape
    kp = p.shape[1]
    n = w_gate.shape[1]
    tm = min(tm, m)
    tn = min(tn, n)
    return pl.pallas_call(
        functools.partial(_ple_kernel, tn=tn),
        grid=(n // tn, m // tm),
        in_specs=[
            pl.BlockSpec((tm, k), lambda j, i: (i, 0)),
            pl.BlockSpec((tm, kp), lambda j, i: (i, 0)),
            pl.BlockSpec((kp, tn), lambda j, i: (0, j)),
            pl.BlockSpec((tm, tn), lambda j, i: (i, j)),
            pl.BlockSpec(memory_space=pl.ANY),
        ],
        out_specs=pl.BlockSpec((tm, tn), lambda j, i: (i, j)),
        out_shape=jax.ShapeDtypeStruct((m, n), F32),
        scratch_shapes=[pltpu.VMEM((k, tn), F32), pltpu.VMEM((k, tn), BF16), pltpu.VMEM((kp, tn), BF16),
                        pltpu.SemaphoreType.DMA],
        compiler_params=_params("arbitrary", "arbitrary"),
        name="ple_gate_residual",
    )(h, p, w_proj, x, w_gate)


EXPERT_TILE_ROWS = 640


def _rope_tables(s):
    half = HEAD_DIM // 2
    inv_freq = ROPE_THETA ** (-jnp.arange(half, dtype=F32) / half)
    ang = jnp.arange(s, dtype=jnp.int32).astype(F32)[:, None] * inv_freq[None, :]
    cos, sin = jnp.cos(ang), jnp.sin(ang)
    return jnp.concatenate([cos, cos], axis=-1), jnp.concatenate([-sin, sin], axis=-1)


def _expert_tiles(counts, n_assign, tm):
    n_experts = counts.shape[0]
    n_tiles = -(-n_assign // tm) + n_experts
    tiles_e = (counts + tm - 1) // tm
    tcum = jnp.cumsum(tiles_e)
    tstart = tcum - tiles_e
    pstart = tstart * tm
    total = tcum[-1]
    t = jnp.arange(n_tiles, dtype=jnp.int32)
    tc = jnp.minimum(t, total - 1)
    te = jnp.minimum(jnp.searchsorted(tcum, tc, side="right"), n_experts - 1).astype(jnp.int32)
    local = tc - tstart[te]
    rows_left = counts[te] - local * tm
    nsub = jnp.clip((rows_left + ROW_BLOCK - 1) // ROW_BLOCK, 0, tm // ROW_BLOCK)
    nsub = jnp.where(t < total, nsub, 0).astype(jnp.int32)
    trb = (tstart[te] + local).astype(jnp.int32)
    meta = (te, trb, nsub, jnp.reshape(total, (1,)).astype(jnp.int32))
    return pstart.astype(jnp.int32), meta, n_tiles


def _layer(x, p, w_in, w_conv, w_up_a, w_out_b, w_o, g_mix, g_ffn, w_group, b_group, w_router, b_router,
           w_gate, w_up, w_down, g_ple, w_ple_gate, w_ple_proj):
    s, d = x.shape
    cw = w_conv.shape[1]
    n_experts = w_gate.shape[0]
    ng = len(ATTN_DILATIONS)
    a_out = HEADS_PER_GROUP * HEAD_DIM
    q_cols = ng * a_out
    qkv_cols = q_cols + 2 * a_out
    col_u, col_gb, col_gc = 0, cw, 2 * cw
    col_ma = 3 * cw
    col_mb = col_ma + d

    h = _rmsnorm(x, g_mix, BF16, 256)
    rope_c, rope_s = _rope_tables(s)
    qkv = _qkv_proj(h, w_in, rope_c, rope_s, qkv_cols, q_cols + a_out, 1024, 512)
    proj = _mm_cols(h, w_in, qkv_cols, 512, 1024)
    y_a = _dilated_attention(qkv)
    y_b = _short_conv(proj, w_conv, col_u, col_gb, col_gc, 512, 512)
    merged = _merge(y_a, y_b, proj, w_up_a, w_out_b, col_ma, col_mb, 1024, 1024)
    x1 = _resid_mm(merged, w_o, x, 512, 1024)

    n_route = N_EXPERT_GROUPS + n_experts
    w_route = jnp.pad(jnp.concatenate([w_group, w_router], axis=1), ((0, 0), (0, LANES - n_route)))
    b_route = jnp.pad(jnp.concatenate([b_group, b_router]), (0, LANES - n_route)).reshape(1, LANES)
    h2, eid, wts = _router(x1, g_ffn, w_route, b_route, N_EXPERT_GROUPS, n_experts, 256)
    rank, cnt = _rank(eid, 256)
    counts = cnt[0, :n_experts]
    tm_e = EXPERT_TILE_ROWS
    pstart, meta, n_tiles = _expert_tiles(counts, s * TOP_K, tm_e)
    dest = _dest_rows(eid, rank, pstart, 1024)[:, :TOP_K].reshape(-1)
    tokens = jnp.repeat(jnp.arange(s, dtype=jnp.int32), TOP_K)
    row_tok = jnp.zeros((n_tiles * tm_e,), jnp.int32).at[dest].set(tokens)
    down_cols = min(1024, d)
    y = _expert_ffn(h2, row_tok, w_gate, w_up, w_down, meta, n_tiles, tm_e, 256, down_cols)
    x2, h3 = _combine(x1, y, dest, wts, g_ple, 256, down_cols)

    return _ple(h3, p, w_ple_gate, w_ple_proj, x2, 512, 1024)


def kernel(x, p, w_in, w_conv, w_up_a, w_out_b, w_o, norm_mix, norm_ffn, w_group, b_group, w_router, b_router,
           w_gate, w_up, w_down, norm_ple, w_ple_gate, w_ple_proj, norm_final):
    b, s, d = x.shape
    depth = w_in.shape[0]
    outs = []
    for bi in range(b):
        xb = x[bi]
        for i in range(depth):
            xb = _layer(xb, p[i, bi], w_in[i], w_conv[i], w_up_a[i], w_out_b[i], w_o[i], norm_mix[i], norm_ffn[i],
                        w_group[i], b_group[i], w_router[i], b_router[i], w_gate[i], w_up[i], w_down[i],
                        norm_ple[i], w_ple_gate[i], w_ple_proj[i])
        outs.append(_rmsnorm(xb, norm_final, x.dtype, 256))
    return jnp.stack(outs)
```

```python
import functools

import jax
import jax.numpy as jnp
from jax import lax
from jax.experimental import pallas as pl
from jax.experimental.pallas import tpu as pltpu

F32 = jnp.float32
BF16 = jnp.bfloat16

HEAD_DIM = 128
ATTN_DILATIONS = (1, 4, 16)
HEADS_PER_GROUP = 4
ROPE_THETA = 10000.0
ATTN_BLOCK = 128
NEG_BIG = -1e30
N_EXPERT_GROUPS = 4
TOP_K = 2
EPS = 1e-6
ROW_BLOCK = 128
LANES = 128

VMEM_LIMIT_BYTES = 58 * 1024 * 1024


def _params(*sem):
    return pltpu.CompilerParams(dimension_semantics=sem, vmem_limit_bytes=VMEM_LIMIT_BYTES)


def _rmsnorm_kernel(x_ref, g_ref, o_ref):
    x = x_ref[...]
    ms = jnp.mean(x * x, axis=-1, keepdims=True)
    o_ref[...] = (x * lax.rsqrt(ms + EPS) * g_ref[...]).astype(o_ref.dtype)


def _rmsnorm(x, gain, out_dtype, tm):
    n, d = x.shape
    tm = min(tm, n)
    return pl.pallas_call(
        _rmsnorm_kernel,
        grid=(n // tm,),
        in_specs=[pl.BlockSpec((tm, d), lambda i: (i, 0)), pl.BlockSpec((1, d), lambda i: (0, 0))],
        out_specs=pl.BlockSpec((tm, d), lambda i: (i, 0)),
        out_shape=jax.ShapeDtypeStruct((n, d), out_dtype),
        compiler_params=_params("arbitrary"),
        name="rmsnorm",
    )(x, gain.reshape(1, d))


def _staged_weight(w_hbm, stage_ref, wb_ref, sem, tn, col0=0):
    j = pl.program_id(0)
    i = pl.program_id(1)

    def fetch(jj):
        cols = pl.ds(pl.multiple_of(col0 + jj * tn, LANES), tn)
        return pltpu.make_async_copy(w_hbm.at[:, cols], stage_ref, sem)

    @pl.when(i == 0)
    def _():
        @pl.when(j == 0)
        def _():
            fetch(0).start()

        fetch(j).wait()
        wb_ref[...] = stage_ref[...].astype(BF16)

        @pl.when(j + 1 < pl.num_programs(0))
        def _():
            fetch(j + 1).start()


def _qkv_proj_kernel(h_ref, c_ref, s_ref, w_hbm, o_ref, stage_ref, wb_ref, sem, *, tn, n_col_tiles, n_rope_heads):
    _staged_weight(w_hbm, stage_ref, wb_ref, sem, tn)
    acc = jnp.dot(h_ref[...], wb_ref[...], preferred_element_type=F32)
    heads = o_ref.shape[0]
    j = pl.program_id(0)
    for jj in range(n_col_tiles):
        @pl.when(j == jj)
        def _(jj=jj):
            for hh in range(heads):
                t = acc[:, hh * HEAD_DIM:(hh + 1) * HEAD_DIM]
                if jj * heads + hh < n_rope_heads:
                    t = t * c_ref[...] + pltpu.roll(t, HEAD_DIM // 2, 1) * s_ref[...]
                o_ref[hh] = t


def _qkv_proj(h, w, rope_c, rope_s, n_cols, n_rope_cols, tm, tn):
    m, k = h.shape
    tm = min(tm, m)
    assert n_cols % tn == 0 and tn % HEAD_DIM == 0 and n_rope_cols % HEAD_DIM == 0 and m % tm == 0
    heads = tn // HEAD_DIM
    return pl.pallas_call(
        functools.partial(_qkv_proj_kernel, tn=tn, n_col_tiles=n_cols // tn, n_rope_heads=n_rope_cols // HEAD_DIM),
        grid=(n_cols // tn, m // tm),
        in_specs=[
            pl.BlockSpec((tm, k), lambda j, i: (i, 0)),
            pl.BlockSpec((tm, HEAD_DIM), lambda j, i: (i, 0)),
            pl.BlockSpec((tm, HEAD_DIM), lambda j, i: (i, 0)),
            pl.BlockSpec(memory_space=pl.ANY),
        ],
        out_specs=pl.BlockSpec((heads, tm, HEAD_DIM), lambda j, i: (j, i, 0)),
        out_shape=jax.ShapeDtypeStruct((n_cols // HEAD_DIM, m, HEAD_DIM), F32),
        scratch_shapes=[pltpu.VMEM((k, tn), F32), pltpu.VMEM((k, tn), BF16), pltpu.SemaphoreType.DMA],
        compiler_params=_params("arbitrary", "arbitrary"),
        name="qkv_proj_rope",
    )(h, rope_c, rope_s, w)


def _mm_kernel(a_ref, w_hbm, o_ref, stage_ref, wb_ref, sem, *, tn, col0):
    _staged_weight(w_hbm, stage_ref, wb_ref, sem, tn, col0)
    o_ref[...] = jnp.dot(a_ref[...], wb_ref[...], preferred_element_type=F32).astype(o_ref.dtype)


def _mm_cols(a, w, col0, tm, tn):
    m, k = a.shape
    n = w.shape[1] - col0
    tm = min(tm, m)
    while n % tn:
        tn //= 2
    assert tn % LANES == 0 and col0 % LANES == 0 and m % tm == 0
    return pl.pallas_call(
        functools.partial(_mm_kernel, tn=tn, col0=col0),
        grid=(n // tn, m // tm),
        in_specs=[
            pl.BlockSpec((tm, k), lambda j, i: (i, 0)),
            pl.BlockSpec(memory_space=pl.ANY),
        ],
        out_specs=pl.BlockSpec((tm, tn), lambda j, i: (i, j)),
        out_shape=jax.ShapeDtypeStruct((m, n), BF16),
        scratch_shapes=[pltpu.VMEM((k, tn), F32), pltpu.VMEM((k, tn), BF16), pltpu.SemaphoreType.DMA],
        compiler_params=_params("arbitrary", "arbitrary"),
        name="in_proj",
    )(a, w)


ATTN_SUPER = ATTN_BLOCK * max(ATTN_DILATIONS)
ATTN_UNROLL = 16


def _rows(start, size, stride):
    return pl.ds(start, size) if stride == 1 else pl.ds(start, size, stride=stride)


def _dilated_attn_kernel(q0_ref, q1_ref, q2_ref, kc_ref, kp_ref, vc_ref, vp_ref, y_ref,
                         kcat_ref, vcat_ref, o_ref, lse_ref):
    sb = pl.program_id(1)
    sup = ATTN_SUPER
    kcat_ref[pl.ds(0, sup), :] = kp_ref[...]
    kcat_ref[pl.ds(sup, sup), :] = kc_ref[...]
    vcat_ref[pl.ds(0, sup), :] = vp_ref[...]
    vcat_ref[pl.ds(sup, sup), :] = vc_ref[...]
    qi = lax.broadcasted_iota(jnp.int32, (ATTN_BLOCK, 2 * ATTN_BLOCK), 0)
    kj = lax.broadcasted_iota(jnp.int32, (ATTN_BLOCK, 2 * ATTN_BLOCK), 1)
    dist = qi + ATTN_BLOCK - kj
    band = (dist >= 0) & (dist <= ATTN_BLOCK)
    in_cur = kj >= ATTN_BLOCK
    scale = HEAD_DIM ** -0.5

    for g, (q_ref, d) in enumerate(zip((q0_ref, q1_ref, q2_ref), ATTN_DILATIONS)):
        span = ATTN_BLOCK * d
        shift = d.bit_length() - 1

        def unit(u, carry, g=g, q_ref=q_ref, d=d, span=span, shift=shift):
            j = lax.shift_right_logical(u, shift)
            row0 = j * span + (u & (d - 1))
            q = q_ref[_rows(row0, ATTN_BLOCK, d), :].astype(BF16)
            k = kcat_ref[_rows(sup - span + row0, 2 * ATTN_BLOCK, d), :].astype(BF16)
            v = vcat_ref[_rows(sup - span + row0, 2 * ATTN_BLOCK, d), :].astype(BF16)
            s = lax.dot_general(q, k, (((1,), (1,)), ((), ())), preferred_element_type=F32) * scale
            has_prev = (sb > 0) | (j > 0)
            s = jnp.where(band & (in_cur | has_prev), s, NEG_BIG)
            m = jnp.max(s, axis=-1, keepdims=True)
            e = jnp.exp(s - m)
            ssum = jnp.sum(e, axis=-1, keepdims=True)
            p = (e * (1.0 / ssum)).astype(BF16)
            o_ref[g, _rows(row0, ATTN_BLOCK, d), :] = jnp.dot(p, v, preferred_element_type=F32)
            lse_ref[g, _rows(row0, ATTN_BLOCK, d), :] = jnp.broadcast_to(m + jnp.log(ssum), (ATTN_BLOCK, HEAD_DIM))
            return carry

        lax.fori_loop(0, sup // ATTN_BLOCK, unit, 0, unroll=ATTN_UNROLL)

    lse = lse_ref[...]
    m = jnp.max(lse, axis=0)
    e = jnp.exp(lse - m[None])
    w = e / jnp.sum(e, axis=0)[None]
    y_ref[...] = jnp.sum(w * o_ref[...], axis=0).astype(y_ref.dtype)


def _dilated_attention(qkv):
    nh, s, hd = qkv.shape
    ng = len(ATTN_DILATIONS)
    hpg = HEADS_PER_GROUP
    assert nh == (ng + 2) * hpg and hd == HEAD_DIM and s % ATTN_SUPER == 0
    sup = ATTN_SUPER

    def q_spec(g):
        return pl.BlockSpec((None, sup, hd), lambda h, b: (g * hpg + h, b, 0))

    def kv_spec(first_head, prev):
        if prev:
            return pl.BlockSpec((None, sup, hd), lambda h, b: (first_head + h, jnp.maximum(b - 1, 0), 0))
        return pl.BlockSpec((None, sup, hd), lambda h, b: (first_head + h, b, 0))

    k0, v0 = ng * hpg, (ng + 1) * hpg
    return pl.pallas_call(
        _dilated_attn_kernel,
        grid=(hpg, s // sup),
        in_specs=[q_spec(0), q_spec(1), q_spec(2), kv_spec(k0, False), kv_spec(k0, True),
                  kv_spec(v0, False), kv_spec(v0, True)],
        out_specs=pl.BlockSpec((sup, hd), lambda h, b: (b, h)),
        out_shape=jax.ShapeDtypeStruct((s, hpg * hd), BF16),
        scratch_shapes=[pltpu.VMEM((2 * sup, hd), F32), pltpu.VMEM((2 * sup, hd), F32),
                        pltpu.VMEM((ng, sup, hd), F32), pltpu.VMEM((ng, sup, hd), F32)],
        compiler_params=_params("arbitrary", "arbitrary"),
        name="dilated_attention",
    )(qkv, qkv, qkv, qkv, qkv, qkv, qkv)


CONV_HALO = 16


def _conv_kernel(u_ref, gc_ref, gb_ref, up_ref, gcp_ref, w_ref, o_ref):
    i = pl.program_id(0)
    cur = u_ref[...].astype(F32) * gc_ref[...].astype(F32)
    prev = up_ref[...].astype(F32) * gcp_ref[...].astype(F32)
    prev = jnp.where(i == 0, 0.0, prev)
    big = jnp.concatenate([prev, cur], axis=0)
    back1 = pltpu.roll(big, 1, 0)[CONV_HALO:]
    back2 = pltpu.roll(big, 2, 0)[CONV_HALO:]
    w = w_ref[...]
    y = w[0:1] * back2 + w[1:2] * back1 + w[2:3] * cur
    o_ref[...] = (gb_ref[...].astype(F32) * y).astype(o_ref.dtype)


def _short_conv(proj, w_conv, col_u, col_gb, col_gc, tm, tc):
    s = proj.shape[0]
    kw, cw = w_conv.shape
    assert kw == 3
    tm = min(tm, s)
    tc = min(tc, cw)
    assert col_u % tc == 0 and col_gb % tc == 0 and col_gc % tc == 0 and tm % CONV_HALO == 0
    hb = tm // CONV_HALO

    def cur(col):
        return pl.BlockSpec((tm, tc), lambda i, j: (i, col // tc + j))

    def prev(col):
        return pl.BlockSpec((CONV_HALO, tc), lambda i, j: (jnp.maximum(i * hb - 1, 0), col // tc + j))

    return pl.pallas_call(
        _conv_kernel,
        grid=(s // tm, cw // tc),
        in_specs=[cur(col_u), cur(col_gc), cur(col_gb), prev(col_u), prev(col_gc),
                  pl.BlockSpec((kw, tc), lambda i, j: (0, j))],
        out_specs=pl.BlockSpec((tm, tc), lambda i, j: (i, j)),
        out_shape=jax.ShapeDtypeStruct((s, cw), BF16),
        compiler_params=_params("arbitrary", "arbitrary"),
        name="short_conv",
    )(proj, proj, proj, proj, proj, w_conv)


def _merge_kernel(ya_ref, yb_ref, wa_ref, wb_ref, ma_ref, mb_ref, o_ref, wab_ref, wbb_ref):
    i = pl.program_id(1)

    @pl.when(i == 0)
    def _():
        wab_ref[...] = wa_ref[...].astype(BF16)
        wbb_ref[...] = wb_ref[...].astype(BF16)

    a = jnp.dot(ya_ref[...], wab_ref[...], preferred_element_type=F32)
    b = jnp.dot(yb_ref[...], wbb_ref[...], preferred_element_type=F32)
    ga = jax.nn.sigmoid(ma_ref[...].astype(F32))
    gb = jax.nn.sigmoid(mb_ref[...].astype(F32))
    o_ref[...] = (ga * a + gb * b).astype(o_ref.dtype)


def _merge(y_a, y_b, proj, w_a, w_b, col_ma, col_mb, tm, tn):
    m, ka = y_a.shape
    kb = y_b.shape[1]
    n = w_a.shape[1]
    tm = min(tm, m)
    tn = min(tn, n)
    while col_ma % tn or col_mb % tn:
        tn //= 2
    assert tn % LANES == 0
    return pl.pallas_call(
        _merge_kernel,
        grid=(n // tn, m // tm),
        in_specs=[
            pl.BlockSpec((tm, ka), lambda j, i: (i, 0)),
            pl.BlockSpec((tm, kb), lambda j, i: (i, 0)),
            pl.BlockSpec((ka, tn), lambda j, i: (0, j)),
            pl.BlockSpec((kb, tn), lambda j, i: (0, j)),
            pl.BlockSpec((tm, tn), lambda j, i: (i, col_ma // tn + j)),
            pl.BlockSpec((tm, tn), lambda j, i: (i, col_mb // tn + j)),
        ],
        out_specs=pl.BlockSpec((tm, tn), lambda j, i: (i, j)),
        out_shape=jax.ShapeDtypeStruct((m, n), BF16),
        scratch_shapes=[pltpu.VMEM((ka, tn), BF16), pltpu.VMEM((kb, tn), BF16)],
        compiler_params=_params("arbitrary", "arbitrary"),
        name="mixer_merge",
    )(y_a, y_b, w_a, w_b, proj, proj)


def _resid_mm_kernel(a_ref, x_ref, w_hbm, o_ref, stage_ref, wb_ref, sem, *, tn):
    _staged_weight(w_hbm, stage_ref, wb_ref, sem, tn)
    o_ref[...] = x_ref[...] + jnp.dot(a_ref[...], wb_ref[...], preferred_element_type=F32)


def _resid_mm(a, w, x, tm, tn):
    m, k = a.shape
    n = w.shape[1]
    tm = min(tm, m)
    tn = min(tn, n)
    return pl.pallas_call(
        functools.partial(_resid_mm_kernel, tn=tn),
        grid=(n // tn, m // tm),
        in_specs=[
            pl.BlockSpec((tm, k), lambda j, i: (i, 0)),
            pl.BlockSpec((tm, tn), lambda j, i: (i, j)),
            pl.BlockSpec(memory_space=pl.ANY),
        ],
        out_specs=pl.BlockSpec((tm, tn), lambda j, i: (i, j)),
        out_shape=jax.ShapeDtypeStruct((m, n), F32),
        scratch_shapes=[pltpu.VMEM((k, tn), F32), pltpu.VMEM((k, tn), BF16), pltpu.SemaphoreType.DMA],
        compiler_params=_params("arbitrary", "arbitrary"),
        name="out_proj_residual",
    )(a, x, w)


def _split_bf16(a):
    hi = a.astype(BF16)
    lo = (a - hi.astype(F32)).astype(BF16)
    return hi, lo


def _pack_bf16_pair(a):
    c = a.shape[1] // 2
    bits = lax.bitcast_convert_type(a.astype(BF16).astype(F32), jnp.uint32)
    return (bits[:, :c] >> 16) | (bits[:, c:] & jnp.uint32(0xFFFF0000))


def _unpack_bf16_pair(w):
    lo = lax.bitcast_convert_type(w << 16, F32)
    hi = lax.bitcast_convert_type(w & jnp.uint32(0xFFFF0000), F32)
    return jnp.concatenate([lo, hi], axis=1)


def _router_kernel(x_ref, g_ref, w_ref, b_ref, h_ref, eid_ref, wt_ref, wh_ref, wl_ref, *, n_groups, n_experts):
    @pl.when(pl.program_id(0) == 0)
    def _():
        wh_ref[...], wl_ref[...] = _split_bf16(w_ref[...])

    x = x_ref[...]
    ms = jnp.mean(x * x, axis=-1, keepdims=True)
    h = x * lax.rsqrt(ms + EPS) * g_ref[...]
    h_ref[...] = _pack_bf16_pair(h)
    hh, hl = _split_bf16(h)
    wh, wl = wh_ref[...], wl_ref[...]
    logits = (jnp.dot(hh, wh, preferred_element_type=F32) + jnp.dot(hh, wl, preferred_element_type=F32)
              + jnp.dot(hl, wh, preferred_element_type=F32)) + b_ref[...]
    tm = x.shape[0]
    per_group = n_experts // n_groups
    lane = lax.broadcasted_iota(jnp.int32, (tm, LANES), 1)
    is_g = lane < n_groups
    gl = jnp.where(is_g, logits, NEG_BIG)
    gmax = jnp.max(gl, axis=-1, keepdims=True)
    gidx = jnp.min(jnp.where(gl == gmax, lane, LANES), axis=-1, keepdims=True)
    gsum = jnp.sum(jnp.where(is_g, jnp.exp(gl - gmax), 0.0), axis=-1, keepdims=True)
    g_w = 1.0 / gsum
    lane_e = lane - n_groups
    sel = (lane_e >= 0) & (lane_e < n_experts) & (lane_e // per_group == gidx)
    el = jnp.where(sel, logits, NEG_BIG)
    emax = jnp.max(el, axis=-1, keepdims=True)
    i1 = jnp.min(jnp.where(el == emax, lane, LANES), axis=-1, keepdims=True)
    esum = jnp.sum(jnp.where(sel, jnp.exp(el - emax), 0.0), axis=-1, keepdims=True)
    el2 = jnp.where(lane == i1, NEG_BIG, el)
    emax2 = jnp.max(el2, axis=-1, keepdims=True)
    i2 = jnp.min(jnp.where(el2 == emax2, lane, LANES), axis=-1, keepdims=True)
    p1 = 1.0 / esum
    p2 = jnp.exp(emax2 - emax) / esum
    den = p1 + p2
    eid_ref[...] = jnp.where(lane == 0, i1 - n_groups, jnp.where(lane == 1, i2 - n_groups, 0))
    wt_ref[...] = jnp.where(lane == 0, g_w * (p1 / den), jnp.where(lane == 1, g_w * (p2 / den), 0.0))


def _router(x, gain, w_route, b_route, n_groups, n_experts, tm):
    t, d = x.shape
    tm = min(tm, t)
    row = pl.BlockSpec((tm, d), lambda i: (i, 0))
    narrow = pl.BlockSpec((tm, LANES), lambda i: (i, 0))
    return pl.pallas_call(
        functools.partial(_router_kernel, n_groups=n_groups, n_experts=n_experts),
        grid=(t // tm,),
        in_specs=[row, pl.BlockSpec((1, d), lambda i: (0, 0)), pl.BlockSpec((d, LANES), lambda i: (0, 0)),
                  pl.BlockSpec((1, LANES), lambda i: (0, 0))],
        out_specs=[pl.BlockSpec((tm, d // 2), lambda i: (i, 0)), narrow, narrow],
        out_shape=[jax.ShapeDtypeStruct((t, d // 2), jnp.uint32), jax.ShapeDtypeStruct((t, LANES), jnp.int32),
                   jax.ShapeDtypeStruct((t, LANES), F32)],
        scratch_shapes=[pltpu.VMEM((d, LANES), BF16), pltpu.VMEM((d, LANES), BF16)],
        compiler_params=_params("arbitrary"),
        name="ffn_norm_router",
    )(x, gain.reshape(1, d), w_route, b_route)


def _rank_kernel(eid_ref, rank_ref, cnt_ref, carry_ref):
    i = pl.program_id(0)

    @pl.when(i == 0)
    def _():
        carry_ref[...] = jnp.zeros_like(carry_ref)

    eid = eid_ref[...]
    tm = eid.shape[0]
    lane = lax.broadcasted_iota(jnp.int32, (tm, LANES), 1)
    oh0 = lane == eid[:, 0:1]
    oh1 = lane == eid[:, 1:2]
    ohs = oh0.astype(F32) + oh1.astype(F32)
    r = lax.broadcasted_iota(jnp.int32, (tm, tm), 0)
    c = lax.broadcasted_iota(jnp.int32, (tm, tm), 1)
    earlier = (c < r).astype(BF16)
    before = jnp.dot(earlier, ohs.astype(BF16), preferred_element_type=F32) + carry_ref[...]
    rank0 = jnp.sum(jnp.where(oh0, before, 0.0), axis=-1, keepdims=True)
    rank1 = jnp.sum(jnp.where(oh1, before, 0.0), axis=-1, keepdims=True)
    rank_ref[...] = jnp.where(lane == 0, rank0, jnp.where(lane == 1, rank1, 0.0)).astype(jnp.int32)
    total = carry_ref[...] + jnp.sum(ohs, axis=0, keepdims=True)
    carry_ref[...] = total
    cnt_ref[...] = jnp.broadcast_to(total, cnt_ref.shape).astype(jnp.int32)


def _rank(eid, tm):
    t = eid.shape[0]
    tm = min(tm, t)
    narrow = pl.BlockSpec((tm, LANES), lambda i: (i, 0))
    return pl.pallas_call(
        _rank_kernel,
        grid=(t // tm,),
        in_specs=[narrow],
        out_specs=[narrow, pl.BlockSpec((8, LANES), lambda i: (0, 0))],
        out_shape=[jax.ShapeDtypeStruct((t, LANES), jnp.int32), jax.ShapeDtypeStruct((8, LANES), jnp.int32)],
        scratch_shapes=[pltpu.VMEM((1, LANES), F32)],
        compiler_params=_params("arbitrary"),
        name="expert_rank",
    )(eid)


def _dest_kernel(eid_ref, rank_ref, ps_ref, o_ref):
    eid = eid_ref[...]
    lane = lax.broadcasted_iota(jnp.int32, eid.shape, 1)
    ps = ps_ref[...]
    s0 = jnp.sum(jnp.where(lane == eid[:, 0:1], ps, 0.0), axis=-1, keepdims=True)
    s1 = jnp.sum(jnp.where(lane == eid[:, 1:2], ps, 0.0), axis=-1, keepdims=True)
    start = jnp.where(lane == 0, s0, jnp.where(lane == 1, s1, 0.0))
    o_ref[...] = start.astype(jnp.int32) + rank_ref[...]


def _dest_rows(eid, rank, pstart, tm):
    t = eid.shape[0]
    tm = min(tm, t)
    ps = jnp.zeros((1, LANES), F32).at[0, :pstart.shape[0]].set(pstart.astype(F32))
    narrow = pl.BlockSpec((tm, LANES), lambda i: (i, 0))
    return pl.pallas_call(
        _dest_kernel,
        grid=(t // tm,),
        in_specs=[narrow, narrow, pl.BlockSpec((1, LANES), lambda i: (0, 0))],
        out_specs=narrow,
        out_shape=jax.ShapeDtypeStruct((t, LANES), jnp.int32),
        compiler_params=_params("arbitrary"),
        name="expert_dest_rows",
    )(eid, rank, ps)


GATHER_UNROLL = 8
GATHER_PRIORITY = 1


def _expert_ffn_kernel(te_ref, trb_ref, nsub_ref, ntot_ref, tok_ref, tok_next_ref, h_hbm, wg_ref, wu_ref, wd_ref,
                       y_ref, land_ref, hid_ref, sem, *, tm, n_up):
    t = pl.program_id(0)
    s = pl.program_id(1)
    n_tiles = pl.num_programs(0)
    nsub = nsub_ref[t]
    slot = t & 1

    def row_copy(tok, r, sl):
        return pltpu.make_async_copy(h_hbm.at[pl.ds(tok, 1)], land_ref.at[sl, pl.ds(r, 1)], sem.at[sl])

    def start_gather(toks_ref, blocks, sl):
        def issue(i, carry):
            for u in range(GATHER_UNROLL):
                r = i * GATHER_UNROLL + u
                row_copy(toks_ref[0, 0, r], r, sl).start(priority=GATHER_PRIORITY)
            return carry

        lax.fori_loop(0, blocks * (ROW_BLOCK // GATHER_UNROLL), issue, 0)

    @pl.when(nsub > 0)
    def _():
        @pl.when(s == 0)
        def _():
            @pl.when(t == 0)
            def _():
                start_gather(tok_ref, nsub, slot)

            def drain(i, carry):
                for u in range(GATHER_UNROLL):
                    row_copy(0, 0, slot).wait()
                return carry

            lax.fori_loop(0, nsub * (ROW_BLOCK // GATHER_UNROLL), drain, 0)

        @pl.when(s == 1)
        def _():
            nxt = jnp.minimum(t + 1, n_tiles - 1)
            n_next = jnp.where(t + 1 < n_tiles, nsub_ref[nxt], 0)
            start_gather(tok_next_ref, n_next, 1 - slot)

        for n in range(1, tm // ROW_BLOCK + 1):
            rows = n * ROW_BLOCK

            @pl.when((nsub == n) & (s < n_up))
            def _(rows=rows):
                x = _unpack_bf16_pair(land_ref[slot, pl.ds(0, rows), :])
                g = jnp.dot(x, wg_ref[...], preferred_element_type=F32)
                u = jnp.dot(x, wu_ref[...], preferred_element_type=F32)
                hid_ref[s, pl.ds(0, rows), :] = g * jax.nn.sigmoid(g) * u

            @pl.when((nsub == n) & (s >= n_up))
            def _(rows=rows):
                hid = jnp.concatenate([hid_ref[c, pl.ds(0, rows), :] for c in range(n_up)], axis=1)
                y = jnp.dot(hid, wd_ref[...], preferred_element_type=F32)
                y_ref[pl.ds(0, rows), :] = _pack_bf16_pair(y)
                if rows < tm:
                    y_ref[pl.ds(rows, tm - rows), :] = jnp.zeros((tm - rows, y_ref.shape[1]), y_ref.dtype)


def _expert_ffn(h, row_tok, w_gate, w_up, w_down, meta, n_tiles, tm, tf, tn):
    d = w_gate.shape[1]
    f = w_gate.shape[2]
    tf = min(tf, f)
    tn = min(tn, d)
    n_up = f // tf
    n_down = d // tn
    n_steps = n_up + n_down

    def step(t, s, ntot):
        return jnp.where(t < ntot[0], s, n_steps - 1)

    def up_idx(t, s, te, trb, ns, ntot):
        return (te[t], 0, jnp.minimum(step(t, s, ntot), n_up - 1))

    def down_col(t, s, ntot):
        return jnp.maximum(step(t, s, ntot) - n_up, 0)

    grid_spec = pltpu.PrefetchScalarGridSpec(
        num_scalar_prefetch=4,
        grid=(n_tiles, n_steps),
        in_specs=[
            pl.BlockSpec((1, 1, tm), lambda t, s, te, trb, ns, ntot: (trb[t], 0, 0), memory_space=pltpu.SMEM),
            pl.BlockSpec((1, 1, tm), lambda t, s, te, trb, ns, ntot: (trb[jnp.minimum(t + 1, n_tiles - 1)], 0, 0),
                         memory_space=pltpu.SMEM),
            pl.BlockSpec(memory_space=pl.ANY),
            pl.BlockSpec((None, d, tf), up_idx),
            pl.BlockSpec((None, d, tf), up_idx),
            pl.BlockSpec((None, f, tn), lambda t, s, te, trb, ns, ntot: (te[t], 0, down_col(t, s, ntot))),
        ],
        out_specs=pl.BlockSpec((tm, tn // 2), lambda t, s, te, trb, ns, ntot: (trb[t], down_col(t, s, ntot))),
        scratch_shapes=[
            pltpu.VMEM((2, tm, d // 2), jnp.uint32),
            pltpu.VMEM((n_up, tm, tf), F32),
            pltpu.SemaphoreType.DMA((2,)),
        ],
    )
    return pl.pallas_call(
        functools.partial(_expert_ffn_kernel, tm=tm, n_up=n_up),
        grid_spec=grid_spec,
        out_shape=jax.ShapeDtypeStruct((n_tiles * tm, d // 2), jnp.uint32),
        compiler_params=_params("arbitrary", "arbitrary"),
        name="expert_ffn",
    )(*meta, row_tok.reshape(n_tiles, 1, tm), row_tok.reshape(n_tiles, 1, tm), h, w_gate, w_up, w_down)


def _combine_kernel(dest_ref, dest_next_ref, wt_ref, x_ref, g_ref, y_hbm, x2_ref, h3_ref, buf_ref, sem, *, tb, chunk):
    i = pl.program_id(0)
    slot = i & 1

    def row_copy(src, k, r, sl):
        return pltpu.make_async_copy(y_hbm.at[pl.ds(src, 1)], buf_ref.at[sl, k, pl.ds(r, 1)], sem.at[sl])

    def start_gather(rows_ref, sl):
        def issue(r, carry):
            for k in range(TOP_K):
                row_copy(rows_ref[0, 0, TOP_K * r + k], k, r, sl).start(priority=k % 2)
            return carry

        lax.fori_loop(0, tb, issue, 0, unroll=GATHER_UNROLL // TOP_K)

    @pl.when(i == 0)
    def _():
        start_gather(dest_ref, slot)

    @pl.when(i + 1 < pl.num_programs(0))
    def _():
        start_gather(dest_next_ref, 1 - slot)

    def drain(r, carry):
        for k in range(TOP_K):
            row_copy(0, k, r, slot).wait()
        return carry

    lax.fori_loop(0, tb, drain, 0, unroll=GATHER_UNROLL // TOP_K)

    def expert_rows(k):
        words = buf_ref[slot, k]
        half = chunk // 2
        return jnp.concatenate([_unpack_bf16_pair(words[:, c * half:(c + 1) * half])
                                for c in range(words.shape[1] // half)], axis=1)

    wt = wt_ref[...]
    x2 = x_ref[...] + (wt[:, 0:1] * expert_rows(0) + wt[:, 1:2] * expert_rows(1))
    x2_ref[...] = x2
    ms = jnp.mean(x2 * x2, axis=-1, keepdims=True)
    h3_ref[...] = (x2 * lax.rsqrt(ms + EPS) * g_ref[...]).astype(h3_ref.dtype)


def _combine(x, y, dest, wts, gain, tb, chunk):
    t, d = x.shape
    tb = min(tb, t)
    nb = t // tb
    dest3 = dest.reshape(nb, 1, TOP_K * tb)
    row = pl.BlockSpec((tb, d), lambda i: (i, 0))
    return pl.pallas_call(
        functools.partial(_combine_kernel, tb=tb, chunk=chunk),
        grid=(nb,),
        in_specs=[
            pl.BlockSpec((1, 1, TOP_K * tb), lambda i: (i, 0, 0), memory_space=pltpu.SMEM),
            pl.BlockSpec((1, 1, TOP_K * tb), lambda i: (jnp.minimum(i + 1, nb - 1), 0, 0), memory_space=pltpu.SMEM),
            pl.BlockSpec((tb, LANES), lambda i: (i, 0)),
            row,
            pl.BlockSpec((1, d), lambda i: (0, 0)),
            pl.BlockSpec(memory_space=pl.ANY),
        ],
        out_specs=[row, row],
        out_shape=[jax.ShapeDtypeStruct((t, d), F32), jax.ShapeDtypeStruct((t, d), BF16)],
        scratch_shapes=[pltpu.VMEM((2, TOP_K, tb, d // 2), jnp.uint32), pltpu.SemaphoreType.DMA((2,))],
        compiler_params=_params("arbitrary"),
        name="moe_combine_norm",
    )(dest3, dest3, wts, x, gain.reshape(1, d), y)


def _ple_kernel(h_ref, p_ref, wp_ref, x_ref, wg_hbm, o_ref, stage_ref, wgb_ref, wpb_ref, sem, *, tn):
    _staged_weight(wg_hbm, stage_ref, wgb_ref, sem, tn)

    @pl.when(pl.program_id(1) == 0)
    def _():
        wpb_ref[...] = wp_ref[...].astype(BF16)

    a = jnp.dot(h_ref[...], wgb_ref[...], preferred_element_type=F32)
    b = jnp.dot(p_ref[...].astype(BF16), wpb_ref[...], preferred_element_type=F32)
    o_ref[...] = x_ref[...] + jax.nn.sigmoid(a) * b


def _ple(h, p, w_gate, w_proj, x, tm, tn):
    m, k = h.shape
    kp = p.shape[1]
    n = w_gate.shape[1]
    tm = min(tm, m)
    tn = min(tn, n)
    return pl.pallas_call(
        functools.partial(_ple_kernel, tn=tn),
        grid=(n // tn, m // tm),
        in_specs=[
            pl.BlockSpec((tm, k), lambda j, i: (i, 0)),
            pl.BlockSpec((tm, kp), lambda j, i: (i, 0)),
            pl.BlockSpec((kp, tn), lambda j, i: (0, j)),
            pl.BlockSpec((tm, tn), lambda j, i: (i, j)),
            pl.BlockSpec(memory_space=pl.ANY),
        ],
        out_specs=pl.BlockSpec((tm, tn), lambda j, i: (i, j)),
        out_shape=jax.ShapeDtypeStruct((m, n), F32),
        scratch_shapes=[pltpu.VMEM((k, tn), F32), pltpu.VMEM((k, tn), BF16), pltpu.VMEM((kp, tn), BF16),
                        pltpu.SemaphoreType.DMA],
        compiler_params=_params("arbitrary", "arbitrary"),
        name="ple_gate_residual",
    )(h, p, w_proj, x, w_gate)


EXPERT_TILE_ROWS = 640


def _rope_tables(s):
    half = HEAD_DIM // 2
    inv_freq = ROPE_THETA ** (-jnp.arange(half, dtype=F32) / half)
    ang = jnp.arange(s, dtype=jnp.int32).astype(F32)[:, None] * inv_freq[None, :]
    cos, sin = jnp.cos(ang), jnp.sin(ang)
    return jnp.concatenate([cos, cos], axis=-1), jnp.concatenate([-sin, sin], axis=-1)


def _expert_tiles(counts, n_assign, tm):
    n_experts = counts.shape[0]
    n_tiles = -(-n_assign // tm) + n_experts
    tiles_e = (counts + tm - 1) // tm
    tcum = jnp.cumsum(tiles_e)
    tstart = tcum - tiles_e
    pstart = tstart * tm
    total = tcum[-1]
    t = jnp.arange(n_tiles, dtype=jnp.int32)
    tc = jnp.minimum(t, total - 1)
    te = jnp.minimum(jnp.searchsorted(tcum, tc, side="right"), n_experts - 1).astype(jnp.int32)
    local = tc - tstart[te]
    rows_left = counts[te] - local * tm
    nsub = jnp.clip((rows_left + ROW_BLOCK - 1) // ROW_BLOCK, 0, tm // ROW_BLOCK)
    nsub = jnp.where(t < total, nsub, 0).astype(jnp.int32)
    trb = (tstart[te] + local).astype(jnp.int32)
    meta = (te, trb, nsub, jnp.reshape(total, (1,)).astype(jnp.int32))
    return pstart.astype(jnp.int32), meta, n_tiles


def _layer(x, p, w_in, w_conv, w_up_a, w_out_b, w_o, g_mix, g_ffn, w_group, b_group, w_router, b_router,
           w_gate, w_up, w_down, g_ple, w_ple_gate, w_ple_proj):
    s, d = x.shape
    cw = w_conv.shape[1]
    n_experts = w_gate.shape[0]
    ng = len(ATTN_DILATIONS)
    a_out = HEADS_PER_GROUP * HEAD_DIM
    q_cols = ng * a_out
    qkv_cols = q_cols + 2 * a_out
    col_u, col_gb, col_gc = 0, cw, 2 * cw
    col_ma = 3 * cw
    col_mb = col_ma + d

    h = _rmsnorm(x, g_mix, BF16, 256)
    rope_c, rope_s = _rope_tables(s)
    qkv = _qkv_proj(h, w_in, rope_c, rope_s, qkv_cols, q_cols + a_out, 512, qkv_cols // 2)
    proj = _mm_cols(h, w_in, qkv_cols, 1024, 1024)
    y_a = _dilated_attention(qkv)
    y_b = _short_conv(proj, w_conv, col_u, col_gb, col_gc, 512, 512)
    merged = _merge(y_a, y_b, proj, w_up_a, w_out_b, col_ma, col_mb, 1024, 1024)
    x1 = _resid_mm(merged, w_o, x, 512, 1024)

    n_route = N_EXPERT_GROUPS + n_experts
    w_route = jnp.pad(jnp.concatenate([w_group, w_router], axis=1), ((0, 0), (0, LANES - n_route)))
    b_route = jnp.pad(jnp.concatenate([b_group, b_router]), (0, LANES - n_route)).reshape(1, LANES)
    h2, eid, wts = _router(x1, g_ffn, w_route, b_route, N_EXPERT_GROUPS, n_experts, 256)
    rank, cnt = _rank(eid, 256)
    counts = cnt[0, :n_experts]
    tm_e = EXPERT_TILE_ROWS
    pstart, meta, n_tiles = _expert_tiles(counts, s * TOP_K, tm_e)
    dest = _dest_rows(eid, rank, pstart, 1024)[:, :TOP_K].reshape(-1)
    tokens = jnp.repeat(jnp.arange(s, dtype=jnp.int32), TOP_K)
    row_tok = jnp.zeros((n_tiles * tm_e,), jnp.int32).at[dest].set(tokens)
    down_cols = min(1024, d)
    y = _expert_ffn(h2, row_tok, w_gate, w_up, w_down, meta, n_tiles, tm_e, 256, down_cols)
    x2, h3 = _combine(x1, y, dest, wts, g_ple, 256, down_cols)

    return _ple(h3, p, w_ple_gate, w_ple_proj, x2, 512, 1024)


def kernel(x, p, w_in, w_conv, w_up_a, w_out_b, w_o, norm_mix, norm_ffn, w_group, b_group, w_router, b_router,
           w_gate, w_up, w_down, norm_ple, w_ple_gate, w_ple_proj, norm_final):
    b, s, d = x.shape
    depth = w_in.shape[0]
    outs = []
    for bi in range(b):
        xb = x[bi]
        for i in range(depth):
            xb = _layer(xb, p[i, bi], w_in[i], w_conv[i], w_up_a[i], w_out_b[i], w_o[i], norm_mix[i], norm_ffn[i],
                        w_group[i], b_group[i], w_router[i], b_router[i], w_gate[i], w_up[i], w_down[i],
                        norm_ple[i], w_ple_gate[i], w_ple_proj[i])
        outs.append(_rmsnorm(xb, norm_final, x.dtype, 256))
    return jnp.stack(outs)
```

```python
import functools

import jax
import jax.numpy as jnp
from jax import lax
from jax.experimental import pallas as pl
from jax.experimental.pallas import tpu as pltpu

F32 = jnp.float32
BF16 = jnp.bfloat16

HEAD_DIM = 128
ATTN_DILATIONS = (1, 4, 16)
HEADS_PER_GROUP = 4
ROPE_THETA = 10000.0
ATTN_BLOCK = 128
NEG_BIG = -1e30
N_EXPERT_GROUPS = 4
TOP_K = 2
EPS = 1e-6
ROW_BLOCK = 128
LANES = 128

VMEM_LIMIT_BYTES = 58 * 1024 * 1024


def _params(*sem):
    return pltpu.CompilerParams(dimension_semantics=sem, vmem_limit_bytes=VMEM_LIMIT_BYTES)


def _rmsnorm_kernel(x_ref, g_ref, o_ref):
    x = x_ref[...]
    ms = jnp.mean(x * x, axis=-1, keepdims=True)
    o_ref[...] = (x * lax.rsqrt(ms + EPS) * g_ref[...]).astype(o_ref.dtype)


def _rmsnorm(x, gain, out_dtype, tm):
    n, d = x.shape
    tm = min(tm, n)
    return pl.pallas_call(
        _rmsnorm_kernel,
        grid=(n // tm,),
        in_specs=[pl.BlockSpec((tm, d), lambda i: (i, 0)), pl.BlockSpec((1, d), lambda i: (0, 0))],
        out_specs=pl.BlockSpec((tm, d), lambda i: (i, 0)),
        out_shape=jax.ShapeDtypeStruct((n, d), out_dtype),
        compiler_params=_params("arbitrary"),
        name="rmsnorm",
    )(x, gain.reshape(1, d))


def _staged_weight(w_hbm, stage_ref, wb_ref, sem, tn, col0=0):
    j = pl.program_id(0)
    i = pl.program_id(1)

    def fetch(jj):
        cols = pl.ds(pl.multiple_of(col0 + jj * tn, LANES), tn)
        return pltpu.make_async_copy(w_hbm.at[:, cols], stage_ref, sem)

    @pl.when(i == 0)
    def _():
        @pl.when(j == 0)
        def _():
            fetch(0).start()

        fetch(j).wait()
        wb_ref[...] = stage_ref[...].astype(BF16)

        @pl.when(j + 1 < pl.num_programs(0))
        def _():
            fetch(j + 1).start()


def _qkv_proj_kernel(h_ref, c_ref, s_ref, w_hbm, o_ref, stage_ref, wb_ref, sem, *, tn, n_col_tiles, n_rope_heads):
    _staged_weight(w_hbm, stage_ref, wb_ref, sem, tn)
    acc = jnp.dot(h_ref[...], wb_ref[...], preferred_element_type=F32)
    heads = o_ref.shape[0]
    j = pl.program_id(0)
    for jj in range(n_col_tiles):
        @pl.when(j == jj)
        def _(jj=jj):
            for hh in range(heads):
                t = acc[:, hh * HEAD_DIM:(hh + 1) * HEAD_DIM]
                if jj * heads + hh < n_rope_heads:
                    t = t * c_ref[...] + pltpu.roll(t, HEAD_DIM // 2, 1) * s_ref[...]
                o_ref[hh] = t


def _qkv_proj(h, w, rope_c, rope_s, n_cols, n_rope_cols, tm, tn):
    m, k = h.shape
    tm = min(tm, m)
    assert n_cols % tn == 0 and tn % HEAD_DIM == 0 and n_rope_cols % HEAD_DIM == 0 and m % tm == 0
    heads = tn // HEAD_DIM
    return pl.pallas_call(
        functools.partial(_qkv_proj_kernel, tn=tn, n_col_tiles=n_cols // tn, n_rope_heads=n_rope_cols // HEAD_DIM),
        grid=(n_cols // tn, m // tm),
        in_specs=[
            pl.BlockSpec((tm, k), lambda j, i: (i, 0)),
            pl.BlockSpec((tm, HEAD_DIM), lambda j, i: (i, 0)),
            pl.BlockSpec((tm, HEAD_DIM), lambda j, i: (i, 0)),
            pl.BlockSpec(memory_space=pl.ANY),
        ],
        out_specs=pl.BlockSpec((heads, tm, HEAD_DIM), lambda j, i: (j, i, 0)),
        out_shape=jax.ShapeDtypeStruct((n_cols // HEAD_DIM, m, HEAD_DIM), F32),
        scratch_shapes=[pltpu.VMEM((k, tn), F32), pltpu.VMEM((k, tn), BF16), pltpu.SemaphoreType.DMA],
        compiler_params=_params("arbitrary", "arbitrary"),
        name="qkv_proj_rope",
    )(h, rope_c, rope_s, w)


def _mm_kernel(a_ref, w_hbm, o_ref, stage_ref, wb_ref, sem, *, tn, col0):
    _staged_weight(w_hbm, stage_ref, wb_ref, sem, tn, col0)
    o_ref[...] = jnp.dot(a_ref[...], wb_ref[...], preferred_element_type=F32).astype(o_ref.dtype)


def _mm_cols(a, w, col0, tm, tn):
    m, k = a.shape
    n = w.shape[1] - col0
    tm = min(tm, m)
    while n % tn:
        tn //= 2
    assert tn % LANES == 0 and col0 % LANES == 0 and m % tm == 0
    return pl.pallas_call(
        functools.partial(_mm_kernel, tn=tn, col0=col0),
        grid=(n // tn, m // tm),
        in_specs=[
            pl.BlockSpec((tm, k), lambda j, i: (i, 0)),
            pl.BlockSpec(memory_space=pl.ANY),
        ],
        out_specs=pl.BlockSpec((tm, tn), lambda j, i: (i, j)),
        out_shape=jax.ShapeDtypeStruct((m, n), BF16),
        scratch_shapes=[pltpu.VMEM((k, tn), F32), pltpu.VMEM((k, tn), BF16), pltpu.SemaphoreType.DMA],
        compiler_params=_params("arbitrary", "arbitrary"),
        name="in_proj",
    )(a, w)


ATTN_SUPER = ATTN_BLOCK * max(ATTN_DILATIONS)
ATTN_UNROLL = 16
LOG2_E = 1.4426950408889634
LN_2 = 0.6931471805599453


def _rows(start, size, stride):
    return pl.ds(start, size) if stride == 1 else pl.ds(start, size, stride=stride)


def _dilated_attn_kernel(q0_ref, q1_ref, q2_ref, kc_ref, kp_ref, vc_ref, vp_ref, y_ref,
                         kcat_ref, vcat_ref, o_ref, lse_ref):
    sb = pl.program_id(1)
    sup = ATTN_SUPER
    kcat_ref[pl.ds(0, sup), :] = kp_ref[...]
    kcat_ref[pl.ds(sup, sup), :] = kc_ref[...]
    vcat_ref[pl.ds(0, sup), :] = vp_ref[...]
    vcat_ref[pl.ds(sup, sup), :] = vc_ref[...]
    qi = lax.broadcasted_iota(jnp.int32, (ATTN_BLOCK, 2 * ATTN_BLOCK), 0)
    kj = lax.broadcasted_iota(jnp.int32, (ATTN_BLOCK, 2 * ATTN_BLOCK), 1)
    dist = qi + ATTN_BLOCK - kj
    band = (dist >= 0) & (dist <= ATTN_BLOCK)
    in_cur = kj >= ATTN_BLOCK
    q_scale = HEAD_DIM ** -0.5 * LOG2_E

    for g, (q_ref, d) in enumerate(zip((q0_ref, q1_ref, q2_ref), ATTN_DILATIONS)):
        span = ATTN_BLOCK * d
        shift = d.bit_length() - 1

        def unit(u, carry, g=g, q_ref=q_ref, d=d, span=span, shift=shift):
            j = lax.shift_right_logical(u, shift)
            row0 = j * span + (u & (d - 1))
            q = (q_ref[_rows(row0, ATTN_BLOCK, d), :] * q_scale).astype(BF16)
            k = kcat_ref[_rows(sup - span + row0, 2 * ATTN_BLOCK, d), :].astype(BF16)
            v = vcat_ref[_rows(sup - span + row0, 2 * ATTN_BLOCK, d), :].astype(BF16)
            s = lax.dot_general(q, k, (((1,), (1,)), ((), ())), preferred_element_type=F32)
            has_prev = (sb > 0) | (j > 0)
            s = jnp.where(band & (in_cur | has_prev), s, NEG_BIG)
            m = jnp.max(s, axis=-1, keepdims=True)
            e = jnp.exp2(s - m)
            ssum = jnp.sum(e, axis=-1, keepdims=True)
            p = (e * (1.0 / ssum)).astype(BF16)
            o_ref[g, _rows(row0, ATTN_BLOCK, d), :] = jnp.dot(p, v, preferred_element_type=F32)
            lse = m * LN_2 + jnp.log(ssum)
            lse_ref[g, _rows(row0, ATTN_BLOCK, d), :] = jnp.broadcast_to(lse, (ATTN_BLOCK, HEAD_DIM))
            return carry

        lax.fori_loop(0, sup // ATTN_BLOCK, unit, 0, unroll=ATTN_UNROLL)

    lse = lse_ref[...]
    m = jnp.max(lse, axis=0)
    e = jnp.exp(lse - m[None])
    w = e / jnp.sum(e, axis=0)[None]
    y_ref[...] = jnp.sum(w * o_ref[...], axis=0).astype(y_ref.dtype)


def _dilated_attention(qkv):
    nh, s, hd = qkv.shape
    ng = len(ATTN_DILATIONS)
    hpg = HEADS_PER_GROUP
    assert nh == (ng + 2) * hpg and hd == HEAD_DIM and s % ATTN_SUPER == 0
    sup = ATTN_SUPER

    def q_spec(g):
        return pl.BlockSpec((None, sup, hd), lambda h, b: (g * hpg + h, b, 0))

    def kv_spec(first_head, prev):
        if prev:
            return pl.BlockSpec((None, sup, hd), lambda h, b: (first_head + h, jnp.maximum(b - 1, 0), 0))
        return pl.BlockSpec((None, sup, hd), lambda h, b: (first_head + h, b, 0))

    k0, v0 = ng * hpg, (ng + 1) * hpg
    return pl.pallas_call(
        _dilated_attn_kernel,
        grid=(hpg, s // sup),
        in_specs=[q_spec(0), q_spec(1), q_spec(2), kv_spec(k0, False), kv_spec(k0, True),
                  kv_spec(v0, False), kv_spec(v0, True)],
        out_specs=pl.BlockSpec((sup, hd), lambda h, b: (b, h)),
        out_shape=jax.ShapeDtypeStruct((s, hpg * hd), BF16),
        scratch_shapes=[pltpu.VMEM((2 * sup, hd), F32), pltpu.VMEM((2 * sup, hd), F32),
                        pltpu.VMEM((ng, sup, hd), F32), pltpu.VMEM((ng, sup, hd), F32)],
        compiler_params=_params("arbitrary", "arbitrary"),
        name="dilated_attention",
    )(qkv, qkv, qkv, qkv, qkv, qkv, qkv)


CONV_HALO = 16


def _conv_kernel(u_ref, gc_ref, gb_ref, up_ref, gcp_ref, w_ref, o_ref):
    i = pl.program_id(0)
    cur = u_ref[...].astype(F32) * gc_ref[...].astype(F32)
    prev = up_ref[...].astype(F32) * gcp_ref[...].astype(F32)
    prev = jnp.where(i == 0, 0.0, prev)
    big = jnp.concatenate([prev, cur], axis=0)
    back1 = pltpu.roll(big, 1, 0)[CONV_HALO:]
    back2 = pltpu.roll(big, 2, 0)[CONV_HALO:]
    w = w_ref[...]
    y = w[0:1] * back2 + w[1:2] * back1 + w[2:3] * cur
    o_ref[...] = (gb_ref[...].astype(F32) * y).astype(o_ref.dtype)


def _short_conv(proj, w_conv, col_u, col_gb, col_gc, tm, tc):
    s = proj.shape[0]
    kw, cw = w_conv.shape
    assert kw == 3
    tm = min(tm, s)
    tc = min(tc, cw)
    assert col_u % tc == 0 and col_gb % tc == 0 and col_gc % tc == 0 and tm % CONV_HALO == 0
    hb = tm // CONV_HALO

    def cur(col):
        return pl.BlockSpec((tm, tc), lambda i, j: (i, col // tc + j))

    def prev(col):
        return pl.BlockSpec((CONV_HALO, tc), lambda i, j: (jnp.maximum(i * hb - 1, 0), col // tc + j))

    return pl.pallas_call(
        _conv_kernel,
        grid=(s // tm, cw // tc),
        in_specs=[cur(col_u), cur(col_gc), cur(col_gb), prev(col_u), prev(col_gc),
                  pl.BlockSpec((kw, tc), lambda i, j: (0, j))],
        out_specs=pl.BlockSpec((tm, tc), lambda i, j: (i, j)),
        out_shape=jax.ShapeDtypeStruct((s, cw), BF16),
        compiler_params=_params("arbitrary", "arbitrary"),
        name="short_conv",
    )(proj, proj, proj, proj, proj, w_conv)


def _merge_kernel(ya_ref, yb_ref, wa_ref, wb_ref, ma_ref, mb_ref, o_ref, wab_ref, wbb_ref):
    i = pl.program_id(1)

    @pl.when(i == 0)
    def _():
        wab_ref[...] = wa_ref[...].astype(BF16)
        wbb_ref[...] = wb_ref[...].astype(BF16)

    a = jnp.dot(ya_ref[...], wab_ref[...], preferred_element_type=F32)
    b = jnp.dot(yb_ref[...], wbb_ref[...], preferred_element_type=F32)
    ga = jax.nn.sigmoid(ma_ref[...].astype(F32))
    gb = jax.nn.sigmoid(mb_ref[...].astype(F32))
    o_ref[...] = (ga * a + gb * b).astype(o_ref.dtype)


def _merge(y_a, y_b, proj, w_a, w_b, col_ma, col_mb, tm, tn):
    m, ka = y_a.shape
    kb = y_b.shape[1]
    n = w_a.shape[1]
    tm = min(tm, m)
    tn = min(tn, n)
    while col_ma % tn or col_mb % tn:
        tn //= 2
    assert tn % LANES == 0
    return pl.pallas_call(
        _merge_kernel,
        grid=(n // tn, m // tm),
        in_specs=[
            pl.BlockSpec((tm, ka), lambda j, i: (i, 0)),
            pl.BlockSpec((tm, kb), lambda j, i: (i, 0)),
            pl.BlockSpec((ka, tn), lambda j, i: (0, j)),
            pl.BlockSpec((kb, tn), lambda j, i: (0, j)),
            pl.BlockSpec((tm, tn), lambda j, i: (i, col_ma // tn + j)),
            pl.BlockSpec((tm, tn), lambda j, i: (i, col_mb // tn + j)),
        ],
        out_specs=pl.BlockSpec((tm, tn), lambda j, i: (i, j)),
        out_shape=jax.ShapeDtypeStruct((m, n), BF16),
        scratch_shapes=[pltpu.VMEM((ka, tn), BF16), pltpu.VMEM((kb, tn), BF16)],
        compiler_params=_params("arbitrary", "arbitrary"),
        name="mixer_merge",
    )(y_a, y_b, w_a, w_b, proj, proj)


def _resid_mm_kernel(a_ref, x_ref, w_hbm, o_ref, stage_ref, wb_ref, sem, *, tn):
    _staged_weight(w_hbm, stage_ref, wb_ref, sem, tn)
    o_ref[...] = x_ref[...] + jnp.dot(a_ref[...], wb_ref[...], preferred_element_type=F32)


def _resid_mm(a, w, x, tm, tn):
    m, k = a.shape
    n = w.shape[1]
    tm = min(tm, m)
    tn = min(tn, n)
    return pl.pallas_call(
        functools.partial(_resid_mm_kernel, tn=tn),
        grid=(n // tn, m // tm),
        in_specs=[
            pl.BlockSpec((tm, k), lambda j, i: (i, 0)),
            pl.BlockSpec((tm, tn), lambda j, i: (i, j)),
            pl.BlockSpec(memory_space=pl.ANY),
        ],
        out_specs=pl.BlockSpec((tm, tn), lambda j, i: (i, j)),
        out_shape=jax.ShapeDtypeStruct((m, n), F32),
        scratch_shapes=[pltpu.VMEM((k, tn), F32), pltpu.VMEM((k, tn), BF16), pltpu.SemaphoreType.DMA],
        compiler_params=_params("arbitrary", "arbitrary"),
        name="out_proj_residual",
    )(a, x, w)


def _split_bf16(a):
    hi = a.astype(BF16)
    lo = (a - hi.astype(F32)).astype(BF16)
    return hi, lo


def _pack_bf16_pair(a):
    c = a.shape[1] // 2
    bits = lax.bitcast_convert_type(a.astype(BF16).astype(F32), jnp.uint32)
    return (bits[:, :c] >> 16) | (bits[:, c:] & jnp.uint32(0xFFFF0000))


def _unpack_bf16_pair(w):
    lo = lax.bitcast_convert_type(w << 16, F32)
    hi = lax.bitcast_convert_type(w & jnp.uint32(0xFFFF0000), F32)
    return jnp.concatenate([lo, hi], axis=1)


def _router_kernel(x_ref, g_ref, w_ref, b_ref, h_ref, eid_ref, wt_ref, whl_ref, *, n_groups, n_experts):
    @pl.when(pl.program_id(0) == 0)
    def _():
        whl_ref[:, :LANES], whl_ref[:, LANES:] = _split_bf16(w_ref[...])

    x = x_ref[...]
    ms = jnp.mean(x * x, axis=-1, keepdims=True)
    h = x * lax.rsqrt(ms + EPS) * g_ref[...]
    h_ref[...] = _pack_bf16_pair(h)
    hh, hl = _split_bf16(h)
    both = jnp.dot(hh, whl_ref[...], preferred_element_type=F32)
    logits = (both[:, :LANES] + both[:, LANES:]
              + jnp.dot(hl, whl_ref[:, :LANES], preferred_element_type=F32)) + b_ref[...]
    tm = x.shape[0]
    per_group = n_experts // n_groups
    lane = lax.broadcasted_iota(jnp.int32, (tm, LANES), 1)
    is_g = lane < n_groups
    gl = jnp.where(is_g, logits, NEG_BIG)
    gmax = jnp.max(gl, axis=-1, keepdims=True)
    gidx = jnp.min(jnp.where(gl == gmax, lane, LANES), axis=-1, keepdims=True)
    gsum = jnp.sum(jnp.where(is_g, jnp.exp(gl - gmax), 0.0), axis=-1, keepdims=True)
    g_w = 1.0 / gsum
    lane_e = lane - n_groups
    sel = (lane_e >= 0) & (lane_e < n_experts) & (lane_e // per_group == gidx)
    el = jnp.where(sel, logits, NEG_BIG)
    emax = jnp.max(el, axis=-1, keepdims=True)
    i1 = jnp.min(jnp.where(el == emax, lane, LANES), axis=-1, keepdims=True)
    esum = jnp.sum(jnp.where(sel, jnp.exp(el - emax), 0.0), axis=-1, keepdims=True)
    el2 = jnp.where(lane == i1, NEG_BIG, el)
    emax2 = jnp.max(el2, axis=-1, keepdims=True)
    i2 = jnp.min(jnp.where(el2 == emax2, lane, LANES), axis=-1, keepdims=True)
    p1 = 1.0 / esum
    p2 = jnp.exp(emax2 - emax) / esum
    den = p1 + p2
    eid_ref[...] = jnp.where(lane == 0, i1 - n_groups, jnp.where(lane == 1, i2 - n_groups, 0))
    wt_ref[...] = jnp.where(lane == 0, g_w * (p1 / den), jnp.where(lane == 1, g_w * (p2 / den), 0.0))


def _router(x, gain, w_route, b_route, n_groups, n_experts, tm):
    t, d = x.shape
    tm = min(tm, t)
    row = pl.BlockSpec((tm, d), lambda i: (i, 0))
    narrow = pl.BlockSpec((tm, LANES), lambda i: (i, 0))
    return pl.pallas_call(
        functools.partial(_router_kernel, n_groups=n_groups, n_experts=n_experts),
        grid=(t // tm,),
        in_specs=[row, pl.BlockSpec((1, d), lambda i: (0, 0)), pl.BlockSpec((d, LANES), lambda i: (0, 0)),
                  pl.BlockSpec((1, LANES), lambda i: (0, 0))],
        out_specs=[pl.BlockSpec((tm, d // 2), lambda i: (i, 0)), narrow, narrow],
        out_shape=[jax.ShapeDtypeStruct((t, d // 2), jnp.uint32), jax.ShapeDtypeStruct((t, LANES), jnp.int32),
                   jax.ShapeDtypeStruct((t, LANES), F32)],
        scratch_shapes=[pltpu.VMEM((d, 2 * LANES), BF16)],
        compiler_params=_params("arbitrary"),
        name="ffn_norm_router",
    )(x, gain.reshape(1, d), w_route, b_route)


def _rank_kernel(eid_ref, rank_ref, cnt_ref, carry_ref):
    i = pl.program_id(0)

    @pl.when(i == 0)
    def _():
        carry_ref[...] = jnp.zeros_like(carry_ref)

    eid = eid_ref[...]
    tm = eid.shape[0]
    lane = lax.broadcasted_iota(jnp.int32, (tm, LANES), 1)
    oh0 = lane == eid[:, 0:1]
    oh1 = lane == eid[:, 1:2]
    ohs = oh0.astype(F32) + oh1.astype(F32)
    r = lax.broadcasted_iota(jnp.int32, (tm, tm), 0)
    c = lax.broadcasted_iota(jnp.int32, (tm, tm), 1)
    earlier = (c < r).astype(BF16)
    before = jnp.dot(earlier, ohs.astype(BF16), preferred_element_type=F32) + carry_ref[...]
    rank0 = jnp.sum(jnp.where(oh0, before, 0.0), axis=-1, keepdims=True)
    rank1 = jnp.sum(jnp.where(oh1, before, 0.0), axis=-1, keepdims=True)
    rank_ref[...] = jnp.where(lane == 0, rank0, jnp.where(lane == 1, rank1, 0.0)).astype(jnp.int32)
    total = carry_ref[...] + jnp.sum(ohs, axis=0, keepdims=True)
    carry_ref[...] = total
    cnt_ref[...] = jnp.broadcast_to(total, cnt_ref.shape).astype(jnp.int32)


def _rank(eid, tm):
    t = eid.shape[0]
    tm = min(tm, t)
    narrow = pl.BlockSpec((tm, LANES), lambda i: (i, 0))
    return pl.pallas_call(
        _rank_kernel,
        grid=(t // tm,),
        in_specs=[narrow],
        out_specs=[narrow, pl.BlockSpec((8, LANES), lambda i: (0, 0))],
        out_shape=[jax.ShapeDtypeStruct((t, LANES), jnp.int32), jax.ShapeDtypeStruct((8, LANES), jnp.int32)],
        scratch_shapes=[pltpu.VMEM((1, LANES), F32)],
        compiler_params=_params("arbitrary"),
        name="expert_rank",
    )(eid)


def _dest_kernel(eid_ref, rank_ref, ps_ref, o_ref):
    eid = eid_ref[...]
    lane = lax.broadcasted_iota(jnp.int32, eid.shape, 1)
    ps = ps_ref[...]
    s0 = jnp.sum(jnp.where(lane == eid[:, 0:1], ps, 0.0), axis=-1, keepdims=True)
    s1 = jnp.sum(jnp.where(lane == eid[:, 1:2], ps, 0.0), axis=-1, keepdims=True)
    start = jnp.where(lane == 0, s0, jnp.where(lane == 1, s1, 0.0))
    o_ref[...] = start.astype(jnp.int32) + rank_ref[...]


def _dest_rows(eid, rank, pstart, tm):
    t = eid.shape[0]
    tm = min(tm, t)
    ps = jnp.zeros((1, LANES), F32).at[0, :pstart.shape[0]].set(pstart.astype(F32))
    narrow = pl.BlockSpec((tm, LANES), lambda i: (i, 0))
    return pl.pallas_call(
        _dest_kernel,
        grid=(t // tm,),
        in_specs=[narrow, narrow, pl.BlockSpec((1, LANES), lambda i: (0, 0))],
        out_specs=narrow,
        out_shape=jax.ShapeDtypeStruct((t, LANES), jnp.int32),
        compiler_params=_params("arbitrary"),
        name="expert_dest_rows",
    )(eid, rank, ps)


GATHER_UNROLL = 8
GATHER_PRIORITY = 1


def _expert_ffn_kernel(te_ref, trb_ref, nsub_ref, ntot_ref, tok_ref, tok_next_ref, h_hbm, wg_ref, wu_ref, wd_ref,
                       y_ref, land_ref, hid_ref, sem, *, tm, n_up):
    t = pl.program_id(0)
    s = pl.program_id(1)
    n_tiles = pl.num_programs(0)
    nsub = nsub_ref[t]
    slot = t & 1

    def row_copy(tok, r, sl):
        return pltpu.make_async_copy(h_hbm.at[pl.ds(tok, 1)], land_ref.at[sl, pl.ds(r, 1)], sem.at[sl])

    def start_gather(toks_ref, blocks, sl):
        def issue(i, carry):
            for u in range(GATHER_UNROLL):
                r = i * GATHER_UNROLL + u
                row_copy(toks_ref[0, 0, r], r, sl).start(priority=GATHER_PRIORITY)
            return carry

        lax.fori_loop(0, blocks * (ROW_BLOCK // GATHER_UNROLL), issue, 0)

    @pl.when(nsub > 0)
    def _():
        @pl.when(s == 0)
        def _():
            @pl.when(t == 0)
            def _():
                start_gather(tok_ref, nsub, slot)

            def drain(i, carry):
                for u in range(GATHER_UNROLL):
                    row_copy(0, 0, slot).wait()
                return carry

            lax.fori_loop(0, nsub * (ROW_BLOCK // GATHER_UNROLL), drain, 0)

        @pl.when(s == 1)
        def _():
            nxt = jnp.minimum(t + 1, n_tiles - 1)
            n_next = jnp.where(t + 1 < n_tiles, nsub_ref[nxt], 0)
            start_gather(tok_next_ref, n_next, 1 - slot)

        for n in range(1, tm // ROW_BLOCK + 1):
            rows = n * ROW_BLOCK

            @pl.when((nsub == n) & (s < n_up))
            def _(rows=rows):
                x = _unpack_bf16_pair(land_ref[slot, pl.ds(0, rows), :])
                g = jnp.dot(x, wg_ref[...], preferred_element_type=F32)
                u = jnp.dot(x, wu_ref[...], preferred_element_type=F32)
                hid_ref[s, pl.ds(0, rows), :] = g * jax.nn.sigmoid(g) * u

            @pl.when((nsub == n) & (s >= n_up))
            def _(rows=rows):
                hid = jnp.concatenate([hid_ref[c, pl.ds(0, rows), :] for c in range(n_up)], axis=1)
                y = jnp.dot(hid, wd_ref[...], preferred_element_type=F32)
                y_ref[pl.ds(0, rows), :] = _pack_bf16_pair(y)
                if rows < tm:
                    y_ref[pl.ds(rows, tm - rows), :] = jnp.zeros((tm - rows, y_ref.shape[1]), y_ref.dtype)


def _expert_ffn(h, row_tok, w_gate, w_up, w_down, meta, n_tiles, tm, tf, tn):
    d = w_gate.shape[1]
    f = w_gate.shape[2]
    tf = min(tf, f)
    tn = min(tn, d)
    n_up = f // tf
    n_down = d // tn
    n_steps = n_up + n_down

    def step(t, s, ntot):
        return jnp.where(t < ntot[0], s, n_steps - 1)

    def up_idx(t, s, te, trb, ns, ntot):
        return (te[t], 0, jnp.minimum(step(t, s, ntot), n_up - 1))

    def down_col(t, s, ntot):
        return jnp.maximum(step(t, s, ntot) - n_up, 0)

    grid_spec = pltpu.PrefetchScalarGridSpec(
        num_scalar_prefetch=4,
        grid=(n_tiles, n_steps),
        in_specs=[
            pl.BlockSpec((1, 1, tm), lambda t, s, te, trb, ns, ntot: (trb[t], 0, 0), memory_space=pltpu.SMEM),
            pl.BlockSpec((1, 1, tm), lambda t, s, te, trb, ns, ntot: (trb[jnp.minimum(t + 1, n_tiles - 1)], 0, 0),
                         memory_space=pltpu.SMEM),
            pl.BlockSpec(memory_space=pl.ANY),
            pl.BlockSpec((None, d, tf), up_idx),
            pl.BlockSpec((None, d, tf), up_idx),
            pl.BlockSpec((None, f, tn), lambda t, s, te, trb, ns, ntot: (te[t], 0, down_col(t, s, ntot))),
        ],
        out_specs=pl.BlockSpec((tm, tn // 2), lambda t, s, te, trb, ns, ntot: (trb[t], down_col(t, s, ntot))),
        scratch_shapes=[
            pltpu.VMEM((2, tm, d // 2), jnp.uint32),
            pltpu.VMEM((n_up, tm, tf), F32),
            pltpu.SemaphoreType.DMA((2,)),
        ],
    )
    return pl.pallas_call(
        functools.partial(_expert_ffn_kernel, tm=tm, n_up=n_up),
        grid_spec=grid_spec,
        out_shape=jax.ShapeDtypeStruct((n_tiles * tm, d // 2), jnp.uint32),
        compiler_params=_params("arbitrary", "arbitrary"),
        name="expert_ffn",
    )(*meta, row_tok.reshape(n_tiles, 1, tm), row_tok.reshape(n_tiles, 1, tm), h, w_gate, w_up, w_down)


def _combine_kernel(dest_ref, dest_next_ref, wt_ref, x_ref, g_ref, y_hbm, x2_ref, h3_ref, buf_ref, sem, *, tb, chunk):
    i = pl.program_id(0)
    slot = i & 1

    def row_copy(src, k, r, sl):
        return pltpu.make_async_copy(y_hbm.at[pl.ds(src, 1)], buf_ref.at[sl, k, pl.ds(r, 1)], sem.at[sl])

    def start_gather(rows_ref, sl):
        def issue(r, carry):
            for k in range(TOP_K):
                row_copy(rows_ref[0, 0, TOP_K * r + k], k, r, sl).start(priority=k % 2)
            return carry

        lax.fori_loop(0, tb, issue, 0, unroll=GATHER_UNROLL // TOP_K)

    @pl.when(i == 0)
    def _():
        start_gather(dest_ref, slot)

    @pl.when(i + 1 < pl.num_programs(0))
    def _():
        start_gather(dest_next_ref, 1 - slot)

    def drain(r, carry):
        for k in range(TOP_K):
            row_copy(0, k, r, slot).wait()
        return carry

    lax.fori_loop(0, tb, drain, 0, unroll=GATHER_UNROLL // TOP_K)

    def expert_rows(k):
        words = buf_ref[slot, k]
        half = chunk // 2
        return jnp.concatenate([_unpack_bf16_pair(words[:, c * half:(c + 1) * half])
                                for c in range(words.shape[1] // half)], axis=1)

    wt = wt_ref[...]
    x2 = x_ref[...] + (wt[:, 0:1] * expert_rows(0) + wt[:, 1:2] * expert_rows(1))
    x2_ref[...] = x2
    ms = jnp.mean(x2 * x2, axis=-1, keepdims=True)
    h3_ref[...] = (x2 * lax.rsqrt(ms + EPS) * g_ref[...]).astype(h3_ref.dtype)


def _combine(x, y, dest, wts, gain, tb, chunk):
    t, d = x.shape
    tb = min(tb, t)
    nb = t // tb
    dest3 = dest.reshape(nb, 1, TOP_K * tb)
    row = pl.BlockSpec((tb, d), lambda i: (i, 0))
    return pl.pallas_call(
        functools.partial(_combine_kernel, tb=tb, chunk=chunk),
        grid=(nb,),
        in_specs=[
            pl.BlockSpec((1, 1, TOP_K * tb), lambda i: (i, 0, 0), memory_space=pltpu.SMEM),
            pl.BlockSpec((1, 1, TOP_K * tb), lambda i: (jnp.minimum(i + 1, nb - 1), 0, 0), memory_space=pltpu.SMEM),
            pl.BlockSpec((tb, LANES), lambda i: (i, 0)),
            row,
            pl.BlockSpec((1, d), lambda i: (0, 0)),
            pl.BlockSpec(memory_space=pl.ANY),
        ],
        out_specs=[row, row],
        out_shape=[jax.ShapeDtypeStruct((t, d), F32), jax.ShapeDtypeStruct((t, d), BF16)],
        scratch_shapes=[pltpu.VMEM((2, TOP_K, tb, d // 2), jnp.uint32), pltpu.SemaphoreType.DMA((2,))],
        compiler_params=_params("arbitrary"),
        name="moe_combine_norm",
    )(dest3, dest3, wts, x, gain.reshape(1, d), y)


def _ple_kernel(h_ref, p_ref, wp_ref, x_ref, wg_hbm, o_ref, stage_ref, wgb_ref, wpb_ref, sem, *, tn):
    _staged_weight(wg_hbm, stage_ref, wgb_ref, sem, tn)

    @pl.when(pl.program_id(1) == 0)
    def _():
        wpb_ref[...] = wp_ref[...].astype(BF16)

    a = jnp.dot(h_ref[...], wgb_ref[...], preferred_element_type=F32)
    b = jnp.dot(p_ref[...].astype(BF16), wpb_ref[...], preferred_element_type=F32)
    o_ref[...] = x_ref[...] + jax.nn.sigmoid(a) * b


def _ple(h, p, w_gate, w_proj, x, tm, tn):
    m, k = h.shape
    kp = p.shape[1]
    n = w_gate.shape[1]
    tm = min(tm, m)
    tn = min(tn, n)
    return pl.pallas_call(
        functools.partial(_ple_kernel, tn=tn),
        grid=(n // tn, m // tm),
        in_specs=[
            pl.BlockSpec((tm, k), lambda j, i: (i, 0)),
            pl.BlockSpec((tm, kp), lambda j, i: (i, 0)),
            pl.BlockSpec((kp, tn), lambda j, i: (0, j)),
            pl.BlockSpec((tm, tn), lambda j, i: (i, j)),
            pl.BlockSpec(memory_space=pl.ANY),
        ],
        out_specs=pl.BlockSpec((tm, tn), lambda j, i: (i, j)),
        out_shape=jax.ShapeDtypeStruct((m, n), F32),
        scratch_shapes=[pltpu.VMEM((k, tn), F32), pltpu.VMEM((k, tn), BF16), pltpu.VMEM((kp, tn), BF16),
                        pltpu.SemaphoreType.DMA],
        compiler_params=_params("arbitrary", "arbitrary"),
        name="ple_gate_residual",
    )(h, p, w_proj, x, w_gate)


EXPERT_TILE_ROWS = 640


def _rope_tables(s):
    half = HEAD_DIM // 2
    inv_freq = ROPE_THETA ** (-jnp.arange(half, dtype=F32) / half)
    ang = jnp.arange(s, dtype=jnp.int32).astype(F32)[:, None] * inv_freq[None, :]
    cos, sin = jnp.cos(ang), jnp.sin(ang)
    return jnp.concatenate([cos, cos], axis=-1), jnp.concatenate([-sin, sin], axis=-1)


def _expert_tiles(counts, n_assign, tm):
    n_experts = counts.shape[0]
    n_tiles = -(-n_assign // tm) + n_experts
    tiles_e = (counts + tm - 1) // tm
    tcum = jnp.cumsum(tiles_e)
    tstart = tcum - tiles_e
    pstart = tstart * tm
    total = tcum[-1]
    t = jnp.arange(n_tiles, dtype=jnp.int32)
    tc = jnp.minimum(t, total - 1)
    te = jnp.minimum(jnp.searchsorted(tcum, tc, side="right"), n_experts - 1).astype(jnp.int32)
    local = tc - tstart[te]
    rows_left = counts[te] - local * tm
    nsub = jnp.clip((rows_left + ROW_BLOCK - 1) // ROW_BLOCK, 0, tm // ROW_BLOCK)
    nsub = jnp.where(t < total, nsub, 0).astype(jnp.int32)
    trb = (tstart[te] + local).astype(jnp.int32)
    meta = (te, trb, nsub, jnp.reshape(total, (1,)).astype(jnp.int32))
    return pstart.astype(jnp.int32), meta, n_tiles


def _layer(x, p, w_in, w_conv, w_up_a, w_out_b, w_o, g_mix, g_ffn, w_group, b_group, w_router, b_router,
           w_gate, w_up, w_down, g_ple, w_ple_gate, w_ple_proj):
    s, d = x.shape
    cw = w_conv.shape[1]
    n_experts = w_gate.shape[0]
    ng = len(ATTN_DILATIONS)
    a_out = HEADS_PER_GROUP * HEAD_DIM
    q_cols = ng * a_out
    qkv_cols = q_cols + 2 * a_out
    col_u, col_gb, col_gc = 0, cw, 2 * cw
    col_ma = 3 * cw
    col_mb = col_ma + d

    h = _rmsnorm(x, g_mix, BF16, 512)
    rope_c, rope_s = _rope_tables(s)
    qkv = _qkv_proj(h, w_in, rope_c, rope_s, qkv_cols, q_cols + a_out, 512, qkv_cols // 2)
    proj = _mm_cols(h, w_in, qkv_cols, 1024, 1024)
    y_a = _dilated_attention(qkv)
    y_b = _short_conv(proj, w_conv, col_u, col_gb, col_gc, 512, 2048)
    merged = _merge(y_a, y_b, proj, w_up_a, w_out_b, col_ma, col_mb, 1024, 1024)
    x1 = _resid_mm(merged, w_o, x, 512, 1024)

    n_route = N_EXPERT_GROUPS + n_experts
    w_route = jnp.pad(jnp.concatenate([w_group, w_router], axis=1), ((0, 0), (0, LANES - n_route)))
    b_route = jnp.pad(jnp.concatenate([b_group, b_router]), (0, LANES - n_route)).reshape(1, LANES)
    h2, eid, wts = _router(x1, g_ffn, w_route, b_route, N_EXPERT_GROUPS, n_experts, 256)
    rank, cnt = _rank(eid, 1024)
    counts = cnt[0, :n_experts]
    tm_e = EXPERT_TILE_ROWS
    pstart, meta, n_tiles = _expert_tiles(counts, s * TOP_K, tm_e)
    dest = _dest_rows(eid, rank, pstart, 1024)[:, :TOP_K].reshape(-1)
    tokens = jnp.repeat(jnp.arange(s, dtype=jnp.int32), TOP_K)
    row_tok = jnp.zeros((n_tiles * tm_e,), jnp.int32).at[dest].set(tokens)
    down_cols = min(1024, d)
    y = _expert_ffn(h2, row_tok, w_gate, w_up, w_down, meta, n_tiles, tm_e, 256, down_cols)
    x2, h3 = _combine(x1, y, dest, wts, g_ple, 256, down_cols)

    return _ple(h3, p, w_ple_gate, w_ple_proj, x2, 512, 1024)


def kernel(x, p, w_in, w_conv, w_up_a, w_out_b, w_o, norm_mix, norm_ffn, w_group, b_group, w_router, b_router,
           w_gate, w_up, w_down, norm_ple, w_ple_gate, w_ple_proj, norm_final):
    b, s, d = x.shape
    depth = w_in.shape[0]
    outs = []
    for bi in range(b):
        xb = x[bi]
        for i in range(depth):
            xb = _layer(xb, p[i, bi], w_in[i], w_conv[i], w_up_a[i], w_out_b[i], w_o[i], norm_mix[i], norm_ffn[i],
                        w_group[i], b_group[i], w_router[i], b_router[i], w_gate[i], w_up[i], w_down[i],
                        norm_ple[i], w_ple_gate[i], w_ple_proj[i])
        outs.append(_rmsnorm(xb, norm_final, x.dtype, 512))
    return jnp.stack(outs)
```

```python
import functools

import jax
import jax.numpy as jnp
from jax import lax
from jax.experimental import pallas as pl
from jax.experimental.pallas import tpu as pltpu

F32 = jnp.float32
BF16 = jnp.bfloat16

HEAD_DIM = 128
ATTN_DILATIONS = (1, 4, 16)
HEADS_PER_GROUP = 4
ROPE_THETA = 10000.0
ATTN_BLOCK = 128
NEG_BIG = -1e30
N_EXPERT_GROUPS = 4
TOP_K = 2
EPS = 1e-6
ROW_BLOCK = 128
LANES = 128

VMEM_LIMIT_BYTES = 58 * 1024 * 1024


def _params(*sem):
    return pltpu.CompilerParams(dimension_semantics=sem, vmem_limit_bytes=VMEM_LIMIT_BYTES)


def _rmsnorm_kernel(x_ref, g_ref, o_ref):
    x = x_ref[...]
    ms = jnp.mean(x * x, axis=-1, keepdims=True)
    o_ref[...] = (x * lax.rsqrt(ms + EPS) * g_ref[...]).astype(o_ref.dtype)


def _rmsnorm(x, gain, out_dtype, tm):
    n, d = x.shape
    tm = min(tm, n)
    return pl.pallas_call(
        _rmsnorm_kernel,
        grid=(n // tm,),
        in_specs=[pl.BlockSpec((tm, d), lambda i: (i, 0)), pl.BlockSpec((1, d), lambda i: (0, 0))],
        out_specs=pl.BlockSpec((tm, d), lambda i: (i, 0)),
        out_shape=jax.ShapeDtypeStruct((n, d), out_dtype),
        compiler_params=_params("arbitrary"),
        name="rmsnorm",
    )(x, gain.reshape(1, d))


def _staged_weight(w_hbm, stage_ref, wb_ref, sem, tn, col0=0):
    j = pl.program_id(0)
    i = pl.program_id(1)

    def fetch(jj):
        cols = pl.ds(pl.multiple_of(col0 + jj * tn, LANES), tn)
        return pltpu.make_async_copy(w_hbm.at[:, cols], stage_ref, sem)

    @pl.when(i == 0)
    def _():
        @pl.when(j == 0)
        def _():
            fetch(0).start()

        fetch(j).wait()
        wb_ref[...] = stage_ref[...].astype(BF16)

        @pl.when(j + 1 < pl.num_programs(0))
        def _():
            fetch(j + 1).start()


def _qkv_proj_kernel(h_ref, c_ref, s_ref, w_hbm, o_ref, stage_ref, wb_ref, sem, *, tn, n_col_tiles, n_rope_heads):
    _staged_weight(w_hbm, stage_ref, wb_ref, sem, tn)
    acc = jnp.dot(h_ref[...], wb_ref[...], preferred_element_type=F32)
    heads = o_ref.shape[0]
    j = pl.program_id(0)
    for jj in range(n_col_tiles):
        @pl.when(j == jj)
        def _(jj=jj):
            for hh in range(heads):
                t = acc[:, hh * HEAD_DIM:(hh + 1) * HEAD_DIM]
                if jj * heads + hh < n_rope_heads:
                    t = t * c_ref[...] + pltpu.roll(t, HEAD_DIM // 2, 1) * s_ref[...]
                o_ref[hh] = t


def _qkv_proj(h, w, rope_c, rope_s, n_cols, n_rope_cols, tm, tn):
    m, k = h.shape
    tm = min(tm, m)
    assert n_cols % tn == 0 and tn % HEAD_DIM == 0 and n_rope_cols % HEAD_DIM == 0 and m % tm == 0
    heads = tn // HEAD_DIM
    return pl.pallas_call(
        functools.partial(_qkv_proj_kernel, tn=tn, n_col_tiles=n_cols // tn, n_rope_heads=n_rope_cols // HEAD_DIM),
        grid=(n_cols // tn, m // tm),
        in_specs=[
            pl.BlockSpec((tm, k), lambda j, i: (i, 0)),
            pl.BlockSpec((tm, HEAD_DIM), lambda j, i: (i, 0)),
            pl.BlockSpec((tm, HEAD_DIM), lambda j, i: (i, 0)),
            pl.BlockSpec(memory_space=pl.ANY),
        ],
        out_specs=pl.BlockSpec((heads, tm, HEAD_DIM), lambda j, i: (j, i, 0)),
        out_shape=jax.ShapeDtypeStruct((n_cols // HEAD_DIM, m, HEAD_DIM), F32),
        scratch_shapes=[pltpu.VMEM((k, tn), F32), pltpu.VMEM((k, tn), BF16), pltpu.SemaphoreType.DMA],
        compiler_params=_params("arbitrary", "arbitrary"),
        name="qkv_proj_rope",
    )(h, rope_c, rope_s, w)


def _mm_kernel(a_ref, w_hbm, o_ref, stage_ref, wb_ref, sem, *, tn, col0):
    _staged_weight(w_hbm, stage_ref, wb_ref, sem, tn, col0)
    o_ref[...] = jnp.dot(a_ref[...], wb_ref[...], preferred_element_type=F32).astype(o_ref.dtype)


def _mm_cols(a, w, col0, tm, tn):
    m, k = a.shape
    n = w.shape[1] - col0
    tm = min(tm, m)
    while n % tn:
        tn //= 2
    assert tn % LANES == 0 and col0 % LANES == 0 and m % tm == 0
    return pl.pallas_call(
        functools.partial(_mm_kernel, tn=tn, col0=col0),
        grid=(n // tn, m // tm),
        in_specs=[
            pl.BlockSpec((tm, k), lambda j, i: (i, 0)),
            pl.BlockSpec(memory_space=pl.ANY),
        ],
        out_specs=pl.BlockSpec((tm, tn), lambda j, i: (i, j)),
        out_shape=jax.ShapeDtypeStruct((m, n), BF16),
        scratch_shapes=[pltpu.VMEM((k, tn), F32), pltpu.VMEM((k, tn), BF16), pltpu.SemaphoreType.DMA],
        compiler_params=_params("arbitrary", "arbitrary"),
        name="in_proj",
    )(a, w)


ATTN_SUPER = ATTN_BLOCK * max(ATTN_DILATIONS)
ATTN_UNROLL = 16
LOG2_E = 1.4426950408889634
LN_2 = 0.6931471805599453


def _rows(start, size, stride):
    return pl.ds(start, size) if stride == 1 else pl.ds(start, size, stride=stride)


def _dilated_attn_kernel(q0_ref, q1_ref, q2_ref, kc_ref, kp_ref, vc_ref, vp_ref, y_ref,
                         kcat_ref, vcat_ref, o_ref, lse_ref):
    sb = pl.program_id(1)
    sup = ATTN_SUPER
    kcat_ref[pl.ds(0, sup), :] = kp_ref[...]
    kcat_ref[pl.ds(sup, sup), :] = kc_ref[...]
    vcat_ref[pl.ds(0, sup), :] = vp_ref[...]
    vcat_ref[pl.ds(sup, sup), :] = vc_ref[...]
    qi = lax.broadcasted_iota(jnp.int32, (ATTN_BLOCK, 2 * ATTN_BLOCK), 0)
    kj = lax.broadcasted_iota(jnp.int32, (ATTN_BLOCK, 2 * ATTN_BLOCK), 1)
    dist = qi + ATTN_BLOCK - kj
    band = (dist >= 0) & (dist <= ATTN_BLOCK)
    in_cur = kj >= ATTN_BLOCK
    q_scale = HEAD_DIM ** -0.5 * LOG2_E

    for g, (q_ref, d) in enumerate(zip((q0_ref, q1_ref, q2_ref), ATTN_DILATIONS)):
        span = ATTN_BLOCK * d
        shift = d.bit_length() - 1

        def unit(u, carry, g=g, q_ref=q_ref, d=d, span=span, shift=shift):
            j = lax.shift_right_logical(u, shift)
            row0 = j * span + (u & (d - 1))
            q = (q_ref[_rows(row0, ATTN_BLOCK, d), :] * q_scale).astype(BF16)
            k = kcat_ref[_rows(sup - span + row0, 2 * ATTN_BLOCK, d), :].astype(BF16)
            v = vcat_ref[_rows(sup - span + row0, 2 * ATTN_BLOCK, d), :].astype(BF16)
            s = lax.dot_general(q, k, (((1,), (1,)), ((), ())), preferred_element_type=F32)
            has_prev = (sb > 0) | (j > 0)
            s = jnp.where(band & (in_cur | has_prev), s, NEG_BIG)
            m = jnp.max(s, axis=-1, keepdims=True)
            e = jnp.exp2(s - m)
            ssum = jnp.sum(e, axis=-1, keepdims=True)
            p = (e * (1.0 / ssum)).astype(BF16)
            o_ref[g, _rows(row0, ATTN_BLOCK, d), :] = jnp.dot(p, v, preferred_element_type=F32)
            lse = m * LN_2 + jnp.log(ssum)
            lse_ref[g, _rows(row0, ATTN_BLOCK, d), :] = jnp.broadcast_to(lse, (ATTN_BLOCK, HEAD_DIM))
            return carry

        lax.fori_loop(0, sup // ATTN_BLOCK, unit, 0, unroll=ATTN_UNROLL)

    lse = lse_ref[...]
    m = jnp.max(lse, axis=0)
    e = jnp.exp(lse - m[None])
    w = e / jnp.sum(e, axis=0)[None]
    y_ref[...] = jnp.sum(w * o_ref[...], axis=0).astype(y_ref.dtype)


def _dilated_attention(qkv):
    nh, s, hd = qkv.shape
    ng = len(ATTN_DILATIONS)
    hpg = HEADS_PER_GROUP
    assert nh == (ng + 2) * hpg and hd == HEAD_DIM and s % ATTN_SUPER == 0
    sup = ATTN_SUPER

    def q_spec(g):
        return pl.BlockSpec((None, sup, hd), lambda h, b: (g * hpg + h, b, 0))

    def kv_spec(first_head, prev):
        if prev:
            return pl.BlockSpec((None, sup, hd), lambda h, b: (first_head + h, jnp.maximum(b - 1, 0), 0))
        return pl.BlockSpec((None, sup, hd), lambda h, b: (first_head + h, b, 0))

    k0, v0 = ng * hpg, (ng + 1) * hpg
    return pl.pallas_call(
        _dilated_attn_kernel,
        grid=(hpg, s // sup),
        in_specs=[q_spec(0), q_spec(1), q_spec(2), kv_spec(k0, False), kv_spec(k0, True),
                  kv_spec(v0, False), kv_spec(v0, True)],
        out_specs=pl.BlockSpec((sup, hd), lambda h, b: (b, h)),
        out_shape=jax.ShapeDtypeStruct((s, hpg * hd), BF16),
        scratch_shapes=[pltpu.VMEM((2 * sup, hd), F32), pltpu.VMEM((2 * sup, hd), F32),
                        pltpu.VMEM((ng, sup, hd), F32), pltpu.VMEM((ng, sup, hd), F32)],
        compiler_params=_params("arbitrary", "arbitrary"),
        name="dilated_attention",
    )(qkv, qkv, qkv, qkv, qkv, qkv, qkv)


CONV_HALO = 16


def _conv_proj_kernel(h_ref, wc_ref, w_hbm, o_ref, stage_ref, wb_ref, carry_ref, sem, *, tc, cols):
    j = pl.program_id(0)
    i = pl.program_id(1)

    def fetch(jj):
        return [pltpu.make_async_copy(w_hbm.at[:, pl.ds(pl.multiple_of(c0 + jj * tc, LANES), tc)],
                                      stage_ref.at[g], sem.at[g]) for g, c0 in enumerate(cols)]

    @pl.when(i == 0)
    def _():
        @pl.when(j == 0)
        def _():
            for c in fetch(0):
                c.start()

        for c in fetch(j):
            c.wait()
        for g in range(len(cols)):
            wb_ref[:, g * tc:(g + 1) * tc] = stage_ref[g].astype(BF16)
        carry_ref[...] = jnp.zeros_like(carry_ref)

        @pl.when(j + 1 < pl.num_programs(0))
        def _():
            for c in fetch(j + 1):
                c.start()

    acc = jnp.dot(h_ref[...], wb_ref[...], preferred_element_type=F32)
    u, gate_b, gate_c = acc[:, :tc], acc[:, tc:2 * tc], acc[:, 2 * tc:]
    cur = gate_c * u
    big = jnp.concatenate([carry_ref[...], cur], axis=0)
    back1 = pltpu.roll(big, 1, 0)[CONV_HALO:]
    back2 = pltpu.roll(big, 2, 0)[CONV_HALO:]
    w = wc_ref[...]
    y = w[0:1] * back2 + w[1:2] * back1 + w[2:3] * cur
    o_ref[...] = (gate_b * y).astype(o_ref.dtype)
    carry_ref[...] = cur[cur.shape[0] - CONV_HALO:, :]


def _conv_proj(h, w, w_conv, col_u, col_gb, col_gc, tm, tc):
    s, k = h.shape
    kw, cw = w_conv.shape
    assert kw == 3
    tm = min(tm, s)
    tc = min(tc, cw)
    assert cw % tc == 0 and tc % LANES == 0 and s % tm == 0 and tm % CONV_HALO == 0
    cols = (col_u, col_gb, col_gc)
    return pl.pallas_call(
        functools.partial(_conv_proj_kernel, tc=tc, cols=cols),
        grid=(cw // tc, s // tm),
        in_specs=[
            pl.BlockSpec((tm, k), lambda j, i: (i, 0)),
            pl.BlockSpec((kw, tc), lambda j, i: (0, j)),
            pl.BlockSpec(memory_space=pl.ANY),
        ],
        out_specs=pl.BlockSpec((tm, tc), lambda j, i: (i, j)),
        out_shape=jax.ShapeDtypeStruct((s, cw), BF16),
        scratch_shapes=[pltpu.VMEM((len(cols), k, tc), F32), pltpu.VMEM((k, len(cols) * tc), BF16),
                        pltpu.VMEM((CONV_HALO, tc), F32), pltpu.SemaphoreType.DMA((len(cols),))],
        compiler_params=_params("arbitrary", "arbitrary"),
        name="conv_proj",
    )(h, w_conv, w)


def _merge_kernel(ya_ref, yb_ref, wa_ref, wb_ref, ma_ref, mb_ref, o_ref, wab_ref, wbb_ref):
    i = pl.program_id(1)

    @pl.when(i == 0)
    def _():
        wab_ref[...] = wa_ref[...].astype(BF16)
        wbb_ref[...] = wb_ref[...].astype(BF16)

    a = jnp.dot(ya_ref[...], wab_ref[...], preferred_element_type=F32)
    b = jnp.dot(yb_ref[...], wbb_ref[...], preferred_element_type=F32)
    ga = jax.nn.sigmoid(ma_ref[...].astype(F32))
    gb = jax.nn.sigmoid(mb_ref[...].astype(F32))
    o_ref[...] = (ga * a + gb * b).astype(o_ref.dtype)


def _merge(y_a, y_b, proj, w_a, w_b, col_ma, col_mb, tm, tn):
    m, ka = y_a.shape
    kb = y_b.shape[1]
    n = w_a.shape[1]
    tm = min(tm, m)
    tn = min(tn, n)
    while col_ma % tn or col_mb % tn:
        tn //= 2
    assert tn % LANES == 0
    return pl.pallas_call(
        _merge_kernel,
        grid=(n // tn, m // tm),
        in_specs=[
            pl.BlockSpec((tm, ka), lambda j, i: (i, 0)),
            pl.BlockSpec((tm, kb), lambda j, i: (i, 0)),
            pl.BlockSpec((ka, tn), lambda j, i: (0, j)),
            pl.BlockSpec((kb, tn), lambda j, i: (0, j)),
            pl.BlockSpec((tm, tn), lambda j, i: (i, col_ma // tn + j)),
            pl.BlockSpec((tm, tn), lambda j, i: (i, col_mb // tn + j)),
        ],
        out_specs=pl.BlockSpec((tm, tn), lambda j, i: (i, j)),
        out_shape=jax.ShapeDtypeStruct((m, n), BF16),
        scratch_shapes=[pltpu.VMEM((ka, tn), BF16), pltpu.VMEM((kb, tn), BF16)],
        compiler_params=_params("arbitrary", "arbitrary"),
        name="mixer_merge",
    )(y_a, y_b, w_a, w_b, proj, proj)


def _resid_mm_kernel(a_ref, x_ref, w_hbm, o_ref, stage_ref, wb_ref, sem, *, tn):
    _staged_weight(w_hbm, stage_ref, wb_ref, sem, tn)
    o_ref[...] = x_ref[...] + jnp.dot(a_ref[...], wb_ref[...], preferred_element_type=F32)


def _resid_mm(a, w, x, tm, tn):
    m, k = a.shape
    n = w.shape[1]
    tm = min(tm, m)
    tn = min(tn, n)
    return pl.pallas_call(
        functools.partial(_resid_mm_kernel, tn=tn),
        grid=(n // tn, m // tm),
        in_specs=[
            pl.BlockSpec((tm, k), lambda j, i: (i, 0)),
            pl.BlockSpec((tm, tn), lambda j, i: (i, j)),
            pl.BlockSpec(memory_space=pl.ANY),
        ],
        out_specs=pl.BlockSpec((tm, tn), lambda j, i: (i, j)),
        out_shape=jax.ShapeDtypeStruct((m, n), F32),
        scratch_shapes=[pltpu.VMEM((k, tn), F32), pltpu.VMEM((k, tn), BF16), pltpu.SemaphoreType.DMA],
        compiler_params=_params("arbitrary", "arbitrary"),
        name="out_proj_residual",
    )(a, x, w)


def _split_bf16(a):
    hi = a.astype(BF16)
    lo = (a - hi.astype(F32)).astype(BF16)
    return hi, lo


def _pack_bf16_pair(a):
    c = a.shape[1] // 2
    bits = lax.bitcast_convert_type(a.astype(BF16).astype(F32), jnp.uint32)
    return (bits[:, :c] >> 16) | (bits[:, c:] & jnp.uint32(0xFFFF0000))


def _unpack_bf16_pair(w):
    lo = lax.bitcast_convert_type(w << 16, F32)
    hi = lax.bitcast_convert_type(w & jnp.uint32(0xFFFF0000), F32)
    return jnp.concatenate([lo, hi], axis=1)


def _router_kernel(x_ref, g_ref, w_ref, b_ref, h_ref, eid_ref, wt_ref, whl_ref, *, n_groups, n_experts):
    @pl.when(pl.program_id(0) == 0)
    def _():
        whl_ref[:, :LANES], whl_ref[:, LANES:] = _split_bf16(w_ref[...])

    x = x_ref[...]
    ms = jnp.mean(x * x, axis=-1, keepdims=True)
    h = x * lax.rsqrt(ms + EPS) * g_ref[...]
    h_ref[...] = _pack_bf16_pair(h)
    hh, hl = _split_bf16(h)
    both = jnp.dot(hh, whl_ref[...], preferred_element_type=F32)
    logits = (both[:, :LANES] + both[:, LANES:]
              + jnp.dot(hl, whl_ref[:, :LANES], preferred_element_type=F32)) + b_ref[...]
    tm = x.shape[0]
    per_group = n_experts // n_groups
    lane = lax.broadcasted_iota(jnp.int32, (tm, LANES), 1)
    is_g = lane < n_groups
    gl = jnp.where(is_g, logits, NEG_BIG)
    gmax = jnp.max(gl, axis=-1, keepdims=True)
    gidx = jnp.min(jnp.where(gl == gmax, lane, LANES), axis=-1, keepdims=True)
    gsum = jnp.sum(jnp.where(is_g, jnp.exp(gl - gmax), 0.0), axis=-1, keepdims=True)
    g_w = 1.0 / gsum
    lane_e = lane - n_groups
    sel = (lane_e >= 0) & (lane_e < n_experts) & (lane_e // per_group == gidx)
    el = jnp.where(sel, logits, NEG_BIG)
    emax = jnp.max(el, axis=-1, keepdims=True)
    i1 = jnp.min(jnp.where(el == emax, lane, LANES), axis=-1, keepdims=True)
    esum = jnp.sum(jnp.where(sel, jnp.exp(el - emax), 0.0), axis=-1, keepdims=True)
    el2 = jnp.where(lane == i1, NEG_BIG, el)
    emax2 = jnp.max(el2, axis=-1, keepdims=True)
    i2 = jnp.min(jnp.where(el2 == emax2, lane, LANES), axis=-1, keepdims=True)
    p1 = 1.0 / esum
    p2 = jnp.exp(emax2 - emax) / esum
    den = p1 + p2
    eid_ref[...] = jnp.where(lane == 0, i1 - n_groups, jnp.where(lane == 1, i2 - n_groups, 0))
    wt_ref[...] = jnp.where(lane == 0, g_w * (p1 / den), jnp.where(lane == 1, g_w * (p2 / den), 0.0))


def _router(x, gain, w_route, b_route, n_groups, n_experts, tm):
    t, d = x.shape
    tm = min(tm, t)
    row = pl.BlockSpec((tm, d), lambda i: (i, 0))
    narrow = pl.BlockSpec((tm, LANES), lambda i: (i, 0))
    return pl.pallas_call(
        functools.partial(_router_kernel, n_groups=n_groups, n_experts=n_experts),
        grid=(t // tm,),
        in_specs=[row, pl.BlockSpec((1, d), lambda i: (0, 0)), pl.BlockSpec((d, LANES), lambda i: (0, 0)),
                  pl.BlockSpec((1, LANES), lambda i: (0, 0))],
        out_specs=[pl.BlockSpec((tm, d // 2), lambda i: (i, 0)), narrow, narrow],
        out_shape=[jax.ShapeDtypeStruct((t, d // 2), jnp.uint32), jax.ShapeDtypeStruct((t, LANES), jnp.int32),
                   jax.ShapeDtypeStruct((t, LANES), F32)],
        scratch_shapes=[pltpu.VMEM((d, 2 * LANES), BF16)],
        compiler_params=_params("arbitrary"),
        name="ffn_norm_router",
    )(x, gain.reshape(1, d), w_route, b_route)


def _rank_kernel(eid_ref, rank_ref, cnt_ref, carry_ref):
    i = pl.program_id(0)

    @pl.when(i == 0)
    def _():
        carry_ref[...] = jnp.zeros_like(carry_ref)

    eid = eid_ref[...]
    tm = eid.shape[0]
    lane = lax.broadcasted_iota(jnp.int32, (tm, LANES), 1)
    oh0 = lane == eid[:, 0:1]
    oh1 = lane == eid[:, 1:2]
    ohs = oh0.astype(F32) + oh1.astype(F32)
    r = lax.broadcasted_iota(jnp.int32, (tm, tm), 0)
    c = lax.broadcasted_iota(jnp.int32, (tm, tm), 1)
    earlier = (c < r).astype(BF16)
    before = jnp.dot(earlier, ohs.astype(BF16), preferred_element_type=F32) + carry_ref[...]
    rank0 = jnp.sum(jnp.where(oh0, before, 0.0), axis=-1, keepdims=True)
    rank1 = jnp.sum(jnp.where(oh1, before, 0.0), axis=-1, keepdims=True)
    rank_ref[...] = jnp.where(lane == 0, rank0, jnp.where(lane == 1, rank1, 0.0)).astype(jnp.int32)
    total = carry_ref[...] + jnp.sum(ohs, axis=0, keepdims=True)
    carry_ref[...] = total
    cnt_ref[...] = jnp.broadcast_to(total, cnt_ref.shape).astype(jnp.int32)


def _rank(eid, tm):
    t = eid.shape[0]
    tm = min(tm, t)
    narrow = pl.BlockSpec((tm, LANES), lambda i: (i, 0))
    return pl.pallas_call(
        _rank_kernel,
        grid=(t // tm,),
        in_specs=[narrow],
        out_specs=[narrow, pl.BlockSpec((8, LANES), lambda i: (0, 0))],
        out_shape=[jax.ShapeDtypeStruct((t, LANES), jnp.int32), jax.ShapeDtypeStruct((8, LANES), jnp.int32)],
        scratch_shapes=[pltpu.VMEM((1, LANES), F32)],
        compiler_params=_params("arbitrary"),
        name="expert_rank",
    )(eid)


def _dest_kernel(eid_ref, rank_ref, ps_ref, o_ref):
    eid = eid_ref[...]
    lane = lax.broadcasted_iota(jnp.int32, eid.shape, 1)
    ps = ps_ref[...]
    s0 = jnp.sum(jnp.where(lane == eid[:, 0:1], ps, 0.0), axis=-1, keepdims=True)
    s1 = jnp.sum(jnp.where(lane == eid[:, 1:2], ps, 0.0), axis=-1, keepdims=True)
    start = jnp.where(lane == 0, s0, jnp.where(lane == 1, s1, 0.0))
    o_ref[...] = start.astype(jnp.int32) + rank_ref[...]


def _dest_rows(eid, rank, pstart, tm):
    t = eid.shape[0]
    tm = min(tm, t)
    ps = jnp.zeros((1, LANES), F32).at[0, :pstart.shape[0]].set(pstart.astype(F32))
    narrow = pl.BlockSpec((tm, LANES), lambda i: (i, 0))
    return pl.pallas_call(
        _dest_kernel,
        grid=(t // tm,),
        in_specs=[narrow, narrow, pl.BlockSpec((1, LANES), lambda i: (0, 0))],
        out_specs=narrow,
        out_shape=jax.ShapeDtypeStruct((t, LANES), jnp.int32),
        compiler_params=_params("arbitrary"),
        name="expert_dest_rows",
    )(eid, rank, ps)


GATHER_UNROLL = 8
GATHER_PRIORITY = 1


def _expert_ffn_kernel(te_ref, trb_ref, nsub_ref, ntot_ref, tok_ref, tok_next_ref, h_hbm, wg_ref, wu_ref, wd_ref,
                       y_ref, land_ref, hid_ref, sem, *, tm, n_up):
    t = pl.program_id(0)
    s = pl.program_id(1)
    n_tiles = pl.num_programs(0)
    nsub = nsub_ref[t]
    slot = t & 1

    def row_copy(tok, r, sl):
        return pltpu.make_async_copy(h_hbm.at[pl.ds(tok, 1)], land_ref.at[sl, pl.ds(r, 1)], sem.at[sl])

    def start_gather(toks_ref, blocks, sl):
        def issue(i, carry):
            for u in range(GATHER_UNROLL):
                r = i * GATHER_UNROLL + u
                row_copy(toks_ref[0, 0, r], r, sl).start(priority=GATHER_PRIORITY)
            return carry

        lax.fori_loop(0, blocks * (ROW_BLOCK // GATHER_UNROLL), issue, 0)

    @pl.when(nsub > 0)
    def _():
        @pl.when(s == 0)
        def _():
            @pl.when(t == 0)
            def _():
                start_gather(tok_ref, nsub, slot)

            def drain(i, carry):
                for u in range(GATHER_UNROLL):
                    row_copy(0, 0, slot).wait()
                return carry

            lax.fori_loop(0, nsub * (ROW_BLOCK // GATHER_UNROLL), drain, 0)

        @pl.when(s == 1)
        def _():
            nxt = jnp.minimum(t + 1, n_tiles - 1)
            n_next = jnp.where(t + 1 < n_tiles, nsub_ref[nxt], 0)
            start_gather(tok_next_ref, n_next, 1 - slot)

        for n in range(1, tm // ROW_BLOCK + 1):
            rows = n * ROW_BLOCK

            @pl.when((nsub == n) & (s < n_up))
            def _(rows=rows):
                x = _unpack_bf16_pair(land_ref[slot, pl.ds(0, rows), :])
                g = jnp.dot(x, wg_ref[...], preferred_element_type=F32)
                u = jnp.dot(x, wu_ref[...], preferred_element_type=F32)
                hid_ref[s, pl.ds(0, rows), :] = g * jax.nn.sigmoid(g) * u

            @pl.when((nsub == n) & (s >= n_up))
            def _(rows=rows):
                hid = jnp.concatenate([hid_ref[c, pl.ds(0, rows), :] for c in range(n_up)], axis=1)
                y = jnp.dot(hid, wd_ref[...], preferred_element_type=F32)
                y_ref[pl.ds(0, rows), :] = _pack_bf16_pair(y)
                if rows < tm:
                    y_ref[pl.ds(rows, tm - rows), :] = jnp.zeros((tm - rows, y_ref.shape[1]), y_ref.dtype)


def _expert_ffn(h, row_tok, w_gate, w_up, w_down, meta, n_tiles, tm, tf, tn):
    d = w_gate.shape[1]
    f = w_gate.shape[2]
    tf = min(tf, f)
    tn = min(tn, d)
    n_up = f // tf
    n_down = d // tn
    n_steps = n_up + n_down

    def step(t, s, ntot):
        return jnp.where(t < ntot[0], s, n_steps - 1)

    def up_idx(t, s, te, trb, ns, ntot):
        return (te[t], 0, jnp.minimum(step(t, s, ntot), n_up - 1))

    def down_col(t, s, ntot):
        return jnp.maximum(step(t, s, ntot) - n_up, 0)

    grid_spec = pltpu.PrefetchScalarGridSpec(
        num_scalar_prefetch=4,
        grid=(n_tiles, n_steps),
        in_specs=[
            pl.BlockSpec((1, 1, tm), lambda t, s, te, trb, ns, ntot: (trb[t], 0, 0), memory_space=pltpu.SMEM),
            pl.BlockSpec((1, 1, tm), lambda t, s, te, trb, ns, ntot: (trb[jnp.minimum(t + 1, n_tiles - 1)], 0, 0),
                         memory_space=pltpu.SMEM),
            pl.BlockSpec(memory_space=pl.ANY),
            pl.BlockSpec((None, d, tf), up_idx),
            pl.BlockSpec((None, d, tf), up_idx),
            pl.BlockSpec((None, f, tn), lambda t, s, te, trb, ns, ntot: (te[t], 0, down_col(t, s, ntot))),
        ],
        out_specs=pl.BlockSpec((tm, tn // 2), lambda t, s, te, trb, ns, ntot: (trb[t], down_col(t, s, ntot))),
        scratch_shapes=[
            pltpu.VMEM((2, tm, d // 2), jnp.uint32),
            pltpu.VMEM((n_up, tm, tf), F32),
            pltpu.SemaphoreType.DMA((2,)),
        ],
    )
    return pl.pallas_call(
        functools.partial(_expert_ffn_kernel, tm=tm, n_up=n_up),
        grid_spec=grid_spec,
        out_shape=jax.ShapeDtypeStruct((n_tiles * tm, d // 2), jnp.uint32),
        compiler_params=_params("arbitrary", "arbitrary"),
        name="expert_ffn",
    )(*meta, row_tok.reshape(n_tiles, 1, tm), row_tok.reshape(n_tiles, 1, tm), h, w_gate, w_up, w_down)


def _combine_kernel(dest_ref, dest_next_ref, wt_ref, x_ref, g_ref, y_hbm, x2_ref, h3_ref, buf_ref, sem, *, tb, chunk):
    i = pl.program_id(0)
    slot = i & 1

    def row_copy(src, k, r, sl):
        return pltpu.make_async_copy(y_hbm.at[pl.ds(src, 1)], buf_ref.at[sl, k, pl.ds(r, 1)], sem.at[sl])

    def start_gather(rows_ref, sl):
        def issue(r, carry):
            for k in range(TOP_K):
                row_copy(rows_ref[0, 0, TOP_K * r + k], k, r, sl).start(priority=k % 2)
            return carry

        lax.fori_loop(0, tb, issue, 0, unroll=GATHER_UNROLL // TOP_K)

    @pl.when(i == 0)
    def _():
        start_gather(dest_ref, slot)

    @pl.when(i + 1 < pl.num_programs(0))
    def _():
        start_gather(dest_next_ref, 1 - slot)

    def drain(r, carry):
        for k in range(TOP_K):
            row_copy(0, k, r, slot).wait()
        return carry

    lax.fori_loop(0, tb, drain, 0, unroll=GATHER_UNROLL // TOP_K)

    def expert_rows(k):
        words = buf_ref[slot, k]
        half = chunk // 2
        return jnp.concatenate([_unpack_bf16_pair(words[:, c * half:(c + 1) * half])
                                for c in range(words.shape[1] // half)], axis=1)

    wt = wt_ref[...]
    x2 = x_ref[...] + (wt[:, 0:1] * expert_rows(0) + wt[:, 1:2] * expert_rows(1))
    x2_ref[...] = x2
    ms = jnp.mean(x2 * x2, axis=-1, keepdims=True)
    h3_ref[...] = (x2 * lax.rsqrt(ms + EPS) * g_ref[...]).astype(h3_ref.dtype)


def _combine(x, y, dest, wts, gain, tb, chunk):
    t, d = x.shape
    tb = min(tb, t)
    nb = t // tb
    dest3 = dest.reshape(nb, 1, TOP_K * tb)
    row = pl.BlockSpec((tb, d), lambda i: (i, 0))
    return pl.pallas_call(
        functools.partial(_combine_kernel, tb=tb, chunk=chunk),
        grid=(nb,),
        in_specs=[
            pl.BlockSpec((1, 1, TOP_K * tb), lambda i: (i, 0, 0), memory_space=pltpu.SMEM),
            pl.BlockSpec((1, 1, TOP_K * tb), lambda i: (jnp.minimum(i + 1, nb - 1), 0, 0), memory_space=pltpu.SMEM),
            pl.BlockSpec((tb, LANES), lambda i: (i, 0)),
            row,
            pl.BlockSpec((1, d), lambda i: (0, 0)),
            pl.BlockSpec(memory_space=pl.ANY),
        ],
        out_specs=[row, row],
        out_shape=[jax.ShapeDtypeStruct((t, d), F32), jax.ShapeDtypeStruct((t, d), BF16)],
        scratch_shapes=[pltpu.VMEM((2, TOP_K, tb, d // 2), jnp.uint32), pltpu.SemaphoreType.DMA((2,))],
        compiler_params=_params("arbitrary"),
        name="moe_combine_norm",
    )(dest3, dest3, wts, x, gain.reshape(1, d), y)


def _ple_kernel(h_ref, p_ref, wp_ref, x_ref, wg_hbm, o_ref, stage_ref, wgb_ref, wpb_ref, sem, *, tn):
    _staged_weight(wg_hbm, stage_ref, wgb_ref, sem, tn)

    @pl.when(pl.program_id(1) == 0)
    def _():
        wpb_ref[...] = wp_ref[...].astype(BF16)

    a = jnp.dot(h_ref[...], wgb_ref[...], preferred_element_type=F32)
    b = jnp.dot(p_ref[...].astype(BF16), wpb_ref[...], preferred_element_type=F32)
    o_ref[...] = x_ref[...] + jax.nn.sigmoid(a) * b


def _ple(h, p, w_gate, w_proj, x, tm, tn):
    m, k = h.shape
    kp = p.shape[1]
    n = w_gate.shape[1]
    tm = min(tm, m)
    tn = min(tn, n)
    return pl.pallas_call(
        functools.partial(_ple_kernel, tn=tn),
        grid=(n // tn, m // tm),
        in_specs=[
            pl.BlockSpec((tm, k), lambda j, i: (i, 0)),
            pl.BlockSpec((tm, kp), lambda j, i: (i, 0)),
            pl.BlockSpec((kp, tn), lambda j, i: (0, j)),
            pl.BlockSpec((tm, tn), lambda j, i: (i, j)),
            pl.BlockSpec(memory_space=pl.ANY),
        ],
        out_specs=pl.BlockSpec((tm, tn), lambda j, i: (i, j)),
        out_shape=jax.ShapeDtypeStruct((m, n), F32),
        scratch_shapes=[pltpu.VMEM((k, tn), F32), pltpu.VMEM((k, tn), BF16), pltpu.VMEM((kp, tn), BF16),
                        pltpu.SemaphoreType.DMA],
        compiler_params=_params("arbitrary", "arbitrary"),
        name="ple_gate_residual",
    )(h, p, w_proj, x, w_gate)


EXPERT_TILE_ROWS = 640


def _rope_tables(s):
    half = HEAD_DIM // 2
    inv_freq = ROPE_THETA ** (-jnp.arange(half, dtype=F32) / half)
    ang = jnp.arange(s, dtype=jnp.int32).astype(F32)[:, None] * inv_freq[None, :]
    cos, sin = jnp.cos(ang), jnp.sin(ang)
    return jnp.concatenate([cos, cos], axis=-1), jnp.concatenate([-sin, sin], axis=-1)


def _expert_tiles(counts, n_assign, tm):
    n_experts = counts.shape[0]
    n_tiles = -(-n_assign // tm) + n_experts
    tiles_e = (counts + tm - 1) // tm
    tcum = jnp.cumsum(tiles_e)
    tstart = tcum - tiles_e
    pstart = tstart * tm
    total = tcum[-1]
    t = jnp.arange(n_tiles, dtype=jnp.int32)
    tc = jnp.minimum(t, total - 1)
    te = jnp.minimum(jnp.searchsorted(tcum, tc, side="right"), n_experts - 1).astype(jnp.int32)
    local = tc - tstart[te]
    rows_left = counts[te] - local * tm
    nsub = jnp.clip((rows_left + ROW_BLOCK - 1) // ROW_BLOCK, 0, tm // ROW_BLOCK)
    nsub = jnp.where(t < total, nsub, 0).astype(jnp.int32)
    trb = (tstart[te] + local).astype(jnp.int32)
    meta = (te, trb, nsub, jnp.reshape(total, (1,)).astype(jnp.int32))
    return pstart.astype(jnp.int32), meta, n_tiles


def _layer(x, p, w_in, w_conv, w_up_a, w_out_b, w_o, g_mix, g_ffn, w_group, b_group, w_router, b_router,
           w_gate, w_up, w_down, g_ple, w_ple_gate, w_ple_proj):
    s, d = x.shape
    cw = w_conv.shape[1]
    n_experts = w_gate.shape[0]
    ng = len(ATTN_DILATIONS)
    a_out = HEADS_PER_GROUP * HEAD_DIM
    q_cols = ng * a_out
    qkv_cols = q_cols + 2 * a_out
    col_u, col_gb, col_gc = qkv_cols, qkv_cols + cw, qkv_cols + 2 * cw
    col_gates = qkv_cols + 3 * cw

    h = _rmsnorm(x, g_mix, BF16, 512)
    rope_c, rope_s = _rope_tables(s)
    qkv = _qkv_proj(h, w_in, rope_c, rope_s, qkv_cols, q_cols + a_out, 512, qkv_cols // 2)
    y_a = _dilated_attention(qkv)
    y_b = _conv_proj(h, w_in, w_conv, col_u, col_gb, col_gc, 1024, 256)
    gates = _mm_cols(h, w_in, col_gates, 1024, 1024)
    merged = _merge(y_a, y_b, gates, w_up_a, w_out_b, 0, d, 1024, 1024)
    x1 = _resid_mm(merged, w_o, x, 512, 1024)

    n_route = N_EXPERT_GROUPS + n_experts
    w_route = jnp.pad(jnp.concatenate([w_group, w_router], axis=1), ((0, 0), (0, LANES - n_route)))
    b_route = jnp.pad(jnp.concatenate([b_group, b_router]), (0, LANES - n_route)).reshape(1, LANES)
    h2, eid, wts = _router(x1, g_ffn, w_route, b_route, N_EXPERT_GROUPS, n_experts, 256)
    rank, cnt = _rank(eid, 1024)
    counts = cnt[0, :n_experts]
    tm_e = EXPERT_TILE_ROWS
    pstart, meta, n_tiles = _expert_tiles(counts, s * TOP_K, tm_e)
    dest = _dest_rows(eid, rank, pstart, 1024)[:, :TOP_K].reshape(-1)
    tokens = jnp.repeat(jnp.arange(s, dtype=jnp.int32), TOP_K)
    row_tok = jnp.zeros((n_tiles * tm_e,), jnp.int32).at[dest].set(tokens, unique_indices=True)
    down_cols = min(1024, d)
    y = _expert_ffn(h2, row_tok, w_gate, w_up, w_down, meta, n_tiles, tm_e, 256, down_cols)
    x2, h3 = _combine(x1, y, dest, wts, g_ple, 256, down_cols)

    return _ple(h3, p, w_ple_gate, w_ple_proj, x2, 512, 1024)


def kernel(x, p, w_in, w_conv, w_up_a, w_out_b, w_o, norm_mix, norm_ffn, w_group, b_group, w_router, b_router,
           w_gate, w_up, w_down, norm_ple, w_ple_gate, w_ple_proj, norm_final):
    b, s, d = x.shape
    depth = w_in.shape[0]
    outs = []
    for bi in range(b):
        xb = x[bi]
        for i in range(depth):
            xb = _layer(xb, p[i, bi], w_in[i], w_conv[i], w_up_a[i], w_out_b[i], w_o[i], norm_mix[i], norm_ffn[i],
                        w_group[i], b_group[i], w_router[i], b_router[i], w_gate[i], w_up[i], w_down[i],
                        norm_ple[i], w_ple_gate[i], w_ple_proj[i])
        outs.append(_rmsnorm(xb, norm_final, x.dtype, 512))
    return jnp.stack(outs)
```

```python
import functools

import jax
import jax.numpy as jnp
from jax import lax
from jax.experimental import pallas as pl
from jax.experimental.pallas import tpu as pltpu

F32 = jnp.float32
BF16 = jnp.bfloat16

HEAD_DIM = 128
ATTN_DILATIONS = (1, 4, 16)
HEADS_PER_GROUP = 4
ROPE_THETA = 10000.0
ATTN_BLOCK = 128
NEG_BIG = -1e30
N_EXPERT_GROUPS = 4
TOP_K = 2
EPS = 1e-6
ROW_BLOCK = 128
LANES = 128

VMEM_LIMIT_BYTES = 58 * 1024 * 1024


def _params(*sem):
    return pltpu.CompilerParams(dimension_semantics=sem, vmem_limit_bytes=VMEM_LIMIT_BYTES)


def _rmsnorm_kernel(x_ref, g_ref, o_ref):
    x = x_ref[...]
    ms = jnp.mean(x * x, axis=-1, keepdims=True)
    o_ref[...] = (x * lax.rsqrt(ms + EPS) * g_ref[...]).astype(o_ref.dtype)


def _rmsnorm(x, gain, out_dtype, tm):
    n, d = x.shape
    tm = min(tm, n)
    return pl.pallas_call(
        _rmsnorm_kernel,
        grid=(n // tm,),
        in_specs=[pl.BlockSpec((tm, d), lambda i: (i, 0)), pl.BlockSpec((1, d), lambda i: (0, 0))],
        out_specs=pl.BlockSpec((tm, d), lambda i: (i, 0)),
        out_shape=jax.ShapeDtypeStruct((n, d), out_dtype),
        compiler_params=_params("arbitrary"),
        name="rmsnorm",
    )(x, gain.reshape(1, d))


def _staged_weight(w_hbm, stage_ref, wb_ref, sem, tn, col0=0):
    j = pl.program_id(0)
    i = pl.program_id(1)

    def fetch(jj):
        cols = pl.ds(pl.multiple_of(col0 + jj * tn, LANES), tn)
        return pltpu.make_async_copy(w_hbm.at[:, cols], stage_ref, sem)

    @pl.when(i == 0)
    def _():
        @pl.when(j == 0)
        def _():
            fetch(0).start()

        fetch(j).wait()
        wb_ref[...] = stage_ref[...].astype(BF16)

        @pl.when(j + 1 < pl.num_programs(0))
        def _():
            fetch(j + 1).start()


def _qkv_proj_kernel(h_ref, c_ref, s_ref, w_hbm, o_ref, stage_ref, wb_ref, sem, *, tn, n_col_tiles, n_rope_heads):
    _staged_weight(w_hbm, stage_ref, wb_ref, sem, tn)
    acc = jnp.dot(h_ref[...], wb_ref[...], preferred_element_type=F32)
    heads = o_ref.shape[0]
    j = pl.program_id(0)
    for jj in range(n_col_tiles):
        @pl.when(j == jj)
        def _(jj=jj):
            for hh in range(heads):
                t = acc[:, hh * HEAD_DIM:(hh + 1) * HEAD_DIM]
                if jj * heads + hh < n_rope_heads:
                    t = t * c_ref[...] + pltpu.roll(t, HEAD_DIM // 2, 1) * s_ref[...]
                o_ref[hh] = t


def _qkv_proj(h, w, rope_c, rope_s, n_cols, n_rope_cols, tm, tn):
    m, k = h.shape
    tm = min(tm, m)
    assert n_cols % tn == 0 and tn % HEAD_DIM == 0 and n_rope_cols % HEAD_DIM == 0 and m % tm == 0
    heads = tn // HEAD_DIM
    return pl.pallas_call(
        functools.partial(_qkv_proj_kernel, tn=tn, n_col_tiles=n_cols // tn, n_rope_heads=n_rope_cols // HEAD_DIM),
        grid=(n_cols // tn, m // tm),
        in_specs=[
            pl.BlockSpec((tm, k), lambda j, i: (i, 0)),
            pl.BlockSpec((tm, HEAD_DIM), lambda j, i: (i, 0)),
            pl.BlockSpec((tm, HEAD_DIM), lambda j, i: (i, 0)),
            pl.BlockSpec(memory_space=pl.ANY),
        ],
        out_specs=pl.BlockSpec((heads, tm, HEAD_DIM), lambda j, i: (j, i, 0)),
        out_shape=jax.ShapeDtypeStruct((n_cols // HEAD_DIM, m, HEAD_DIM), F32),
        scratch_shapes=[pltpu.VMEM((k, tn), F32), pltpu.VMEM((k, tn), BF16), pltpu.SemaphoreType.DMA],
        compiler_params=_params("arbitrary", "arbitrary"),
        name="qkv_proj_rope",
    )(h, rope_c, rope_s, w)


def _mm_kernel(a_ref, w_hbm, o_ref, stage_ref, wb_ref, sem, *, tn, col0):
    _staged_weight(w_hbm, stage_ref, wb_ref, sem, tn, col0)
    o_ref[...] = jnp.dot(a_ref[...], wb_ref[...], preferred_element_type=F32).astype(o_ref.dtype)


def _mm_cols(a, w, col0, tm, tn):
    m, k = a.shape
    n = w.shape[1] - col0
    tm = min(tm, m)
    while n % tn:
        tn //= 2
    assert tn % LANES == 0 and col0 % LANES == 0 and m % tm == 0
    return pl.pallas_call(
        functools.partial(_mm_kernel, tn=tn, col0=col0),
        grid=(n // tn, m // tm),
        in_specs=[
            pl.BlockSpec((tm, k), lambda j, i: (i, 0)),
            pl.BlockSpec(memory_space=pl.ANY),
        ],
        out_specs=pl.BlockSpec((tm, tn), lambda j, i: (i, j)),
        out_shape=jax.ShapeDtypeStruct((m, n), BF16),
        scratch_shapes=[pltpu.VMEM((k, tn), F32), pltpu.VMEM((k, tn), BF16), pltpu.SemaphoreType.DMA],
        compiler_params=_params("arbitrary", "arbitrary"),
        name="merge_gate_proj",
    )(a, w)


ATTN_SUPER = ATTN_BLOCK * max(ATTN_DILATIONS)
LOG2_E = 1.4426950408889634
LN_2 = 0.6931471805599453


def _rows(start, size, stride):
    return pl.ds(start, size) if stride == 1 else pl.ds(start, size, stride=stride)


def _dilated_attn_kernel(q0_ref, q1_ref, q2_ref, kc_ref, kp_ref, vc_ref, vp_ref, y_ref, o_ref, lse_ref):
    sb = pl.program_id(1)
    sup = ATTN_SUPER
    qi = lax.broadcasted_iota(jnp.int32, (ATTN_BLOCK, 2 * ATTN_BLOCK), 0)
    kj = lax.broadcasted_iota(jnp.int32, (ATTN_BLOCK, 2 * ATTN_BLOCK), 1)
    dist = qi + ATTN_BLOCK - kj
    band = (dist >= 0) & (dist <= ATTN_BLOCK)
    band_first = band & ((kj >= ATTN_BLOCK) | (sb > 0))
    q_scale = HEAD_DIM ** -0.5 * LOG2_E

    for g, (q_ref, d) in enumerate(zip((q0_ref, q1_ref, q2_ref), ATTN_DILATIONS)):
        span = ATTN_BLOCK * d
        for j in range(sup // span):
            for r in range(d):
                row0 = j * span + r
                rows = _rows(row0, ATTN_BLOCK, d)
                if j == 0:
                    prev_rows = _rows(sup - span + r, ATTN_BLOCK, d)
                    k_prev, v_prev = kp_ref[prev_rows, :], vp_ref[prev_rows, :]
                else:
                    prev_rows = _rows(row0 - span, ATTN_BLOCK, d)
                    k_prev, v_prev = kc_ref[prev_rows, :], vc_ref[prev_rows, :]
                q = (q_ref[rows, :] * q_scale).astype(BF16)
                k = jnp.concatenate([k_prev, kc_ref[rows, :]], axis=0).astype(BF16)
                v = jnp.concatenate([v_prev, vc_ref[rows, :]], axis=0).astype(BF16)
                s = lax.dot_general(q, k, (((1,), (1,)), ((), ())), preferred_element_type=F32)
                s = jnp.where(band_first if j == 0 else band, s, NEG_BIG)
                m = jnp.max(s, axis=-1, keepdims=True)
                e = jnp.exp2(s - m)
                ssum = jnp.sum(e, axis=-1, keepdims=True)
                p = (e * (1.0 / ssum)).astype(BF16)
                o_ref[g, rows, :] = jnp.dot(p, v, preferred_element_type=F32)
                lse = m * LN_2 + jnp.log(ssum)
                lse_ref[g, rows, :] = jnp.broadcast_to(lse, (ATTN_BLOCK, HEAD_DIM))

    lse = lse_ref[...]
    m = jnp.max(lse, axis=0)
    e = jnp.exp(lse - m[None])
    w = e / jnp.sum(e, axis=0)[None]
    y_ref[...] = jnp.sum(w * o_ref[...], axis=0).astype(y_ref.dtype)


def _dilated_attention(qkv):
    nh, s, hd = qkv.shape
    ng = len(ATTN_DILATIONS)
    hpg = HEADS_PER_GROUP
    assert nh == (ng + 2) * hpg and hd == HEAD_DIM and s % ATTN_SUPER == 0
    sup = ATTN_SUPER

    def q_spec(g):
        return pl.BlockSpec((None, sup, hd), lambda h, b: (g * hpg + h, b, 0))

    def kv_spec(first_head, prev):
        if prev:
            return pl.BlockSpec((None, sup, hd), lambda h, b: (first_head + h, jnp.maximum(b - 1, 0), 0))
        return pl.BlockSpec((None, sup, hd), lambda h, b: (first_head + h, b, 0))

    k0, v0 = ng * hpg, (ng + 1) * hpg
    return pl.pallas_call(
        _dilated_attn_kernel,
        grid=(hpg, s // sup),
        in_specs=[q_spec(0), q_spec(1), q_spec(2), kv_spec(k0, False), kv_spec(k0, True),
                  kv_spec(v0, False), kv_spec(v0, True)],
        out_specs=pl.BlockSpec((sup, hd), lambda h, b: (b, h)),
        out_shape=jax.ShapeDtypeStruct((s, hpg * hd), BF16),
        scratch_shapes=[pltpu.VMEM((ng, sup, hd), F32), pltpu.VMEM((ng, sup, hd), F32)],
        compiler_params=_params("arbitrary", "arbitrary"),
        name="dilated_attention",
    )(qkv, qkv, qkv, qkv, qkv, qkv, qkv)


CONV_HALO = 16


def _conv_proj_kernel(h_ref, wc_ref, w_hbm, o_ref, stage_ref, wb_ref, carry_ref, sem, *, tc, cols):
    j = pl.program_id(0)
    i = pl.program_id(1)

    def fetch(jj):
        return [pltpu.make_async_copy(w_hbm.at[:, pl.ds(pl.multiple_of(c0 + jj * tc, LANES), tc)],
                                      stage_ref.at[g], sem.at[g]) for g, c0 in enumerate(cols)]

    @pl.when(i == 0)
    def _():
        @pl.when(j == 0)
        def _():
            for c in fetch(0):
                c.start()

        for c in fetch(j):
            c.wait()
        for g in range(len(cols)):
            wb_ref[:, g * tc:(g + 1) * tc] = stage_ref[g].astype(BF16)
        carry_ref[...] = jnp.zeros_like(carry_ref)

        @pl.when(j + 1 < pl.num_programs(0))
        def _():
            for c in fetch(j + 1):
                c.start()

    acc = jnp.dot(h_ref[...], wb_ref[...], preferred_element_type=F32)
    u, gate_b, gate_c = acc[:, :tc], acc[:, tc:2 * tc], acc[:, 2 * tc:]
    cur = gate_c * u
    big = jnp.concatenate([carry_ref[...], cur], axis=0)
    back1 = pltpu.roll(big, 1, 0)[CONV_HALO:]
    back2 = pltpu.roll(big, 2, 0)[CONV_HALO:]
    w = wc_ref[...]
    y = w[0:1] * back2 + w[1:2] * back1 + w[2:3] * cur
    o_ref[...] = (gate_b * y).astype(o_ref.dtype)
    carry_ref[...] = cur[cur.shape[0] - CONV_HALO:, :]


def _conv_proj(h, w, w_conv, col_u, col_gb, col_gc, tm, tc):
    s, k = h.shape
    kw, cw = w_conv.shape
    assert kw == 3
    tm = min(tm, s)
    tc = min(tc, cw)
    assert cw % tc == 0 and tc % LANES == 0 and s % tm == 0 and tm % CONV_HALO == 0
    cols = (col_u, col_gb, col_gc)
    return pl.pallas_call(
        functools.partial(_conv_proj_kernel, tc=tc, cols=cols),
        grid=(cw // tc, s // tm),
        in_specs=[
            pl.BlockSpec((tm, k), lambda j, i: (i, 0)),
            pl.BlockSpec((kw, tc), lambda j, i: (0, j)),
            pl.BlockSpec(memory_space=pl.ANY),
        ],
        out_specs=pl.BlockSpec((tm, tc), lambda j, i: (i, j)),
        out_shape=jax.ShapeDtypeStruct((s, cw), BF16),
        scratch_shapes=[pltpu.VMEM((len(cols), k, tc), F32), pltpu.VMEM((k, len(cols) * tc), BF16),
                        pltpu.VMEM((CONV_HALO, tc), F32), pltpu.SemaphoreType.DMA((len(cols),))],
        compiler_params=_params("arbitrary", "arbitrary"),
        name="conv_proj",
    )(h, w_conv, w)


def _merge_kernel(ya_ref, yb_ref, wa_ref, wb_ref, ma_ref, mb_ref, o_ref, wab_ref, wbb_ref):
    i = pl.program_id(1)

    @pl.when(i == 0)
    def _():
        wab_ref[...] = wa_ref[...].astype(BF16)
        wbb_ref[...] = wb_ref[...].astype(BF16)

    a = jnp.dot(ya_ref[...], wab_ref[...], preferred_element_type=F32)
    b = jnp.dot(yb_ref[...], wbb_ref[...], preferred_element_type=F32)
    ga = jax.nn.sigmoid(ma_ref[...].astype(F32))
    gb = jax.nn.sigmoid(mb_ref[...].astype(F32))
    o_ref[...] = (ga * a + gb * b).astype(o_ref.dtype)


def _merge(y_a, y_b, proj, w_a, w_b, col_ma, col_mb, tm, tn):
    m, ka = y_a.shape
    kb = y_b.shape[1]
    n = w_a.shape[1]
    tm = min(tm, m)
    tn = min(tn, n)
    while col_ma % tn or col_mb % tn:
        tn //= 2
    assert tn % LANES == 0
    return pl.pallas_call(
        _merge_kernel,
        grid=(n // tn, m // tm),
        in_specs=[
            pl.BlockSpec((tm, ka), lambda j, i: (i, 0)),
            pl.BlockSpec((tm, kb), lambda j, i: (i, 0)),
            pl.BlockSpec((ka, tn), lambda j, i: (0, j)),
            pl.BlockSpec((kb, tn), lambda j, i: (0, j)),
            pl.BlockSpec((tm, tn), lambda j, i: (i, col_ma // tn + j)),
            pl.BlockSpec((tm, tn), lambda j, i: (i, col_mb // tn + j)),
        ],
        out_specs=pl.BlockSpec((tm, tn), lambda j, i: (i, j)),
        out_shape=jax.ShapeDtypeStruct((m, n), BF16),
        scratch_shapes=[pltpu.VMEM((ka, tn), BF16), pltpu.VMEM((kb, tn), BF16)],
        compiler_params=_params("arbitrary", "arbitrary"),
        name="mixer_merge",
    )(y_a, y_b, w_a, w_b, proj, proj)


def _resid_mm_kernel(a_ref, x_ref, w_hbm, o_ref, stage_ref, wb_ref, sem, *, tn):
    _staged_weight(w_hbm, stage_ref, wb_ref, sem, tn)
    o_ref[...] = x_ref[...] + jnp.dot(a_ref[...], wb_ref[...], preferred_element_type=F32)


def _resid_mm(a, w, x, tm, tn):
    m, k = a.shape
    n = w.shape[1]
    tm = min(tm, m)
    tn = min(tn, n)
    return pl.pallas_call(
        functools.partial(_resid_mm_kernel, tn=tn),
        grid=(n // tn, m // tm),
        in_specs=[
            pl.BlockSpec((tm, k), lambda j, i: (i, 0)),
            pl.BlockSpec((tm, tn), lambda j, i: (i, j)),
            pl.BlockSpec(memory_space=pl.ANY),
        ],
        out_specs=pl.BlockSpec((tm, tn), lambda j, i: (i, j)),
        out_shape=jax.ShapeDtypeStruct((m, n), F32),
        scratch_shapes=[pltpu.VMEM((k, tn), F32), pltpu.VMEM((k, tn), BF16), pltpu.SemaphoreType.DMA],
        compiler_params=_params("arbitrary", "arbitrary"),
        name="out_proj_residual",
    )(a, x, w)


def _split_bf16(a):
    hi = a.astype(BF16)
    lo = (a - hi.astype(F32)).astype(BF16)
    return hi, lo


def _pack_bf16_pair(a):
    c = a.shape[1] // 2
    bits = lax.bitcast_convert_type(a.astype(BF16).astype(F32), jnp.uint32)
    return (bits[:, :c] >> 16) | (bits[:, c:] & jnp.uint32(0xFFFF0000))


def _unpack_bf16_pair(w):
    lo = lax.bitcast_convert_type(w << 16, F32)
    hi = lax.bitcast_convert_type(w & jnp.uint32(0xFFFF0000), F32)
    return jnp.concatenate([lo, hi], axis=1)


def _router_kernel(x_ref, g_ref, w_ref, b_ref, h_ref, eid_ref, wt_ref, whl_ref, *, n_groups, n_experts):
    @pl.when(pl.program_id(0) == 0)
    def _():
        whl_ref[:, :LANES], whl_ref[:, LANES:] = _split_bf16(w_ref[...])

    x = x_ref[...]
    ms = jnp.mean(x * x, axis=-1, keepdims=True)
    h = x * lax.rsqrt(ms + EPS) * g_ref[...]
    h_ref[...] = _pack_bf16_pair(h)
    hh, hl = _split_bf16(h)
    both = jnp.dot(hh, whl_ref[...], preferred_element_type=F32)
    logits = (both[:, :LANES] + both[:, LANES:]
              + jnp.dot(hl, whl_ref[:, :LANES], preferred_element_type=F32)) + b_ref[...]
    tm = x.shape[0]
    per_group = n_experts // n_groups
    lane = lax.broadcasted_iota(jnp.int32, (tm, LANES), 1)
    is_g = lane < n_groups
    gl = jnp.where(is_g, logits, NEG_BIG)
    gmax = jnp.max(gl, axis=-1, keepdims=True)
    gidx = jnp.min(jnp.where(gl == gmax, lane, LANES), axis=-1, keepdims=True)
    gsum = jnp.sum(jnp.where(is_g, jnp.exp(gl - gmax), 0.0), axis=-1, keepdims=True)
    g_w = 1.0 / gsum
    lane_e = lane - n_groups
    sel = (lane_e >= 0) & (lane_e < n_experts) & (lane_e // per_group == gidx)
    el = jnp.where(sel, logits, NEG_BIG)
    emax = jnp.max(el, axis=-1, keepdims=True)
    i1 = jnp.min(jnp.where(el == emax, lane, LANES), axis=-1, keepdims=True)
    esum = jnp.sum(jnp.where(sel, jnp.exp(el - emax), 0.0), axis=-1, keepdims=True)
    el2 = jnp.where(lane == i1, NEG_BIG, el)
    emax2 = jnp.max(el2, axis=-1, keepdims=True)
    i2 = jnp.min(jnp.where(el2 == emax2, lane, LANES), axis=-1, keepdims=True)
    p1 = 1.0 / esum
    p2 = jnp.exp(emax2 - emax) / esum
    den = p1 + p2
    eid_ref[...] = jnp.where(lane == 0, i1 - n_groups, jnp.where(lane == 1, i2 - n_groups, 0))
    wt_ref[...] = jnp.where(lane == 0, g_w * (p1 / den), jnp.where(lane == 1, g_w * (p2 / den), 0.0))


def _router(x, gain, w_route, b_route, n_groups, n_experts, tm):
    t, d = x.shape
    tm = min(tm, t)
    row = pl.BlockSpec((tm, d), lambda i: (i, 0))
    narrow = pl.BlockSpec((tm, LANES), lambda i: (i, 0))
    return pl.pallas_call(
        functools.partial(_router_kernel, n_groups=n_groups, n_experts=n_experts),
        grid=(t // tm,),
        in_specs=[row, pl.BlockSpec((1, d), lambda i: (0, 0)), pl.BlockSpec((d, LANES), lambda i: (0, 0)),
                  pl.BlockSpec((1, LANES), lambda i: (0, 0))],
        out_specs=[pl.BlockSpec((tm, d // 2), lambda i: (i, 0)), narrow, narrow],
        out_shape=[jax.ShapeDtypeStruct((t, d // 2), jnp.uint32), jax.ShapeDtypeStruct((t, LANES), jnp.int32),
                   jax.ShapeDtypeStruct((t, LANES), F32)],
        scratch_shapes=[pltpu.VMEM((d, 2 * LANES), BF16)],
        compiler_params=_params("arbitrary"),
        name="ffn_norm_router",
    )(x, gain.reshape(1, d), w_route, b_route)


def _rank_kernel(eid_ref, rank_ref, cnt_ref, carry_ref):
    i = pl.program_id(0)

    @pl.when(i == 0)
    def _():
        carry_ref[...] = jnp.zeros_like(carry_ref)

    eid = eid_ref[...]
    tm = eid.shape[0]
    lane = lax.broadcasted_iota(jnp.int32, (tm, LANES), 1)
    oh0 = lane == eid[:, 0:1]
    oh1 = lane == eid[:, 1:2]
    ohs = oh0.astype(F32) + oh1.astype(F32)
    r = lax.broadcasted_iota(jnp.int32, (tm, tm), 0)
    c = lax.broadcasted_iota(jnp.int32, (tm, tm), 1)
    earlier = (c < r).astype(BF16)
    before = jnp.dot(earlier, ohs.astype(BF16), preferred_element_type=F32) + carry_ref[...]
    rank0 = jnp.sum(jnp.where(oh0, before, 0.0), axis=-1, keepdims=True)
    rank1 = jnp.sum(jnp.where(oh1, before, 0.0), axis=-1, keepdims=True)
    rank_ref[...] = jnp.where(lane == 0, rank0, jnp.where(lane == 1, rank1, 0.0)).astype(jnp.int32)
    total = carry_ref[...] + jnp.sum(ohs, axis=0, keepdims=True)
    carry_ref[...] = total
    cnt_ref[...] = jnp.broadcast_to(total, cnt_ref.shape).astype(jnp.int32)


def _rank(eid, tm):
    t = eid.shape[0]
    tm = min(tm, t)
    narrow = pl.BlockSpec((tm, LANES), lambda i: (i, 0))
    return pl.pallas_call(
        _rank_kernel,
        grid=(t // tm,),
        in_specs=[narrow],
        out_specs=[narrow, pl.BlockSpec((8, LANES), lambda i: (0, 0))],
        out_shape=[jax.ShapeDtypeStruct((t, LANES), jnp.int32), jax.ShapeDtypeStruct((8, LANES), jnp.int32)],
        scratch_shapes=[pltpu.VMEM((1, LANES), F32)],
        compiler_params=_params("arbitrary"),
        name="expert_rank",
    )(eid)


def _dest_kernel(eid_ref, rank_ref, ps_ref, o_ref):
    eid = eid_ref[...]
    lane = lax.broadcasted_iota(jnp.int32, eid.shape, 1)
    ps = ps_ref[...]
    s0 = jnp.sum(jnp.where(lane == eid[:, 0:1], ps, 0.0), axis=-1, keepdims=True)
    s1 = jnp.sum(jnp.where(lane == eid[:, 1:2], ps, 0.0), axis=-1, keepdims=True)
    start = jnp.where(lane == 0, s0, jnp.where(lane == 1, s1, 0.0))
    o_ref[...] = start.astype(jnp.int32) + rank_ref[...]


def _dest_rows(eid, rank, pstart, tm):
    t = eid.shape[0]
    tm = min(tm, t)
    ps = jnp.zeros((1, LANES), F32).at[0, :pstart.shape[0]].set(pstart.astype(F32))
    narrow = pl.BlockSpec((tm, LANES), lambda i: (i, 0))
    return pl.pallas_call(
        _dest_kernel,
        grid=(t // tm,),
        in_specs=[narrow, narrow, pl.BlockSpec((1, LANES), lambda i: (0, 0))],
        out_specs=narrow,
        out_shape=jax.ShapeDtypeStruct((t, LANES), jnp.int32),
        compiler_params=_params("arbitrary"),
        name="expert_dest_rows",
    )(eid, rank, ps)


GATHER_UNROLL = 8
GATHER_PRIORITY = 1


def _expert_ffn_kernel(te_ref, trb_ref, nsub_ref, ntot_ref, tok_ref, tok_next_ref, h_hbm, wg_ref, wu_ref, wd_ref,
                       y_ref, land_ref, hid_ref, sem, *, tm, n_up):
    t = pl.program_id(0)
    s = pl.program_id(1)
    n_tiles = pl.num_programs(0)
    nsub = nsub_ref[t]
    slot = t & 1

    def row_copy(tok, r, sl):
        return pltpu.make_async_copy(h_hbm.at[pl.ds(tok, 1)], land_ref.at[sl, pl.ds(r, 1)], sem.at[sl])

    def start_gather(toks_ref, blocks, sl):
        def issue(i, carry):
            for u in range(GATHER_UNROLL):
                r = i * GATHER_UNROLL + u
                row_copy(toks_ref[0, 0, r], r, sl).start(priority=GATHER_PRIORITY)
            return carry

        lax.fori_loop(0, blocks * (ROW_BLOCK // GATHER_UNROLL), issue, 0)

    @pl.when(nsub > 0)
    def _():
        @pl.when(s == 0)
        def _():
            @pl.when(t == 0)
            def _():
                start_gather(tok_ref, nsub, slot)

            def drain(i, carry):
                for u in range(GATHER_UNROLL):
                    row_copy(0, 0, slot).wait()
                return carry

            lax.fori_loop(0, nsub * (ROW_BLOCK // GATHER_UNROLL), drain, 0)

        @pl.when(s == 1)
        def _():
            nxt = jnp.minimum(t + 1, n_tiles - 1)
            n_next = jnp.where(t + 1 < n_tiles, nsub_ref[nxt], 0)
            start_gather(tok_next_ref, n_next, 1 - slot)

        for n in range(1, tm // ROW_BLOCK + 1):
            rows = n * ROW_BLOCK

            @pl.when((nsub == n) & (s < n_up))
            def _(rows=rows):
                x = _unpack_bf16_pair(land_ref[slot, pl.ds(0, rows), :])
                g = jnp.dot(x, wg_ref[...], preferred_element_type=F32)
                u = jnp.dot(x, wu_ref[...], preferred_element_type=F32)
                hid_ref[s, pl.ds(0, rows), :] = g * jax.nn.sigmoid(g) * u

            @pl.when((nsub == n) & (s >= n_up))
            def _(rows=rows):
                hid = jnp.concatenate([hid_ref[c, pl.ds(0, rows), :] for c in range(n_up)], axis=1)
                y = jnp.dot(hid, wd_ref[...], preferred_element_type=F32)
                y_ref[pl.ds(0, rows), :] = _pack_bf16_pair(y)
                if rows < tm:
                    y_ref[pl.ds(rows, tm - rows), :] = jnp.zeros((tm - rows, y_ref.shape[1]), y_ref.dtype)


def _expert_ffn(h, row_tok, w_gate, w_up, w_down, meta, n_tiles, tm, tf, tn):
    d = w_gate.shape[1]
    f = w_gate.shape[2]
    tf = min(tf, f)
    tn = min(tn, d)
    n_up = f // tf
    n_down = d // tn
    n_steps = n_up + n_down

    def step(t, s, ntot):
        return jnp.where(t < ntot[0], s, n_steps - 1)

    def up_idx(t, s, te, trb, ns, ntot):
        return (te[t], 0, jnp.minimum(step(t, s, ntot), n_up - 1))

    def down_col(t, s, ntot):
        return jnp.maximum(step(t, s, ntot) - n_up, 0)

    grid_spec = pltpu.PrefetchScalarGridSpec(
        num_scalar_prefetch=4,
        grid=(n_tiles, n_steps),
        in_specs=[
            pl.BlockSpec((1, 1, tm), lambda t, s, te, trb, ns, ntot: (trb[t], 0, 0), memory_space=pltpu.SMEM),
            pl.BlockSpec((1, 1, tm), lambda t, s, te, trb, ns, ntot: (trb[jnp.minimum(t + 1, n_tiles - 1)], 0, 0),
                         memory_space=pltpu.SMEM),
            pl.BlockSpec(memory_space=pl.ANY),
            pl.BlockSpec((None, d, tf), up_idx),
            pl.BlockSpec((None, d, tf), up_idx),
            pl.BlockSpec((None, f, tn), lambda t, s, te, trb, ns, ntot: (te[t], 0, down_col(t, s, ntot))),
        ],
        out_specs=pl.BlockSpec((tm, tn // 2), lambda t, s, te, trb, ns, ntot: (trb[t], down_col(t, s, ntot))),
        scratch_shapes=[
            pltpu.VMEM((2, tm, d // 2), jnp.uint32),
            pltpu.VMEM((n_up, tm, tf), F32),
            pltpu.SemaphoreType.DMA((2,)),
        ],
    )
    return pl.pallas_call(
        functools.partial(_expert_ffn_kernel, tm=tm, n_up=n_up),
        grid_spec=grid_spec,
        out_shape=jax.ShapeDtypeStruct((n_tiles * tm, d // 2), jnp.uint32),
        compiler_params=_params("arbitrary", "arbitrary"),
        name="expert_ffn",
    )(*meta, row_tok.reshape(n_tiles, 1, tm), row_tok.reshape(n_tiles, 1, tm), h, w_gate, w_up, w_down)


def _combine_kernel(dest_ref, dest_next_ref, wt_ref, x_ref, g_ref, y_hbm, x2_ref, h3_ref, buf_ref, sem, *, tb, chunk):
    i = pl.program_id(0)
    slot = i & 1

    def row_copy(src, k, r, sl):
        return pltpu.make_async_copy(y_hbm.at[pl.ds(src, 1)], buf_ref.at[sl, k, pl.ds(r, 1)], sem.at[sl])

    def start_gather(rows_ref, sl):
        def issue(r, carry):
            for k in range(TOP_K):
                row_copy(rows_ref[0, 0, TOP_K * r + k], k, r, sl).start(priority=GATHER_PRIORITY)
            return carry

        lax.fori_loop(0, tb, issue, 0, unroll=GATHER_UNROLL // TOP_K)

    @pl.when(i == 0)
    def _():
        start_gather(dest_ref, slot)

    @pl.when(i + 1 < pl.num_programs(0))
    def _():
        start_gather(dest_next_ref, 1 - slot)

    def drain(r, carry):
        for k in range(TOP_K):
            row_copy(0, k, r, slot).wait()
        return carry

    lax.fori_loop(0, tb, drain, 0, unroll=GATHER_UNROLL // TOP_K)

    def expert_rows(k):
        words = buf_ref[slot, k]
        half = chunk // 2
        return jnp.concatenate([_unpack_bf16_pair(words[:, c * half:(c + 1) * half])
                                for c in range(words.shape[1] // half)], axis=1)

    wt = wt_ref[...]
    x2 = x_ref[...] + (wt[:, 0:1] * expert_rows(0) + wt[:, 1:2] * expert_rows(1))
    x2_ref[...] = x2
    ms = jnp.mean(x2 * x2, axis=-1, keepdims=True)
    h3_ref[...] = (x2 * lax.rsqrt(ms + EPS) * g_ref[...]).astype(h3_ref.dtype)


def _combine(x, y, dest, wts, gain, tb, chunk):
    t, d = x.shape
    tb = min(tb, t)
    nb = t // tb
    dest3 = dest.reshape(nb, 1, TOP_K * tb)
    row = pl.BlockSpec((tb, d), lambda i: (i, 0))
    return pl.pallas_call(
        functools.partial(_combine_kernel, tb=tb, chunk=chunk),
        grid=(nb,),
        in_specs=[
            pl.BlockSpec((1, 1, TOP_K * tb), lambda i: (i, 0, 0), memory_space=pltpu.SMEM),
            pl.BlockSpec((1, 1, TOP_K * tb), lambda i: (jnp.minimum(i + 1, nb - 1), 0, 0), memory_space=pltpu.SMEM),
            pl.BlockSpec((tb, LANES), lambda i: (i, 0)),
            row,
            pl.BlockSpec((1, d), lambda i: (0, 0)),
            pl.BlockSpec(memory_space=pl.ANY),
        ],
        out_specs=[row, row],
        out_shape=[jax.ShapeDtypeStruct((t, d), F32), jax.ShapeDtypeStruct((t, d), BF16)],
        scratch_shapes=[pltpu.VMEM((2, TOP_K, tb, d // 2), jnp.uint32), pltpu.SemaphoreType.DMA((2,))],
        compiler_params=_params("arbitrary"),
        name="moe_combine_norm",
    )(dest3, dest3, wts, x, gain.reshape(1, d), y)


def _ple_kernel(h_ref, p_ref, wp_ref, x_ref, wg_hbm, o_ref, stage_ref, wgb_ref, wpb_ref, sem, *, tn):
    _staged_weight(wg_hbm, stage_ref, wgb_ref, sem, tn)

    @pl.when(pl.program_id(1) == 0)
    def _():
        wpb_ref[...] = wp_ref[...].astype(BF16)

    a = jnp.dot(h_ref[...], wgb_ref[...], preferred_element_type=F32)
    b = jnp.dot(p_ref[...].astype(BF16), wpb_ref[...], preferred_element_type=F32)
    o_ref[...] = x_ref[...] + jax.nn.sigmoid(a) * b


def _ple(h, p, w_gate, w_proj, x, tm, tn):
    m, k = h.shape
    kp = p.shape[1]
    n = w_gate.shape[1]
    tm = min(tm, m)
    tn = min(tn, n)
    return pl.pallas_call(
        functools.partial(_ple_kernel, tn=tn),
        grid=(n // tn, m // tm),
        in_specs=[
            pl.BlockSpec((tm, k), lambda j, i: (i, 0)),
            pl.BlockSpec((tm, kp), lambda j, i: (i, 0)),
            pl.BlockSpec((kp, tn), lambda j, i: (0, j)),
            pl.BlockSpec((tm, tn), lambda j, i: (i, j)),
            pl.BlockSpec(memory_space=pl.ANY),
        ],
        out_specs=pl.BlockSpec((tm, tn), lambda j, i: (i, j)),
        out_shape=jax.ShapeDtypeStruct((m, n), F32),
        scratch_shapes=[pltpu.VMEM((k, tn), F32), pltpu.VMEM((k, tn), BF16), pltpu.VMEM((kp, tn), BF16),
                        pltpu.SemaphoreType.DMA],
        compiler_params=_params("arbitrary", "arbitrary"),
        name="ple_gate_residual",
    )(h, p, w_proj, x, w_gate)


EXPERT_TILE_ROWS = 640

NORM_ROWS = 512
QKV_TILE_ROWS = 512
CONV_PROJ_TILE = (1024, 256)
GATE_PROJ_TILE = (1024, 1024)
MERGE_TILE = (1024, 1024)
OUT_PROJ_TILE = (512, 1024)
PLE_TILE = (512, 1024)
ROUTER_ROWS = 256
RANK_ROWS = 1024
FFN_UP_COLS = 256
FFN_DOWN_COLS = 1024
COMBINE_ROWS = 256


def _rope_tables(s):
    half = HEAD_DIM // 2
    inv_freq = ROPE_THETA ** (-jnp.arange(half, dtype=F32) / half)
    ang = jnp.arange(s, dtype=jnp.int32).astype(F32)[:, None] * inv_freq[None, :]
    cos, sin = jnp.cos(ang), jnp.sin(ang)
    return jnp.concatenate([cos, cos], axis=-1), jnp.concatenate([-sin, sin], axis=-1)


def _expert_tiles(counts, n_assign, tm):
    n_experts = counts.shape[0]
    n_tiles = -(-n_assign // tm) + n_experts
    tiles_e = (counts + tm - 1) // tm
    tcum = jnp.cumsum(tiles_e)
    tstart = tcum - tiles_e
    pstart = tstart * tm
    total = tcum[-1]
    t = jnp.arange(n_tiles, dtype=jnp.int32)
    tc = jnp.minimum(t, total - 1)
    te = jnp.minimum(jnp.searchsorted(tcum, tc, side="right"), n_experts - 1).astype(jnp.int32)
    local = tc - tstart[te]
    rows_left = counts[te] - local * tm
    nsub = jnp.clip((rows_left + ROW_BLOCK - 1) // ROW_BLOCK, 0, tm // ROW_BLOCK)
    nsub = jnp.where(t < total, nsub, 0).astype(jnp.int32)
    trb = (tstart[te] + local).astype(jnp.int32)
    meta = (te, trb, nsub, jnp.reshape(total, (1,)).astype(jnp.int32))
    return pstart.astype(jnp.int32), meta, n_tiles


def _layer(x, p, w_in, w_conv, w_up_a, w_out_b, w_o, g_mix, g_ffn, w_group, b_group, w_router, b_router,
           w_gate, w_up, w_down, g_ple, w_ple_gate, w_ple_proj):
    s, d = x.shape
    cw = w_conv.shape[1]
    n_experts = w_gate.shape[0]
    ng = len(ATTN_DILATIONS)
    a_out = HEADS_PER_GROUP * HEAD_DIM
    q_cols = ng * a_out
    qkv_cols = q_cols + 2 * a_out
    col_u, col_gb, col_gc = qkv_cols, qkv_cols + cw, qkv_cols + 2 * cw
    col_gates = qkv_cols + 3 * cw

    h = _rmsnorm(x, g_mix, BF16, NORM_ROWS)
    rope_c, rope_s = _rope_tables(s)
    qkv = _qkv_proj(h, w_in, rope_c, rope_s, qkv_cols, q_cols + a_out, QKV_TILE_ROWS, qkv_cols // 2)
    y_a = _dilated_attention(qkv)
    y_b = _conv_proj(h, w_in, w_conv, col_u, col_gb, col_gc, *CONV_PROJ_TILE)
    gates = _mm_cols(h, w_in, col_gates, *GATE_PROJ_TILE)
    merged = _merge(y_a, y_b, gates, w_up_a, w_out_b, 0, d, *MERGE_TILE)
    x1 = _resid_mm(merged, w_o, x, *OUT_PROJ_TILE)

    n_route = N_EXPERT_GROUPS + n_experts
    w_route = jnp.pad(jnp.concatenate([w_group, w_router], axis=1), ((0, 0), (0, LANES - n_route)))
    b_route = jnp.pad(jnp.concatenate([b_group, b_router]), (0, LANES - n_route)).reshape(1, LANES)
    h2, eid, wts = _router(x1, g_ffn, w_route, b_route, N_EXPERT_GROUPS, n_experts, ROUTER_ROWS)
    rank, cnt = _rank(eid, RANK_ROWS)
    counts = cnt[0, :n_experts]
    tm_e = EXPERT_TILE_ROWS
    pstart, meta, n_tiles = _expert_tiles(counts, s * TOP_K, tm_e)
    dest = _dest_rows(eid, rank, pstart, RANK_ROWS)[:, :TOP_K].reshape(-1)
    tokens = jnp.repeat(jnp.arange(s, dtype=jnp.int32), TOP_K)
    row_tok = jnp.zeros((n_tiles * tm_e,), jnp.int32).at[dest].set(tokens, unique_indices=True)
    down_cols = min(FFN_DOWN_COLS, d)
    y = _expert_ffn(h2, row_tok, w_gate, w_up, w_down, meta, n_tiles, tm_e, FFN_UP_COLS, down_cols)
    x2, h3 = _combine(x1, y, dest, wts, g_ple, COMBINE_ROWS, down_cols)

    return _ple(h3, p, w_ple_gate, w_ple_proj, x2, *PLE_TILE)


def kernel(x, p, w_in, w_conv, w_up_a, w_out_b, w_o, norm_mix, norm_ffn, w_group, b_group, w_router, b_router,
           w_gate, w_up, w_down, norm_ple, w_ple_gate, w_ple_proj, norm_final):
    b, s, d = x.shape
    depth = w_in.shape[0]
    outs = []
    for bi in range(b):
        xb = x[bi]
        for i in range(depth):
            xb = _layer(xb, p[i, bi], w_in[i], w_conv[i], w_up_a[i], w_out_b[i], w_o[i], norm_mix[i], norm_ffn[i],
                        w_group[i], b_group[i], w_router[i], b_router[i], w_gate[i], w_up[i], w_down[i],
                        norm_ple[i], w_ple_gate[i], w_ple_proj[i])
        outs.append(_rmsnorm(xb, norm_final, x.dtype, NORM_ROWS))
    return jnp.stack(outs)
```

```python
import functools

import jax
import jax.numpy as jnp
from jax import lax
from jax.experimental import pallas as pl
from jax.experimental.pallas import tpu as pltpu

F32 = jnp.float32
BF16 = jnp.bfloat16

HEAD_DIM = 128
ATTN_DILATIONS = (1, 4, 16)
HEADS_PER_GROUP = 4
ROPE_THETA = 10000.0
ATTN_BLOCK = 128
NEG_BIG = -1e30
N_EXPERT_GROUPS = 4
TOP_K = 2
EPS = 1e-6
ROW_BLOCK = 128
LANES = 128

VMEM_LIMIT_BYTES = 58 * 1024 * 1024


def _params(*sem):
    return pltpu.CompilerParams(dimension_semantics=sem, vmem_limit_bytes=VMEM_LIMIT_BYTES)


def _rmsnorm_kernel(x_ref, g_ref, o_ref):
    x = x_ref[...]
    ms = jnp.mean(x * x, axis=-1, keepdims=True)
    o_ref[...] = (x * lax.rsqrt(ms + EPS) * g_ref[...]).astype(o_ref.dtype)


def _rmsnorm(x, gain, out_dtype, tm):
    n, d = x.shape
    tm = min(tm, n)
    return pl.pallas_call(
        _rmsnorm_kernel,
        grid=(n // tm,),
        in_specs=[pl.BlockSpec((tm, d), lambda i: (i, 0)), pl.BlockSpec((1, d), lambda i: (0, 0))],
        out_specs=pl.BlockSpec((tm, d), lambda i: (i, 0)),
        out_shape=jax.ShapeDtypeStruct((n, d), out_dtype),
        compiler_params=_params("arbitrary"),
        name="rmsnorm",
    )(x, gain.reshape(1, d))


def _staged_weight(w_hbm, stage_ref, wb_ref, sem, tn, col0=0):
    j = pl.program_id(0)
    i = pl.program_id(1)

    def fetch(jj):
        cols = pl.ds(pl.multiple_of(col0 + jj * tn, LANES), tn)
        return pltpu.make_async_copy(w_hbm.at[:, cols], stage_ref, sem)

    @pl.when(i == 0)
    def _():
        @pl.when(j == 0)
        def _():
            fetch(0).start()

        fetch(j).wait()
        wb_ref[...] = stage_ref[...].astype(BF16)

        @pl.when(j + 1 < pl.num_programs(0))
        def _():
            fetch(j + 1).start()


def _qkv_proj_kernel(h_ref, c_ref, s_ref, w_hbm, o_ref, stage_ref, wb_ref, sem, *, tn, n_rope_heads):
    _staged_weight(w_hbm, stage_ref, wb_ref, sem, tn)
    acc = jnp.dot(h_ref[...], wb_ref[...], preferred_element_type=F32)
    heads = o_ref.shape[0]
    first_head = pl.program_id(0) * heads
    for hh in range(heads):
        t = acc[:, hh * HEAD_DIM:(hh + 1) * HEAD_DIM]
        rotary = first_head + hh < n_rope_heads
        c = jnp.where(rotary, c_ref[...], 1.0)
        s = jnp.where(rotary, s_ref[...], 0.0)
        o_ref[hh] = t * c + pltpu.roll(t, HEAD_DIM // 2, 1) * s


def _qkv_proj(h, w, rope_c, rope_s, n_cols, n_rope_cols, tm, tn):
    m, k = h.shape
    tm = min(tm, m)
    assert n_cols % tn == 0 and tn % HEAD_DIM == 0 and n_rope_cols % HEAD_DIM == 0 and m % tm == 0
    heads = tn // HEAD_DIM
    return pl.pallas_call(
        functools.partial(_qkv_proj_kernel, tn=tn, n_rope_heads=n_rope_cols // HEAD_DIM),
        grid=(n_cols // tn, m // tm),
        in_specs=[
            pl.BlockSpec((tm, k), lambda j, i: (i, 0)),
            pl.BlockSpec((tm, HEAD_DIM), lambda j, i: (i, 0)),
            pl.BlockSpec((tm, HEAD_DIM), lambda j, i: (i, 0)),
            pl.BlockSpec(memory_space=pl.ANY),
        ],
        out_specs=pl.BlockSpec((heads, tm, HEAD_DIM), lambda j, i: (j, i, 0)),
        out_shape=jax.ShapeDtypeStruct((n_cols // HEAD_DIM, m, HEAD_DIM), F32),
        scratch_shapes=[pltpu.VMEM((k, tn), F32), pltpu.VMEM((k, tn), BF16), pltpu.SemaphoreType.DMA],
        compiler_params=_params("arbitrary", "arbitrary"),
        name="qkv_proj_rope",
    )(h, rope_c, rope_s, w)


def _mm_kernel(a_ref, w_hbm, o_ref, stage_ref, wb_ref, sem, *, tn, col0):
    _staged_weight(w_hbm, stage_ref, wb_ref, sem, tn, col0)
    o_ref[...] = jnp.dot(a_ref[...], wb_ref[...], preferred_element_type=F32).astype(o_ref.dtype)


def _mm_cols(a, w, col0, tm, tn):
    m, k = a.shape
    n = w.shape[1] - col0
    tm = min(tm, m)
    while n % tn:
        tn //= 2
    assert tn % LANES == 0 and col0 % LANES == 0 and m % tm == 0
    return pl.pallas_call(
        functools.partial(_mm_kernel, tn=tn, col0=col0),
        grid=(n // tn, m // tm),
        in_specs=[
            pl.BlockSpec((tm, k), lambda j, i: (i, 0)),
            pl.BlockSpec(memory_space=pl.ANY),
        ],
        out_specs=pl.BlockSpec((tm, tn), lambda j, i: (i, j)),
        out_shape=jax.ShapeDtypeStruct((m, n), BF16),
        scratch_shapes=[pltpu.VMEM((k, tn), F32), pltpu.VMEM((k, tn), BF16), pltpu.SemaphoreType.DMA],
        compiler_params=_params("arbitrary", "arbitrary"),
        name="merge_gate_proj",
    )(a, w)


ATTN_SUPER = ATTN_BLOCK * max(ATTN_DILATIONS)
LOG2_E = 1.4426950408889634
LN_2 = 0.6931471805599453


def _rows(start, size, stride):
    return pl.ds(start, size) if stride == 1 else pl.ds(start, size, stride=stride)


def _dilated_attn_kernel(q0_ref, q1_ref, q2_ref, kc_ref, kp_ref, vc_ref, vp_ref, y_ref, o_ref, lse_ref):
    sb = pl.program_id(1)
    sup = ATTN_SUPER
    qi = lax.broadcasted_iota(jnp.int32, (ATTN_BLOCK, 2 * ATTN_BLOCK), 0)
    kj = lax.broadcasted_iota(jnp.int32, (ATTN_BLOCK, 2 * ATTN_BLOCK), 1)
    dist = qi + ATTN_BLOCK - kj
    band = (dist >= 0) & (dist <= ATTN_BLOCK)
    band_first = band & ((kj >= ATTN_BLOCK) | (sb > 0))
    q_scale = HEAD_DIM ** -0.5 * LOG2_E

    for g, (q_ref, d) in enumerate(zip((q0_ref, q1_ref, q2_ref), ATTN_DILATIONS)):
        span = ATTN_BLOCK * d
        for j in range(sup // span):
            for r in range(d):
                row0 = j * span + r
                rows = _rows(row0, ATTN_BLOCK, d)
                if j == 0:
                    prev_rows = _rows(sup - span + r, ATTN_BLOCK, d)
                    k_prev, v_prev = kp_ref[prev_rows, :], vp_ref[prev_rows, :]
                else:
                    prev_rows = _rows(row0 - span, ATTN_BLOCK, d)
                    k_prev, v_prev = kc_ref[prev_rows, :], vc_ref[prev_rows, :]
                q = (q_ref[rows, :] * q_scale).astype(BF16)
                k = jnp.concatenate([k_prev, kc_ref[rows, :]], axis=0).astype(BF16)
                v = jnp.concatenate([v_prev, vc_ref[rows, :]], axis=0).astype(BF16)
                s = lax.dot_general(q, k, (((1,), (1,)), ((), ())), preferred_element_type=F32)
                s = jnp.where(band_first if j == 0 else band, s, NEG_BIG)
                m = jnp.max(s, axis=-1, keepdims=True)
                e = jnp.exp2(s - m)
                ssum = jnp.sum(e, axis=-1, keepdims=True)
                p = (e * (1.0 / ssum)).astype(BF16)
                o_ref[g, rows, :] = jnp.dot(p, v, preferred_element_type=F32)
                lse = m * LN_2 + jnp.log(ssum)
                lse_ref[g, rows, :] = jnp.broadcast_to(lse, (ATTN_BLOCK, HEAD_DIM))

    lse = lse_ref[...]
    m = jnp.max(lse, axis=0)
    e = jnp.exp(lse - m[None])
    w = e / jnp.sum(e, axis=0)[None]
    y_ref[...] = jnp.sum(w * o_ref[...], axis=0).astype(y_ref.dtype)


def _dilated_attention(qkv):
    nh, s, hd = qkv.shape
    ng = len(ATTN_DILATIONS)
    hpg = HEADS_PER_GROUP
    assert nh == (ng + 2) * hpg and hd == HEAD_DIM and s % ATTN_SUPER == 0
    sup = ATTN_SUPER

    def q_spec(g):
        return pl.BlockSpec((None, sup, hd), lambda h, b: (g * hpg + h, b, 0))

    def kv_spec(first_head, prev):
        if prev:
            return pl.BlockSpec((None, sup, hd), lambda h, b: (first_head + h, jnp.maximum(b - 1, 0), 0))
        return pl.BlockSpec((None, sup, hd), lambda h, b: (first_head + h, b, 0))

    k0, v0 = ng * hpg, (ng + 1) * hpg
    return pl.pallas_call(
        _dilated_attn_kernel,
        grid=(hpg, s // sup),
        in_specs=[q_spec(0), q_spec(1), q_spec(2), kv_spec(k0, False), kv_spec(k0, True),
                  kv_spec(v0, False), kv_spec(v0, True)],
        out_specs=pl.BlockSpec((sup, hd), lambda h, b: (b, h)),
        out_shape=jax.ShapeDtypeStruct((s, hpg * hd), BF16),
        scratch_shapes=[pltpu.VMEM((ng, sup, hd), F32), pltpu.VMEM((ng, sup, hd), F32)],
        compiler_params=_params("arbitrary", "arbitrary"),
        name="dilated_attention",
    )(qkv, qkv, qkv, qkv, qkv, qkv, qkv)


CONV_HALO = 16


def _conv_proj_kernel(h_ref, wc_ref, w_hbm, o_ref, stage_ref, wb_ref, carry_ref, sem, *, tc, cols):
    j = pl.program_id(0)
    i = pl.program_id(1)

    def fetch(jj):
        return [pltpu.make_async_copy(w_hbm.at[:, pl.ds(pl.multiple_of(c0 + jj * tc, LANES), tc)],
                                      stage_ref.at[g], sem.at[g]) for g, c0 in enumerate(cols)]

    @pl.when(i == 0)
    def _():
        @pl.when(j == 0)
        def _():
            for c in fetch(0):
                c.start()

        for c in fetch(j):
            c.wait()
        for g in range(len(cols)):
            wb_ref[:, g * tc:(g + 1) * tc] = stage_ref[g].astype(BF16)
        carry_ref[...] = jnp.zeros_like(carry_ref)

        @pl.when(j + 1 < pl.num_programs(0))
        def _():
            for c in fetch(j + 1):
                c.start()

    acc = jnp.dot(h_ref[...], wb_ref[...], preferred_element_type=F32)
    u, gate_b, gate_c = acc[:, :tc], acc[:, tc:2 * tc], acc[:, 2 * tc:]
    cur = gate_c * u
    big = jnp.concatenate([carry_ref[...], cur], axis=0)
    back1 = pltpu.roll(big, 1, 0)[CONV_HALO:]
    back2 = pltpu.roll(big, 2, 0)[CONV_HALO:]
    w = wc_ref[...]
    y = w[0:1] * back2 + w[1:2] * back1 + w[2:3] * cur
    o_ref[...] = (gate_b * y).astype(o_ref.dtype)
    carry_ref[...] = cur[cur.shape[0] - CONV_HALO:, :]


def _conv_proj(h, w, w_conv, col_u, col_gb, col_gc, tm, tc):
    s, k = h.shape
    kw, cw = w_conv.shape
    assert kw == 3
    tm = min(tm, s)
    tc = min(tc, cw)
    assert cw % tc == 0 and tc % LANES == 0 and s % tm == 0 and tm % CONV_HALO == 0
    cols = (col_u, col_gb, col_gc)
    return pl.pallas_call(
        functools.partial(_conv_proj_kernel, tc=tc, cols=cols),
        grid=(cw // tc, s // tm),
        in_specs=[
            pl.BlockSpec((tm, k), lambda j, i: (i, 0)),
            pl.BlockSpec((kw, tc), lambda j, i: (0, j)),
            pl.BlockSpec(memory_space=pl.ANY),
        ],
        out_specs=pl.BlockSpec((tm, tc), lambda j, i: (i, j)),
        out_shape=jax.ShapeDtypeStruct((s, cw), BF16),
        scratch_shapes=[pltpu.VMEM((len(cols), k, tc), F32), pltpu.VMEM((k, len(cols) * tc), BF16),
                        pltpu.VMEM((CONV_HALO, tc), F32), pltpu.SemaphoreType.DMA((len(cols),))],
        compiler_params=_params("arbitrary", "arbitrary"),
        name="conv_proj",
    )(h, w_conv, w)


def _merge_kernel(ya_ref, yb_ref, wa_ref, wb_ref, ma_ref, mb_ref, o_ref, wab_ref, wbb_ref):
    i = pl.program_id(1)

    @pl.when(i == 0)
    def _():
        wab_ref[...] = wa_ref[...].astype(BF16)
        wbb_ref[...] = wb_ref[...].astype(BF16)

    a = jnp.dot(ya_ref[...], wab_ref[...], preferred_element_type=F32)
    b = jnp.dot(yb_ref[...], wbb_ref[...], preferred_element_type=F32)
    ga = jax.nn.sigmoid(ma_ref[...].astype(F32))
    gb = jax.nn.sigmoid(mb_ref[...].astype(F32))
    o_ref[...] = (ga * a + gb * b).astype(o_ref.dtype)


def _merge(y_a, y_b, proj, w_a, w_b, col_ma, col_mb, tm, tn):
    m, ka = y_a.shape
    kb = y_b.shape[1]
    n = w_a.shape[1]
    tm = min(tm, m)
    tn = min(tn, n)
    while col_ma % tn or col_mb % tn:
        tn //= 2
    assert tn % LANES == 0
    return pl.pallas_call(
        _merge_kernel,
        grid=(n // tn, m // tm),
        in_specs=[
            pl.BlockSpec((tm, ka), lambda j, i: (i, 0)),
            pl.BlockSpec((tm, kb), lambda j, i: (i, 0)),
            pl.BlockSpec((ka, tn), lambda j, i: (0, j)),
            pl.BlockSpec((kb, tn), lambda j, i: (0, j)),
            pl.BlockSpec((tm, tn), lambda j, i: (i, col_ma // tn + j)),
            pl.BlockSpec((tm, tn), lambda j, i: (i, col_mb // tn + j)),
        ],
        out_specs=pl.BlockSpec((tm, tn), lambda j, i: (i, j)),
        out_shape=jax.ShapeDtypeStruct((m, n), BF16),
        scratch_shapes=[pltpu.VMEM((ka, tn), BF16), pltpu.VMEM((kb, tn), BF16)],
        compiler_params=_params("arbitrary", "arbitrary"),
        name="mixer_merge",
    )(y_a, y_b, w_a, w_b, proj, proj)


def _resid_mm_kernel(a_ref, x_ref, w_hbm, o_ref, stage_ref, wb_ref, sem, *, tn):
    _staged_weight(w_hbm, stage_ref, wb_ref, sem, tn)
    o_ref[...] = x_ref[...] + jnp.dot(a_ref[...], wb_ref[...], preferred_element_type=F32)


def _resid_mm(a, w, x, tm, tn):
    m, k = a.shape
    n = w.shape[1]
    tm = min(tm, m)
    tn = min(tn, n)
    return pl.pallas_call(
        functools.partial(_resid_mm_kernel, tn=tn),
        grid=(n // tn, m // tm),
        in_specs=[
            pl.BlockSpec((tm, k), lambda j, i: (i, 0)),
            pl.BlockSpec((tm, tn), lambda j, i: (i, j)),
            pl.BlockSpec(memory_space=pl.ANY),
        ],
        out_specs=pl.BlockSpec((tm, tn), lambda j, i: (i, j)),
        out_shape=jax.ShapeDtypeStruct((m, n), F32),
        scratch_shapes=[pltpu.VMEM((k, tn), F32), pltpu.VMEM((k, tn), BF16), pltpu.SemaphoreType.DMA],
        compiler_params=_params("arbitrary", "arbitrary"),
        name="out_proj_residual",
    )(a, x, w)


def _split_bf16(a):
    hi = a.astype(BF16)
    lo = (a - hi.astype(F32)).astype(BF16)
    return hi, lo


def _pack_bf16_pair(a):
    c = a.shape[1] // 2
    bits = lax.bitcast_convert_type(a.astype(BF16).astype(F32), jnp.uint32)
    return (bits[:, :c] >> 16) | (bits[:, c:] & jnp.uint32(0xFFFF0000))


def _unpack_bf16_pair(w):
    lo = lax.bitcast_convert_type(w << 16, F32)
    hi = lax.bitcast_convert_type(w & jnp.uint32(0xFFFF0000), F32)
    return jnp.concatenate([lo, hi], axis=1)


def _router_kernel(x_ref, g_ref, w_ref, b_ref, h_ref, eid_ref, wt_ref, whl_ref, *, n_groups, n_experts):
    @pl.when(pl.program_id(0) == 0)
    def _():
        whl_ref[:, :LANES], whl_ref[:, LANES:] = _split_bf16(w_ref[...])

    x = x_ref[...]
    ms = jnp.mean(x * x, axis=-1, keepdims=True)
    h = x * lax.rsqrt(ms + EPS) * g_ref[...]
    h_ref[...] = _pack_bf16_pair(h)
    hh, hl = _split_bf16(h)
    both = jnp.dot(hh, whl_ref[...], preferred_element_type=F32)
    logits = (both[:, :LANES] + both[:, LANES:]
              + jnp.dot(hl, whl_ref[:, :LANES], preferred_element_type=F32)) + b_ref[...]
    tm = x.shape[0]
    per_group = n_experts // n_groups
    lane = lax.broadcasted_iota(jnp.int32, (tm, LANES), 1)
    is_g = lane < n_groups
    gl = jnp.where(is_g, logits, NEG_BIG)
    gmax = jnp.max(gl, axis=-1, keepdims=True)
    gidx = jnp.min(jnp.where(gl == gmax, lane, LANES), axis=-1, keepdims=True)
    gsum = jnp.sum(jnp.where(is_g, jnp.exp(gl - gmax), 0.0), axis=-1, keepdims=True)
    g_w = 1.0 / gsum
    lane_e = lane - n_groups
    sel = (lane_e >= 0) & (lane_e < n_experts) & (lane_e // per_group == gidx)
    el = jnp.where(sel, logits, NEG_BIG)
    emax = jnp.max(el, axis=-1, keepdims=True)
    i1 = jnp.min(jnp.where(el == emax, lane, LANES), axis=-1, keepdims=True)
    esum = jnp.sum(jnp.where(sel, jnp.exp(el - emax), 0.0), axis=-1, keepdims=True)
    el2 = jnp.where(lane == i1, NEG_BIG, el)
    emax2 = jnp.max(el2, axis=-1, keepdims=True)
    i2 = jnp.min(jnp.where(el2 == emax2, lane, LANES), axis=-1, keepdims=True)
    p1 = 1.0 / esum
    p2 = jnp.exp(emax2 - emax) / esum
    den = p1 + p2
    eid_ref[...] = jnp.where(lane == 0, i1 - n_groups, jnp.where(lane == 1, i2 - n_groups, 0))
    wt_ref[...] = jnp.where(lane == 0, g_w * (p1 / den), jnp.where(lane == 1, g_w * (p2 / den), 0.0))


def _router(x, gain, w_route, b_route, n_groups, n_experts, tm):
    t, d = x.shape
    tm = min(tm, t)
    row = pl.BlockSpec((tm, d), lambda i: (i, 0))
    narrow = pl.BlockSpec((tm, LANES), lambda i: (i, 0))
    return pl.pallas_call(
        functools.partial(_router_kernel, n_groups=n_groups, n_experts=n_experts),
        grid=(t // tm,),
        in_specs=[row, pl.BlockSpec((1, d), lambda i: (0, 0)), pl.BlockSpec((d, LANES), lambda i: (0, 0)),
                  pl.BlockSpec((1, LANES), lambda i: (0, 0))],
        out_specs=[pl.BlockSpec((tm, d // 2), lambda i: (i, 0)), narrow, narrow],
        out_shape=[jax.ShapeDtypeStruct((t, d // 2), jnp.uint32), jax.ShapeDtypeStruct((t, LANES), jnp.int32),
                   jax.ShapeDtypeStruct((t, LANES), F32)],
        scratch_shapes=[pltpu.VMEM((d, 2 * LANES), BF16)],
        compiler_params=_params("arbitrary"),
        name="ffn_norm_router",
    )(x, gain.reshape(1, d), w_route, b_route)


def _rank_kernel(eid_ref, rank_ref, cnt_ref, carry_ref):
    i = pl.program_id(0)

    @pl.when(i == 0)
    def _():
        carry_ref[...] = jnp.zeros_like(carry_ref)

    eid = eid_ref[...]
    tm = eid.shape[0]
    lane = lax.broadcasted_iota(jnp.int32, (tm, LANES), 1)
    oh0 = lane == eid[:, 0:1]
    oh1 = lane == eid[:, 1:2]
    ohs = oh0.astype(F32) + oh1.astype(F32)
    r = lax.broadcasted_iota(jnp.int32, (tm, tm), 0)
    c = lax.broadcasted_iota(jnp.int32, (tm, tm), 1)
    earlier = (c < r).astype(BF16)
    before = jnp.dot(earlier, ohs.astype(BF16), preferred_element_type=F32) + carry_ref[...]
    rank0 = jnp.sum(jnp.where(oh0, before, 0.0), axis=-1, keepdims=True)
    rank1 = jnp.sum(jnp.where(oh1, before, 0.0), axis=-1, keepdims=True)
    rank_ref[...] = jnp.where(lane == 0, rank0, jnp.where(lane == 1, rank1, 0.0)).astype(jnp.int32)
    total = carry_ref[...] + jnp.sum(ohs, axis=0, keepdims=True)
    carry_ref[...] = total
    cnt_ref[...] = jnp.broadcast_to(total, cnt_ref.shape).astype(jnp.int32)


def _rank(eid, tm):
    t = eid.shape[0]
    tm = min(tm, t)
    narrow = pl.BlockSpec((tm, LANES), lambda i: (i, 0))
    return pl.pallas_call(
        _rank_kernel,
        grid=(t // tm,),
        in_specs=[narrow],
        out_specs=[narrow, pl.BlockSpec((8, LANES), lambda i: (0, 0))],
        out_shape=[jax.ShapeDtypeStruct((t, LANES), jnp.int32), jax.ShapeDtypeStruct((8, LANES), jnp.int32)],
        scratch_shapes=[pltpu.VMEM((1, LANES), F32)],
        compiler_params=_params("arbitrary"),
        name="expert_rank",
    )(eid)


def _dest_kernel(eid_ref, rank_ref, ps_ref, o_ref):
    eid = eid_ref[...]
    lane = lax.broadcasted_iota(jnp.int32, eid.shape, 1)
    ps = ps_ref[...]
    s0 = jnp.sum(jnp.where(lane == eid[:, 0:1], ps, 0.0), axis=-1, keepdims=True)
    s1 = jnp.sum(jnp.where(lane == eid[:, 1:2], ps, 0.0), axis=-1, keepdims=True)
    start = jnp.where(lane == 0, s0, jnp.where(lane == 1, s1, 0.0))
    o_ref[...] = start.astype(jnp.int32) + rank_ref[...]


def _dest_rows(eid, rank, pstart, tm):
    t = eid.shape[0]
    tm = min(tm, t)
    ps = jnp.zeros((1, LANES), F32).at[0, :pstart.shape[0]].set(pstart.astype(F32))
    narrow = pl.BlockSpec((tm, LANES), lambda i: (i, 0))
    return pl.pallas_call(
        _dest_kernel,
        grid=(t // tm,),
        in_specs=[narrow, narrow, pl.BlockSpec((1, LANES), lambda i: (0, 0))],
        out_specs=narrow,
        out_shape=jax.ShapeDtypeStruct((t, LANES), jnp.int32),
        compiler_params=_params("arbitrary"),
        name="expert_dest_rows",
    )(eid, rank, ps)


GATHER_UNROLL = 8
GATHER_PRIORITY = 1


def _expert_ffn_kernel(te_ref, trb_ref, nsub_ref, ntot_ref, tok_ref, tok_next_ref, h_hbm, wg_ref, wu_ref, wd_ref,
                       y_ref, land_ref, hid_ref, sem, *, tm, n_up):
    t = pl.program_id(0)
    s = pl.program_id(1)
    n_tiles = pl.num_programs(0)
    nsub = nsub_ref[t]
    slot = t & 1

    def row_copy(tok, r, sl):
        return pltpu.make_async_copy(h_hbm.at[pl.ds(tok, 1)], land_ref.at[sl, pl.ds(r, 1)], sem.at[sl])

    def start_gather(toks_ref, blocks, sl):
        def issue(i, carry):
            for u in range(GATHER_UNROLL):
                r = i * GATHER_UNROLL + u
                row_copy(toks_ref[0, 0, r], r, sl).start(priority=GATHER_PRIORITY)
            return carry

        lax.fori_loop(0, blocks * (ROW_BLOCK // GATHER_UNROLL), issue, 0)

    @pl.when(nsub > 0)
    def _():
        @pl.when(s == 0)
        def _():
            @pl.when(t == 0)
            def _():
                start_gather(tok_ref, nsub, slot)

            def drain(i, carry):
                for u in range(GATHER_UNROLL):
                    row_copy(0, 0, slot).wait()
                return carry

            lax.fori_loop(0, nsub * (ROW_BLOCK // GATHER_UNROLL), drain, 0)

        @pl.when(s == 1)
        def _():
            nxt = jnp.minimum(t + 1, n_tiles - 1)
            n_next = jnp.where(t + 1 < n_tiles, nsub_ref[nxt], 0)
            start_gather(tok_next_ref, n_next, 1 - slot)

        for n in range(1, tm // ROW_BLOCK + 1):
            rows = n * ROW_BLOCK

            @pl.when((nsub == n) & (s < n_up))
            def _(rows=rows):
                x = _unpack_bf16_pair(land_ref[slot, pl.ds(0, rows), :])
                g = jnp.dot(x, wg_ref[...], preferred_element_type=F32)
                u = jnp.dot(x, wu_ref[...], preferred_element_type=F32)
                hid_ref[s, pl.ds(0, rows), :] = g * jax.nn.sigmoid(g) * u

            @pl.when((nsub == n) & (s >= n_up))
            def _(rows=rows):
                hid = jnp.concatenate([hid_ref[c, pl.ds(0, rows), :] for c in range(n_up)], axis=1)
                y = jnp.dot(hid, wd_ref[...], preferred_element_type=F32)
                y_ref[pl.ds(0, rows), :] = _pack_bf16_pair(y)
                if rows < tm:
                    y_ref[pl.ds(rows, tm - rows), :] = jnp.zeros((tm - rows, y_ref.shape[1]), y_ref.dtype)


def _expert_ffn(h, row_tok, w_gate, w_up, w_down, meta, n_tiles, tm, tf, tn):
    d = w_gate.shape[1]
    f = w_gate.shape[2]
    tf = min(tf, f)
    tn = min(tn, d)
    n_up = f // tf
    n_down = d // tn
    n_steps = n_up + n_down

    def step(t, s, ntot):
        return jnp.where(t < ntot[0], s, n_steps - 1)

    def up_idx(t, s, te, trb, ns, ntot):
        return (te[t], 0, jnp.minimum(step(t, s, ntot), n_up - 1))

    def down_col(t, s, ntot):
        return jnp.maximum(step(t, s, ntot) - n_up, 0)

    grid_spec = pltpu.PrefetchScalarGridSpec(
        num_scalar_prefetch=4,
        grid=(n_tiles, n_steps),
        in_specs=[
            pl.BlockSpec((1, 1, tm), lambda t, s, te, trb, ns, ntot: (trb[t], 0, 0), memory_space=pltpu.SMEM),
            pl.BlockSpec((1, 1, tm), lambda t, s, te, trb, ns, ntot: (trb[jnp.minimum(t + 1, n_tiles - 1)], 0, 0),
                         memory_space=pltpu.SMEM),
            pl.BlockSpec(memory_space=pl.ANY),
            pl.BlockSpec((None, d, tf), up_idx),
            pl.BlockSpec((None, d, tf), up_idx),
            pl.BlockSpec((None, f, tn), lambda t, s, te, trb, ns, ntot: (te[t], 0, down_col(t, s, ntot))),
        ],
        out_specs=pl.BlockSpec((tm, tn // 2), lambda t, s, te, trb, ns, ntot: (trb[t], down_col(t, s, ntot))),
        scratch_shapes=[
            pltpu.VMEM((2, tm, d // 2), jnp.uint32),
            pltpu.VMEM((n_up, tm, tf), F32),
            pltpu.SemaphoreType.DMA((2,)),
        ],
    )
    return pl.pallas_call(
        functools.partial(_expert_ffn_kernel, tm=tm, n_up=n_up),
        grid_spec=grid_spec,
        out_shape=jax.ShapeDtypeStruct((n_tiles * tm, d // 2), jnp.uint32),
        compiler_params=_params("arbitrary", "arbitrary"),
        name="expert_ffn",
    )(*meta, row_tok.reshape(n_tiles, 1, tm), row_tok.reshape(n_tiles, 1, tm), h, w_gate, w_up, w_down)


def _combine_kernel(dest_ref, dest_next_ref, wt_ref, x_ref, g_ref, y_hbm, x2_ref, h3_ref, buf_ref, sem, *, tb, chunk):
    i = pl.program_id(0)
    slot = i & 1

    def row_copy(src, k, r, sl):
        return pltpu.make_async_copy(y_hbm.at[pl.ds(src, 1)], buf_ref.at[sl, k, pl.ds(r, 1)], sem.at[sl])

    def start_gather(rows_ref, sl):
        def issue(r, carry):
            for k in range(TOP_K):
                row_copy(rows_ref[0, 0, TOP_K * r + k], k, r, sl).start(priority=GATHER_PRIORITY)
            return carry

        lax.fori_loop(0, tb, issue, 0, unroll=GATHER_UNROLL // TOP_K)

    @pl.when(i == 0)
    def _():
        start_gather(dest_ref, slot)

    @pl.when(i + 1 < pl.num_programs(0))
    def _():
        start_gather(dest_next_ref, 1 - slot)

    def drain(r, carry):
        for k in range(TOP_K):
            row_copy(0, k, r, slot).wait()
        return carry

    lax.fori_loop(0, tb, drain, 0, unroll=GATHER_UNROLL // TOP_K)

    def expert_rows(k):
        words = buf_ref[slot, k]
        half = chunk // 2
        return jnp.concatenate([_unpack_bf16_pair(words[:, c * half:(c + 1) * half])
                                for c in range(words.shape[1] // half)], axis=1)

    wt = wt_ref[...]
    x2 = x_ref[...] + (wt[:, 0:1] * expert_rows(0) + wt[:, 1:2] * expert_rows(1))
    x2_ref[...] = x2
    ms = jnp.mean(x2 * x2, axis=-1, keepdims=True)
    h3_ref[...] = (x2 * lax.rsqrt(ms + EPS) * g_ref[...]).astype(h3_ref.dtype)


def _combine(x, y, dest, wts, gain, tb, chunk):
    t, d = x.shape
    tb = min(tb, t)
    nb = t // tb
    dest3 = dest.reshape(nb, 1, TOP_K * tb)
    row = pl.BlockSpec((tb, d), lambda i: (i, 0))
    return pl.pallas_call(
        functools.partial(_combine_kernel, tb=tb, chunk=chunk),
        grid=(nb,),
        in_specs=[
            pl.BlockSpec((1, 1, TOP_K * tb), lambda i: (i, 0, 0), memory_space=pltpu.SMEM),
            pl.BlockSpec((1, 1, TOP_K * tb), lambda i: (jnp.minimum(i + 1, nb - 1), 0, 0), memory_space=pltpu.SMEM),
            pl.BlockSpec((tb, LANES), lambda i: (i, 0)),
            row,
            pl.BlockSpec((1, d), lambda i: (0, 0)),
            pl.BlockSpec(memory_space=pl.ANY),
        ],
        out_specs=[row, row],
        out_shape=[jax.ShapeDtypeStruct((t, d), F32), jax.ShapeDtypeStruct((t, d), BF16)],
        scratch_shapes=[pltpu.VMEM((2, TOP_K, tb, d // 2), jnp.uint32), pltpu.SemaphoreType.DMA((2,))],
        compiler_params=_params("arbitrary"),
        name="moe_combine_norm",
    )(dest3, dest3, wts, x, gain.reshape(1, d), y)


def _ple_kernel(h_ref, p_ref, wp_ref, x_ref, wg_hbm, o_ref, stage_ref, wgb_ref, wpb_ref, sem, *, tn):
    _staged_weight(wg_hbm, stage_ref, wgb_ref, sem, tn)

    @pl.when(pl.program_id(1) == 0)
    def _():
        wpb_ref[...] = wp_ref[...].astype(BF16)

    a = jnp.dot(h_ref[...], wgb_ref[...], preferred_element_type=F32)
    b = jnp.dot(p_ref[...].astype(BF16), wpb_ref[...], preferred_element_type=F32)
    o_ref[...] = x_ref[...] + jax.nn.sigmoid(a) * b


def _ple(h, p, w_gate, w_proj, x, tm, tn):
    m, k = h.shape
    kp = p.shape[1]
    n = w_gate.shape[1]
    tm = min(tm, m)
    tn = min(tn, n)
    return pl.pallas_call(
        functools.partial(_ple_kernel, tn=tn),
        grid=(n // tn, m // tm),
        in_specs=[
            pl.BlockSpec((tm, k), lambda j, i: (i, 0)),
            pl.BlockSpec((tm, kp), lambda j, i: (i, 0)),
            pl.BlockSpec((kp, tn), lambda j, i: (0, j)),
            pl.BlockSpec((tm, tn), lambda j, i: (i, j)),
            pl.BlockSpec(memory_space=pl.ANY),
        ],
        out_specs=pl.BlockSpec((tm, tn), lambda j, i: (i, j)),
        out_shape=jax.ShapeDtypeStruct((m, n), F32),
        scratch_shapes=[pltpu.VMEM((k, tn), F32), pltpu.VMEM((k, tn), BF16), pltpu.VMEM((kp, tn), BF16),
                        pltpu.SemaphoreType.DMA],
        compiler_params=_params("arbitrary", "arbitrary"),
        name="ple_gate_residual",
    )(h, p, w_proj, x, w_gate)


EXPERT_TILE_ROWS = 640

NORM_ROWS = 512
QKV_TILE_ROWS = 512
CONV_PROJ_TILE = (1024, 256)
GATE_PROJ_TILE = (1024, 1024)
MERGE_TILE = (1024, 1024)
OUT_PROJ_TILE = (512, 1024)
PLE_TILE = (512, 1024)
ROUTER_ROWS = 256
RANK_ROWS = 1024
FFN_UP_COLS = 256
FFN_DOWN_COLS = 1024
COMBINE_ROWS = 256


def _rope_tables(s):
    half = HEAD_DIM // 2
    inv_freq = ROPE_THETA ** (-jnp.arange(half, dtype=F32) / half)
    ang = jnp.arange(s, dtype=jnp.int32).astype(F32)[:, None] * inv_freq[None, :]
    cos, sin = jnp.cos(ang), jnp.sin(ang)
    return jnp.concatenate([cos, cos], axis=-1), jnp.concatenate([-sin, sin], axis=-1)


def _expert_tiles(counts, n_assign, tm):
    n_experts = counts.shape[0]
    n_tiles = -(-n_assign // tm) + n_experts
    tiles_e = (counts + tm - 1) // tm
    tcum = jnp.cumsum(tiles_e)
    tstart = tcum - tiles_e
    pstart = tstart * tm
    total = tcum[-1]
    t = jnp.arange(n_tiles, dtype=jnp.int32)
    tc = jnp.minimum(t, total - 1)
    te = jnp.minimum(jnp.searchsorted(tcum, tc, side="right"), n_experts - 1).astype(jnp.int32)
    local = tc - tstart[te]
    rows_left = counts[te] - local * tm
    nsub = jnp.clip((rows_left + ROW_BLOCK - 1) // ROW_BLOCK, 0, tm // ROW_BLOCK)
    nsub = jnp.where(t < total, nsub, 0).astype(jnp.int32)
    trb = (tstart[te] + local).astype(jnp.int32)
    meta = (te, trb, nsub, jnp.reshape(total, (1,)).astype(jnp.int32))
    return pstart.astype(jnp.int32), meta, n_tiles


def _layer(x, p, w_in, w_conv, w_up_a, w_out_b, w_o, g_mix, g_ffn, w_group, b_group, w_router, b_router,
           w_gate, w_up, w_down, g_ple, w_ple_gate, w_ple_proj):
    s, d = x.shape
    cw = w_conv.shape[1]
    n_experts = w_gate.shape[0]
    ng = len(ATTN_DILATIONS)
    a_out = HEADS_PER_GROUP * HEAD_DIM
    q_cols = ng * a_out
    qkv_cols = q_cols + 2 * a_out
    col_u, col_gb, col_gc = qkv_cols, qkv_cols + cw, qkv_cols + 2 * cw
    col_gates = qkv_cols + 3 * cw

    h = _rmsnorm(x, g_mix, BF16, NORM_ROWS)
    rope_c, rope_s = _rope_tables(s)
    qkv = _qkv_proj(h, w_in, rope_c, rope_s, qkv_cols, q_cols + a_out, QKV_TILE_ROWS, qkv_cols // 2)
    y_a = _dilated_attention(qkv)
    y_b = _conv_proj(h, w_in, w_conv, col_u, col_gb, col_gc, *CONV_PROJ_TILE)
    gates = _mm_cols(h, w_in, col_gates, *GATE_PROJ_TILE)
    merged = _merge(y_a, y_b, gates, w_up_a, w_out_b, 0, d, *MERGE_TILE)
    x1 = _resid_mm(merged, w_o, x, *OUT_PROJ_TILE)

    n_route = N_EXPERT_GROUPS + n_experts
    w_route = jnp.pad(jnp.concatenate([w_group, w_router], axis=1), ((0, 0), (0, LANES - n_route)))
    b_route = jnp.pad(jnp.concatenate([b_group, b_router]), (0, LANES - n_route)).reshape(1, LANES)
    h2, eid, wts = _router(x1, g_ffn, w_route, b_route, N_EXPERT_GROUPS, n_experts, ROUTER_ROWS)
    rank, cnt = _rank(eid, RANK_ROWS)
    counts = cnt[0, :n_experts]
    tm_e = EXPERT_TILE_ROWS
    pstart, meta, n_tiles = _expert_tiles(counts, s * TOP_K, tm_e)
    dest = _dest_rows(eid, rank, pstart, RANK_ROWS)[:, :TOP_K].reshape(-1)
    tokens = jnp.repeat(jnp.arange(s, dtype=jnp.int32), TOP_K)
    row_tok = jnp.zeros((n_tiles * tm_e,), jnp.int32).at[dest].set(tokens, unique_indices=True)
    down_cols = min(FFN_DOWN_COLS, d)
    y = _expert_ffn(h2, row_tok, w_gate, w_up, w_down, meta, n_tiles, tm_e, FFN_UP_COLS, down_cols)
    x2, h3 = _combine(x1, y, dest, wts, g_ple, COMBINE_ROWS, down_cols)

    return _ple(h3, p, w_ple_gate, w_ple_proj, x2, *PLE_TILE)


def kernel(x, p, w_in, w_conv, w_up_a, w_out_b, w_o, norm_mix, norm_ffn, w_group, b_group, w_router, b_router,
           w_gate, w_up, w_down, norm_ple, w_ple_gate, w_ple_proj, norm_final):
    b, s, d = x.shape
    depth = w_in.shape[0]
    outs = []
    for bi in range(b):
        xb = x[bi]
        for i in range(depth):
            xb = _layer(xb, p[i, bi], w_in[i], w_conv[i], w_up_a[i], w_out_b[i], w_o[i], norm_mix[i], norm_ffn[i],
                        w_group[i], b_group[i], w_router[i], b_router[i], w_gate[i], w_up[i], w_down[i],
                        norm_ple[i], w_ple_gate[i], w_ple_proj[i])
        outs.append(_rmsnorm(xb, norm_final, x.dtype, NORM_ROWS))
    return jnp.stack(outs)
```

```python
import functools

import jax
import jax.numpy as jnp
from jax import lax
from jax.experimental import pallas as pl
from jax.experimental.pallas import tpu as pltpu

F32 = jnp.float32
BF16 = jnp.bfloat16

HEAD_DIM = 128
ATTN_DILATIONS = (1, 4, 16)
HEADS_PER_GROUP = 4
ROPE_THETA = 10000.0
ATTN_BLOCK = 128
NEG_BIG = -1e30
N_EXPERT_GROUPS = 4
TOP_K = 2
EPS = 1e-6
ROW_BLOCK = 128
LANES = 128

VMEM_LIMIT_BYTES = 58 * 1024 * 1024


def _params(*sem):
    return pltpu.CompilerParams(dimension_semantics=sem, vmem_limit_bytes=VMEM_LIMIT_BYTES)


def _rmsnorm_kernel(x_ref, g_ref, o_ref):
    x = x_ref[...]
    ms = jnp.mean(x * x, axis=-1, keepdims=True)
    o_ref[...] = (x * lax.rsqrt(ms + EPS) * g_ref[...]).astype(o_ref.dtype)


def _rmsnorm(x, gain, out_dtype, tm):
    n, d = x.shape
    tm = min(tm, n)
    return pl.pallas_call(
        _rmsnorm_kernel,
        grid=(n // tm,),
        in_specs=[pl.BlockSpec((tm, d), lambda i: (i, 0)), pl.BlockSpec((1, d), lambda i: (0, 0))],
        out_specs=pl.BlockSpec((tm, d), lambda i: (i, 0)),
        out_shape=jax.ShapeDtypeStruct((n, d), out_dtype),
        compiler_params=_params("arbitrary"),
        name="rmsnorm",
    )(x, gain.reshape(1, d))


def _staged_weight(w_hbm, stage_ref, wb_ref, sem, tn, col0=0):
    j = pl.program_id(0)
    i = pl.program_id(1)

    def fetch(jj):
        cols = pl.ds(pl.multiple_of(col0 + jj * tn, LANES), tn)
        return pltpu.make_async_copy(w_hbm.at[:, cols], stage_ref, sem)

    @pl.when(i == 0)
    def _():
        @pl.when(j == 0)
        def _():
            fetch(0).start()

        fetch(j).wait()
        wb_ref[...] = stage_ref[...].astype(BF16)

        @pl.when(j + 1 < pl.num_programs(0))
        def _():
            fetch(j + 1).start()


def _qkv_proj_kernel(h_ref, c_ref, s_ref, w_hbm, o_ref, stage_ref, wb_ref, sem, *, tn, n_rope_heads):
    _staged_weight(w_hbm, stage_ref, wb_ref, sem, tn)
    acc = jnp.dot(h_ref[...], wb_ref[...], preferred_element_type=F32)
    heads = o_ref.shape[0]
    first_head = pl.program_id(0) * heads
    for hh in range(heads):
        t = acc[:, hh * HEAD_DIM:(hh + 1) * HEAD_DIM]
        rotary = first_head + hh < n_rope_heads
        c = jnp.where(rotary, c_ref[...], 1.0)
        s = jnp.where(rotary, s_ref[...], 0.0)
        o_ref[hh] = t * c + pltpu.roll(t, HEAD_DIM // 2, 1) * s


def _qkv_proj(h, w, rope_c, rope_s, n_cols, n_rope_cols, tm, tn):
    m, k = h.shape
    tm = min(tm, m)
    assert n_cols % tn == 0 and tn % HEAD_DIM == 0 and n_rope_cols % HEAD_DIM == 0 and m % tm == 0
    heads = tn // HEAD_DIM
    return pl.pallas_call(
        functools.partial(_qkv_proj_kernel, tn=tn, n_rope_heads=n_rope_cols // HEAD_DIM),
        grid=(n_cols // tn, m // tm),
        in_specs=[
            pl.BlockSpec((tm, k), lambda j, i: (i, 0)),
            pl.BlockSpec((tm, HEAD_DIM), lambda j, i: (i, 0)),
            pl.BlockSpec((tm, HEAD_DIM), lambda j, i: (i, 0)),
            pl.BlockSpec(memory_space=pl.ANY),
        ],
        out_specs=pl.BlockSpec((heads, tm, HEAD_DIM), lambda j, i: (j, i, 0)),
        out_shape=jax.ShapeDtypeStruct((n_cols // HEAD_DIM, m, HEAD_DIM), F32),
        scratch_shapes=[pltpu.VMEM((k, tn), F32), pltpu.VMEM((k, tn), BF16), pltpu.SemaphoreType.DMA],
        compiler_params=_params("arbitrary", "arbitrary"),
        name="qkv_proj_rope",
    )(h, rope_c, rope_s, w)


def _mm_kernel(a_ref, w_hbm, o_ref, stage_ref, wb_ref, sem, *, tn, col0):
    _staged_weight(w_hbm, stage_ref, wb_ref, sem, tn, col0)
    o_ref[...] = jnp.dot(a_ref[...], wb_ref[...], preferred_element_type=F32).astype(o_ref.dtype)


def _mm_cols(a, w, col0, tm, tn):
    m, k = a.shape
    n = w.shape[1] - col0
    tm = min(tm, m)
    while n % tn:
        tn //= 2
    assert tn % LANES == 0 and col0 % LANES == 0 and m % tm == 0
    return pl.pallas_call(
        functools.partial(_mm_kernel, tn=tn, col0=col0),
        grid=(n // tn, m // tm),
        in_specs=[
            pl.BlockSpec((tm, k), lambda j, i: (i, 0)),
            pl.BlockSpec(memory_space=pl.ANY),
        ],
        out_specs=pl.BlockSpec((tm, tn), lambda j, i: (i, j)),
        out_shape=jax.ShapeDtypeStruct((m, n), BF16),
        scratch_shapes=[pltpu.VMEM((k, tn), F32), pltpu.VMEM((k, tn), BF16), pltpu.SemaphoreType.DMA],
        compiler_params=_params("arbitrary", "arbitrary"),
        name="merge_gate_proj",
    )(a, w)


ATTN_SUPER = ATTN_BLOCK * max(ATTN_DILATIONS)
LOG2_E = 1.4426950408889634
LN_2 = 0.6931471805599453


def _rows(start, size, stride):
    return pl.ds(start, size) if stride == 1 else pl.ds(start, size, stride=stride)


def _dilated_attn_kernel(q0_ref, q1_ref, q2_ref, kc_ref, kp_ref, vc_ref, vp_ref, y_ref, o_ref, lse_ref):
    sb = pl.program_id(1)
    sup = ATTN_SUPER
    qi = lax.broadcasted_iota(jnp.int32, (ATTN_BLOCK, 2 * ATTN_BLOCK), 0)
    kj = lax.broadcasted_iota(jnp.int32, (ATTN_BLOCK, 2 * ATTN_BLOCK), 1)
    dist = qi + ATTN_BLOCK - kj
    band = (dist >= 0) & (dist <= ATTN_BLOCK)
    band_first = band & ((kj >= ATTN_BLOCK) | (sb > 0))
    q_scale = HEAD_DIM ** -0.5 * LOG2_E

    for g, (q_ref, d) in enumerate(zip((q0_ref, q1_ref, q2_ref), ATTN_DILATIONS)):
        span = ATTN_BLOCK * d
        for j in range(sup // span):
            for r in range(d):
                row0 = j * span + r
                rows = _rows(row0, ATTN_BLOCK, d)
                if j == 0:
                    prev_rows = _rows(sup - span + r, ATTN_BLOCK, d)
                    k_prev, v_prev = kp_ref[prev_rows, :], vp_ref[prev_rows, :]
                else:
                    prev_rows = _rows(row0 - span, ATTN_BLOCK, d)
                    k_prev, v_prev = kc_ref[prev_rows, :], vc_ref[prev_rows, :]
                q = (q_ref[rows, :] * q_scale).astype(BF16)
                k = jnp.concatenate([k_prev, kc_ref[rows, :]], axis=0).astype(BF16)
                v = jnp.concatenate([v_prev, vc_ref[rows, :]], axis=0).astype(BF16)
                s = lax.dot_general(q, k, (((1,), (1,)), ((), ())), preferred_element_type=F32)
                s = jnp.where(band_first if j == 0 else band, s, NEG_BIG)
                m = jnp.max(s, axis=-1, keepdims=True)
                e = jnp.exp2(s - m)
                ssum = jnp.sum(e, axis=-1, keepdims=True)
                p = (e * (1.0 / ssum)).astype(BF16)
                o_ref[g, rows, :] = jnp.dot(p, v, preferred_element_type=F32)
                lse = m * LN_2 + jnp.log(ssum)
                lse_ref[g, rows, :] = jnp.broadcast_to(lse, (ATTN_BLOCK, HEAD_DIM))

    lse = lse_ref[...]
    m = jnp.max(lse, axis=0)
    e = jnp.exp(lse - m[None])
    w = e / jnp.sum(e, axis=0)[None]
    y_ref[...] = jnp.sum(w * o_ref[...], axis=0).astype(y_ref.dtype)


def _dilated_attention(qkv):
    nh, s, hd = qkv.shape
    ng = len(ATTN_DILATIONS)
    hpg = HEADS_PER_GROUP
    assert nh == (ng + 2) * hpg and hd == HEAD_DIM and s % ATTN_SUPER == 0
    sup = ATTN_SUPER

    def q_spec(g):
        return pl.BlockSpec((None, sup, hd), lambda h, b: (g * hpg + h, b, 0))

    def kv_spec(first_head, prev):
        if prev:
            return pl.BlockSpec((None, sup, hd), lambda h, b: (first_head + h, jnp.maximum(b - 1, 0), 0))
        return pl.BlockSpec((None, sup, hd), lambda h, b: (first_head + h, b, 0))

    k0, v0 = ng * hpg, (ng + 1) * hpg
    return pl.pallas_call(
        _dilated_attn_kernel,
        grid=(hpg, s // sup),
        in_specs=[q_spec(0), q_spec(1), q_spec(2), kv_spec(k0, False), kv_spec(k0, True),
                  kv_spec(v0, False), kv_spec(v0, True)],
        out_specs=pl.BlockSpec((sup, hd), lambda h, b: (b, h)),
        out_shape=jax.ShapeDtypeStruct((s, hpg * hd), BF16),
        scratch_shapes=[pltpu.VMEM((ng, sup, hd), F32), pltpu.VMEM((ng, sup, hd), F32)],
        compiler_params=_params("arbitrary", "arbitrary"),
        name="dilated_attention",
    )(qkv, qkv, qkv, qkv, qkv, qkv, qkv)


CONV_HALO = 16


def _conv_proj_kernel(h_ref, wc_ref, w_hbm, o_ref, stage_ref, wb_ref, carry_ref, sem, *, tc, cols):
    j = pl.program_id(0)
    i = pl.program_id(1)

    def fetch(jj):
        return [pltpu.make_async_copy(w_hbm.at[:, pl.ds(pl.multiple_of(c0 + jj * tc, LANES), tc)],
                                      stage_ref.at[g], sem.at[g]) for g, c0 in enumerate(cols)]

    @pl.when(i == 0)
    def _():
        @pl.when(j == 0)
        def _():
            for c in fetch(0):
                c.start()

        for c in fetch(j):
            c.wait()
        for g in range(len(cols)):
            wb_ref[:, g * tc:(g + 1) * tc] = stage_ref[g].astype(BF16)
        carry_ref[...] = jnp.zeros_like(carry_ref)

        @pl.when(j + 1 < pl.num_programs(0))
        def _():
            for c in fetch(j + 1):
                c.start()

    acc = jnp.dot(h_ref[...], wb_ref[...], preferred_element_type=F32)
    u, gate_b, gate_c = acc[:, :tc], acc[:, tc:2 * tc], acc[:, 2 * tc:]
    cur = gate_c * u
    big = jnp.concatenate([carry_ref[...], cur], axis=0)
    back1 = pltpu.roll(big, 1, 0)[CONV_HALO:]
    back2 = pltpu.roll(big, 2, 0)[CONV_HALO:]
    w = wc_ref[...]
    y = w[0:1] * back2 + w[1:2] * back1 + w[2:3] * cur
    o_ref[...] = (gate_b * y).astype(o_ref.dtype)
    carry_ref[...] = cur[cur.shape[0] - CONV_HALO:, :]


def _conv_proj(h, w, w_conv, col_u, col_gb, col_gc, tm, tc):
    s, k = h.shape
    kw, cw = w_conv.shape
    assert kw == 3
    tm = min(tm, s)
    tc = min(tc, cw)
    assert cw % tc == 0 and tc % LANES == 0 and s % tm == 0 and tm % CONV_HALO == 0
    cols = (col_u, col_gb, col_gc)
    return pl.pallas_call(
        functools.partial(_conv_proj_kernel, tc=tc, cols=cols),
        grid=(cw // tc, s // tm),
        in_specs=[
            pl.BlockSpec((tm, k), lambda j, i: (i, 0)),
            pl.BlockSpec((kw, tc), lambda j, i: (0, j)),
            pl.BlockSpec(memory_space=pl.ANY),
        ],
        out_specs=pl.BlockSpec((tm, tc), lambda j, i: (i, j)),
        out_shape=jax.ShapeDtypeStruct((s, cw), BF16),
        scratch_shapes=[pltpu.VMEM((len(cols), k, tc), F32), pltpu.VMEM((k, len(cols) * tc), BF16),
                        pltpu.VMEM((CONV_HALO, tc), F32), pltpu.SemaphoreType.DMA((len(cols),))],
        compiler_params=_params("arbitrary", "arbitrary"),
        name="conv_proj",
    )(h, w_conv, w)


def _merge_kernel(ya_ref, yb_ref, wa_ref, wb_ref, ma_ref, mb_ref, o_ref, wab_ref, wbb_ref):
    i = pl.program_id(1)

    @pl.when(i == 0)
    def _():
        wab_ref[...] = wa_ref[...].astype(BF16)
        wbb_ref[...] = wb_ref[...].astype(BF16)

    a = jnp.dot(ya_ref[...], wab_ref[...], preferred_element_type=F32)
    b = jnp.dot(yb_ref[...], wbb_ref[...], preferred_element_type=F32)
    ga = jax.nn.sigmoid(ma_ref[...].astype(F32))
    gb = jax.nn.sigmoid(mb_ref[...].astype(F32))
    o_ref[...] = (ga * a + gb * b).astype(o_ref.dtype)


def _merge(y_a, y_b, proj, w_a, w_b, col_ma, col_mb, tm, tn):
    m, ka = y_a.shape
    kb = y_b.shape[1]
    n = w_a.shape[1]
    tm = min(tm, m)
    tn = min(tn, n)
    while col_ma % tn or col_mb % tn:
        tn //= 2
    assert tn % LANES == 0
    return pl.pallas_call(
        _merge_kernel,
        grid=(n // tn, m // tm),
        in_specs=[
            pl.BlockSpec((tm, ka), lambda j, i: (i, 0)),
            pl.BlockSpec((tm, kb), lambda j, i: (i, 0)),
            pl.BlockSpec((ka, tn), lambda j, i: (0, j)),
            pl.BlockSpec((kb, tn), lambda j, i: (0, j)),
            pl.BlockSpec((tm, tn), lambda j, i: (i, col_ma // tn + j)),
            pl.BlockSpec((tm, tn), lambda j, i: (i, col_mb // tn + j)),
        ],
        out_specs=pl.BlockSpec((tm, tn), lambda j, i: (i, j)),
        out_shape=jax.ShapeDtypeStruct((m, n), BF16),
        scratch_shapes=[pltpu.VMEM((ka, tn), BF16), pltpu.VMEM((kb, tn), BF16)],
        compiler_params=_params("arbitrary", "arbitrary"),
        name="mixer_merge",
    )(y_a, y_b, w_a, w_b, proj, proj)


def _resid_mm_kernel(a_ref, x_ref, w_hbm, o_ref, stage_ref, wb_ref, sem, *, tn):
    _staged_weight(w_hbm, stage_ref, wb_ref, sem, tn)
    o_ref[...] = x_ref[...] + jnp.dot(a_ref[...], wb_ref[...], preferred_element_type=F32)


def _resid_mm(a, w, x, tm, tn):
    m, k = a.shape
    n = w.shape[1]
    tm = min(tm, m)
    tn = min(tn, n)
    return pl.pallas_call(
        functools.partial(_resid_mm_kernel, tn=tn),
        grid=(n // tn, m // tm),
        in_specs=[
            pl.BlockSpec((tm, k), lambda j, i: (i, 0)),
            pl.BlockSpec((tm, tn), lambda j, i: (i, j)),
            pl.BlockSpec(memory_space=pl.ANY),
        ],
        out_specs=pl.BlockSpec((tm, tn), lambda j, i: (i, j)),
        out_shape=jax.ShapeDtypeStruct((m, n), F32),
        scratch_shapes=[pltpu.VMEM((k, tn), F32), pltpu.VMEM((k, tn), BF16), pltpu.SemaphoreType.DMA],
        compiler_params=_params("arbitrary", "arbitrary"),
        name="out_proj_residual",
    )(a, x, w)


def _split_bf16(a):
    hi = a.astype(BF16)
    lo = (a - hi.astype(F32)).astype(BF16)
    return hi, lo


def _pack_bf16_pair(a):
    c = a.shape[1] // 2
    bits = lax.bitcast_convert_type(a.astype(BF16).astype(F32), jnp.uint32)
    return (bits[:, :c] >> 16) | (bits[:, c:] & jnp.uint32(0xFFFF0000))


def _unpack_bf16_pair(w):
    lo = lax.bitcast_convert_type(w << 16, F32)
    hi = lax.bitcast_convert_type(w & jnp.uint32(0xFFFF0000), F32)
    return jnp.concatenate([lo, hi], axis=1)


def _router_kernel(x_ref, g_ref, w_ref, b_ref, h_ref, eid_ref, wt_ref, whl_ref, *, n_groups, n_experts):
    @pl.when(pl.program_id(0) == 0)
    def _():
        whl_ref[:, :LANES], whl_ref[:, LANES:] = _split_bf16(w_ref[...])

    x = x_ref[...]
    ms = jnp.mean(x * x, axis=-1, keepdims=True)
    h = x * lax.rsqrt(ms + EPS) * g_ref[...]
    h_ref[...] = _pack_bf16_pair(h)
    hh, hl = _split_bf16(h)
    both = jnp.dot(hh, whl_ref[...], preferred_element_type=F32)
    logits = (both[:, :LANES] + both[:, LANES:]
              + jnp.dot(hl, whl_ref[:, :LANES], preferred_element_type=F32)) + b_ref[...]
    tm = x.shape[0]
    per_group = n_experts // n_groups
    lane = lax.broadcasted_iota(jnp.int32, (tm, LANES), 1)
    is_g = lane < n_groups
    gl = jnp.where(is_g, logits, NEG_BIG)
    gmax = jnp.max(gl, axis=-1, keepdims=True)
    gidx = jnp.min(jnp.where(gl == gmax, lane, LANES), axis=-1, keepdims=True)
    gsum = jnp.sum(jnp.where(is_g, jnp.exp(gl - gmax), 0.0), axis=-1, keepdims=True)
    g_w = 1.0 / gsum
    lane_e = lane - n_groups
    sel = (lane_e >= 0) & (lane_e < n_experts) & (lane_e // per_group == gidx)
    el = jnp.where(sel, logits, NEG_BIG)
    emax = jnp.max(el, axis=-1, keepdims=True)
    i1 = jnp.min(jnp.where(el == emax, lane, LANES), axis=-1, keepdims=True)
    esum = jnp.sum(jnp.where(sel, jnp.exp(el - emax), 0.0), axis=-1, keepdims=True)
    el2 = jnp.where(lane == i1, NEG_BIG, el)
    emax2 = jnp.max(el2, axis=-1, keepdims=True)
    i2 = jnp.min(jnp.where(el2 == emax2, lane, LANES), axis=-1, keepdims=True)
    p1 = 1.0 / esum
    p2 = jnp.exp(emax2 - emax) / esum
    den = p1 + p2
    eid_ref[...] = jnp.where(lane == 0, i1 - n_groups, jnp.where(lane == 1, i2 - n_groups, 0))
    wt_ref[...] = jnp.where(lane == 0, g_w * (p1 / den), jnp.where(lane == 1, g_w * (p2 / den), 0.0))


def _router(x, gain, w_route, b_route, n_groups, n_experts, tm):
    t, d = x.shape
    tm = min(tm, t)
    row = pl.BlockSpec((tm, d), lambda i: (i, 0))
    narrow = pl.BlockSpec((tm, LANES), lambda i: (i, 0))
    return pl.pallas_call(
        functools.partial(_router_kernel, n_groups=n_groups, n_experts=n_experts),
        grid=(t // tm,),
        in_specs=[row, pl.BlockSpec((1, d), lambda i: (0, 0)), pl.BlockSpec((d, LANES), lambda i: (0, 0)),
                  pl.BlockSpec((1, LANES), lambda i: (0, 0))],
        out_specs=[pl.BlockSpec((tm, d // 2), lambda i: (i, 0)), narrow, narrow],
        out_shape=[jax.ShapeDtypeStruct((t, d // 2), jnp.uint32), jax.ShapeDtypeStruct((t, LANES), jnp.int32),
                   jax.ShapeDtypeStruct((t, LANES), F32)],
        scratch_shapes=[pltpu.VMEM((d, 2 * LANES), BF16)],
        compiler_params=_params("arbitrary"),
        name="ffn_norm_router",
    )(x, gain.reshape(1, d), w_route, b_route)


def _rank_kernel(eid_ref, rank_ref, cnt_ref, carry_ref):
    i = pl.program_id(0)

    @pl.when(i == 0)
    def _():
        carry_ref[...] = jnp.zeros_like(carry_ref)

    eid = eid_ref[...]
    tm = eid.shape[0]
    lane = lax.broadcasted_iota(jnp.int32, (tm, LANES), 1)
    oh0 = lane == eid[:, 0:1]
    oh1 = lane == eid[:, 1:2]
    ohs = oh0.astype(F32) + oh1.astype(F32)
    r = lax.broadcasted_iota(jnp.int32, (tm, tm), 0)
    c = lax.broadcasted_iota(jnp.int32, (tm, tm), 1)
    earlier = (c < r).astype(BF16)
    before = jnp.dot(earlier, ohs.astype(BF16), preferred_element_type=F32) + carry_ref[...]
    rank0 = jnp.sum(jnp.where(oh0, before, 0.0), axis=-1, keepdims=True)
    rank1 = jnp.sum(jnp.where(oh1, before, 0.0), axis=-1, keepdims=True)
    rank_ref[...] = jnp.where(lane == 0, rank0, jnp.where(lane == 1, rank1, 0.0)).astype(jnp.int32)
    total = carry_ref[...] + jnp.sum(ohs, axis=0, keepdims=True)
    carry_ref[...] = total
    cnt_ref[...] = jnp.broadcast_to(total, cnt_ref.shape).astype(jnp.int32)


def _rank(eid, tm):
    t = eid.shape[0]
    tm = min(tm, t)
    narrow = pl.BlockSpec((tm, LANES), lambda i: (i, 0))
    return pl.pallas_call(
        _rank_kernel,
        grid=(t // tm,),
        in_specs=[narrow],
        out_specs=[narrow, pl.BlockSpec((8, LANES), lambda i: (0, 0))],
        out_shape=[jax.ShapeDtypeStruct((t, LANES), jnp.int32), jax.ShapeDtypeStruct((8, LANES), jnp.int32)],
        scratch_shapes=[pltpu.VMEM((1, LANES), F32)],
        compiler_params=_params("arbitrary"),
        name="expert_rank",
    )(eid)


def _dest_kernel(eid_ref, rank_ref, ps_ref, o_ref):
    eid = eid_ref[...]
    lane = lax.broadcasted_iota(jnp.int32, eid.shape, 1)
    ps = ps_ref[...]
    s0 = jnp.sum(jnp.where(lane == eid[:, 0:1], ps, 0.0), axis=-1, keepdims=True)
    s1 = jnp.sum(jnp.where(lane == eid[:, 1:2], ps, 0.0), axis=-1, keepdims=True)
    start = jnp.where(lane == 0, s0, jnp.where(lane == 1, s1, 0.0))
    o_ref[...] = start.astype(jnp.int32) + rank_ref[...]


def _dest_rows(eid, rank, pstart, tm):
    t = eid.shape[0]
    tm = min(tm, t)
    ps = jnp.zeros((1, LANES), F32).at[0, :pstart.shape[0]].set(pstart.astype(F32))
    narrow = pl.BlockSpec((tm, LANES), lambda i: (i, 0))
    return pl.pallas_call(
        _dest_kernel,
        grid=(t // tm,),
        in_specs=[narrow, narrow, pl.BlockSpec((1, LANES), lambda i: (0, 0))],
        out_specs=narrow,
        out_shape=jax.ShapeDtypeStruct((t, LANES), jnp.int32),
        compiler_params=_params("arbitrary"),
        name="expert_dest_rows",
    )(eid, rank, ps)


GATHER_UNROLL = 8
GATHER_PRIORITY = 1


def _expert_ffn_kernel(te_ref, trb_ref, nsub_ref, ntot_ref, tok_ref, tok_next_ref, h_hbm, wg_ref, wu_ref, wd_ref,
                       y_ref, land_ref, hid_ref, sem, *, tm, n_up):
    t = pl.program_id(0)
    s = pl.program_id(1)
    n_tiles = pl.num_programs(0)
    nsub = nsub_ref[t]
    slot = t & 1

    def row_copy(tok, r, sl):
        return pltpu.make_async_copy(h_hbm.at[pl.ds(tok, 1)], land_ref.at[sl, pl.ds(r, 1)], sem.at[sl])

    def start_gather(toks_ref, blocks, sl):
        def issue(i, carry):
            for u in range(GATHER_UNROLL):
                r = i * GATHER_UNROLL + u
                row_copy(toks_ref[0, 0, r], r, sl).start(priority=GATHER_PRIORITY)
            return carry

        lax.fori_loop(0, blocks * (ROW_BLOCK // GATHER_UNROLL), issue, 0)

    @pl.when(nsub > 0)
    def _():
        @pl.when(s == 0)
        def _():
            @pl.when(t == 0)
            def _():
                start_gather(tok_ref, nsub, slot)

            def drain(i, carry):
                for u in range(GATHER_UNROLL):
                    row_copy(0, 0, slot).wait()
                return carry

            lax.fori_loop(0, nsub * (ROW_BLOCK // GATHER_UNROLL), drain, 0)

        @pl.when(s == 1)
        def _():
            nxt = jnp.minimum(t + 1, n_tiles - 1)
            n_next = jnp.where(t + 1 < n_tiles, nsub_ref[nxt], 0)
            start_gather(tok_next_ref, n_next, 1 - slot)

        for n in range(1, tm // ROW_BLOCK + 1):
            rows = n * ROW_BLOCK

            @pl.when((nsub == n) & (s < n_up))
            def _(rows=rows):
                x = _unpack_bf16_pair(land_ref[slot, pl.ds(0, rows), :])
                g = jnp.dot(x, wg_ref[...], preferred_element_type=F32)
                u = jnp.dot(x, wu_ref[...], preferred_element_type=F32)
                hid_ref[s, pl.ds(0, rows), :] = g * jax.nn.sigmoid(g) * u

            @pl.when((nsub == n) & (s >= n_up))
            def _(rows=rows):
                hid = jnp.concatenate([hid_ref[c, pl.ds(0, rows), :] for c in range(n_up)], axis=1)
                y = jnp.dot(hid, wd_ref[...], preferred_element_type=F32)
                y_ref[pl.ds(0, rows), :] = _pack_bf16_pair(y)
                if rows < tm:
                    y_ref[pl.ds(rows, tm - rows), :] = jnp.zeros((tm - rows, y_ref.shape[1]), y_ref.dtype)


def _expert_ffn(h, row_tok, w_gate, w_up, w_down, meta, n_tiles, tm, tf, tn):
    d = w_gate.shape[1]
    f = w_gate.shape[2]
    tf = min(tf, f)
    tn = min(tn, d)
    n_up = f // tf
    n_down = d // tn
    n_steps = n_up + n_down

    def step(t, s, ntot):
        return jnp.where(t < ntot[0], s, n_steps - 1)

    def up_idx(t, s, te, trb, ns, ntot):
        return (te[t], 0, jnp.minimum(step(t, s, ntot), n_up - 1))

    def down_col(t, s, ntot):
        return jnp.maximum(step(t, s, ntot) - n_up, 0)

    grid_spec = pltpu.PrefetchScalarGridSpec(
        num_scalar_prefetch=4,
        grid=(n_tiles, n_steps),
        in_specs=[
            pl.BlockSpec((1, 1, tm), lambda t, s, te, trb, ns, ntot: (trb[t], 0, 0), memory_space=pltpu.SMEM),
            pl.BlockSpec((1, 1, tm), lambda t, s, te, trb, ns, ntot: (trb[jnp.minimum(t + 1, n_tiles - 1)], 0, 0),
                         memory_space=pltpu.SMEM),
            pl.BlockSpec(memory_space=pl.ANY),
            pl.BlockSpec((None, d, tf), up_idx),
            pl.BlockSpec((None, d, tf), up_idx),
            pl.BlockSpec((None, f, tn), lambda t, s, te, trb, ns, ntot: (te[t], 0, down_col(t, s, ntot))),
        ],
        out_specs=pl.BlockSpec((tm, tn // 2), lambda t, s, te, trb, ns, ntot: (trb[t], down_col(t, s, ntot))),
        scratch_shapes=[
            pltpu.VMEM((2, tm, d // 2), jnp.uint32),
            pltpu.VMEM((n_up, tm, tf), F32),
            pltpu.SemaphoreType.DMA((2,)),
        ],
    )
    return pl.pallas_call(
        functools.partial(_expert_ffn_kernel, tm=tm, n_up=n_up),
        grid_spec=grid_spec,
        out_shape=jax.ShapeDtypeStruct((n_tiles * tm, d // 2), jnp.uint32),
        compiler_params=_params("arbitrary", "arbitrary"),
        name="expert_ffn",
    )(*meta, row_tok.reshape(n_tiles, 1, tm), row_tok.reshape(n_tiles, 1, tm), h, w_gate, w_up, w_down)


def _combine_kernel(dest_ref, dest_next_ref, wt_ref, x_ref, g_ref, y_hbm, x2_ref, h3_ref, buf_ref, sem, *, tb, chunk):
    i = pl.program_id(0)
    slot = i & 1

    def row_copy(src, k, r, sl):
        return pltpu.make_async_copy(y_hbm.at[pl.ds(src, 1)], buf_ref.at[sl, k, pl.ds(r, 1)], sem.at[sl])

    def start_gather(rows_ref, sl):
        def issue(r, carry):
            for k in range(TOP_K):
                row_copy(rows_ref[0, 0, TOP_K * r + k], k, r, sl).start(priority=GATHER_PRIORITY)
            return carry

        lax.fori_loop(0, tb, issue, 0, unroll=GATHER_UNROLL // TOP_K)

    @pl.when(i == 0)
    def _():
        start_gather(dest_ref, slot)

    @pl.when(i + 1 < pl.num_programs(0))
    def _():
        start_gather(dest_next_ref, 1 - slot)

    def drain(r, carry):
        for k in range(TOP_K):
            row_copy(0, k, r, slot).wait()
        return carry

    lax.fori_loop(0, tb, drain, 0, unroll=GATHER_UNROLL // TOP_K)

    def expert_rows(k):
        words = buf_ref[slot, k]
        half = chunk // 2
        return jnp.concatenate([_unpack_bf16_pair(words[:, c * half:(c + 1) * half])
                                for c in range(words.shape[1] // half)], axis=1)

    wt = wt_ref[...]
    x2 = x_ref[...] + (wt[:, 0:1] * expert_rows(0) + wt[:, 1:2] * expert_rows(1))
    x2_ref[...] = x2
    ms = jnp.mean(x2 * x2, axis=-1, keepdims=True)
    h3_ref[...] = (x2 * lax.rsqrt(ms + EPS) * g_ref[...]).astype(h3_ref.dtype)


def _combine(x, y, dest, wts, gain, tb, chunk):
    t, d = x.shape
    tb = min(tb, t)
    nb = t // tb
    dest3 = dest.reshape(nb, 1, TOP_K * tb)
    row = pl.BlockSpec((tb, d), lambda i: (i, 0))
    return pl.pallas_call(
        functools.partial(_combine_kernel, tb=tb, chunk=chunk),
        grid=(nb,),
        in_specs=[
            pl.BlockSpec((1, 1, TOP_K * tb), lambda i: (i, 0, 0), memory_space=pltpu.SMEM),
            pl.BlockSpec((1, 1, TOP_K * tb), lambda i: (jnp.minimum(i + 1, nb - 1), 0, 0), memory_space=pltpu.SMEM),
            pl.BlockSpec((tb, LANES), lambda i: (i, 0)),
            row,
            pl.BlockSpec((1, d), lambda i: (0, 0)),
            pl.BlockSpec(memory_space=pl.ANY),
        ],
        out_specs=[row, row],
        out_shape=[jax.ShapeDtypeStruct((t, d), F32), jax.ShapeDtypeStruct((t, d), BF16)],
        scratch_shapes=[pltpu.VMEM((2, TOP_K, tb, d // 2), jnp.uint32), pltpu.SemaphoreType.DMA((2,))],
        compiler_params=_params("arbitrary"),
        name="moe_combine_norm",
    )(dest3, dest3, wts, x, gain.reshape(1, d), y)


def _ple_kernel(h_ref, p_ref, wp_ref, x_ref, wg_hbm, o_ref, stage_ref, wgb_ref, wpb_ref, sem, *, tn):
    _staged_weight(wg_hbm, stage_ref, wgb_ref, sem, tn)

    @pl.when(pl.program_id(1) == 0)
    def _():
        wpb_ref[...] = wp_ref[...].astype(BF16)

    a = jnp.dot(h_ref[...], wgb_ref[...], preferred_element_type=F32)
    b = jnp.dot(p_ref[...].astype(BF16), wpb_ref[...], preferred_element_type=F32)
    o_ref[...] = x_ref[...] + jax.nn.sigmoid(a) * b


def _ple(h, p, w_gate, w_proj, x, tm, tn):
    m, k = h.shape
    kp = p.shape[1]
    n = w_gate.shape[1]
    tm = min(tm, m)
    tn = min(tn, n)
    return pl.pallas_call(
        functools.partial(_ple_kernel, tn=tn),
        grid=(n // tn, m // tm),
        in_specs=[
            pl.BlockSpec((tm, k), lambda j, i: (i, 0)),
            pl.BlockSpec((tm, kp), lambda j, i: (i, 0)),
            pl.BlockSpec((kp, tn), lambda j, i: (0, j)),
            pl.BlockSpec((tm, tn), lambda j, i: (i, j)),
            pl.BlockSpec(memory_space=pl.ANY),
        ],
        out_specs=pl.BlockSpec((tm, tn), lambda j, i: (i, j)),
        out_shape=jax.ShapeDtypeStruct((m, n), F32),
        scratch_shapes=[pltpu.VMEM((k, tn), F32), pltpu.VMEM((k, tn), BF16), pltpu.VMEM((kp, tn), BF16),
                        pltpu.SemaphoreType.DMA],
        compiler_params=_params("arbitrary", "arbitrary"),
        name="ple_gate_residual",
    )(h, p, w_proj, x, w_gate)


EXPERT_TILE_ROWS = 640

NORM_ROWS = 512
QKV_TILE_ROWS = 512
CONV_PROJ_TILE = (1024, 256)
GATE_PROJ_TILE = (1024, 1024)
MERGE_TILE = (1024, 1024)
OUT_PROJ_TILE = (512, 1024)
PLE_TILE = (512, 1024)
ROUTER_ROWS = 512
RANK_ROWS = 1024
FFN_UP_COLS = 256
FFN_DOWN_COLS = 1024
COMBINE_ROWS = 256


def _rope_tables(s):
    half = HEAD_DIM // 2
    inv_freq = ROPE_THETA ** (-jnp.arange(half, dtype=F32) / half)
    ang = jnp.arange(s, dtype=jnp.int32).astype(F32)[:, None] * inv_freq[None, :]
    cos, sin = jnp.cos(ang), jnp.sin(ang)
    return jnp.concatenate([cos, cos], axis=-1), jnp.concatenate([-sin, sin], axis=-1)


def _expert_tiles(counts, n_assign, tm):
    n_experts = counts.shape[0]
    n_tiles = -(-n_assign // tm) + n_experts
    tiles_e = (counts + tm - 1) // tm
    tcum = jnp.cumsum(tiles_e)
    tstart = tcum - tiles_e
    pstart = tstart * tm
    total = tcum[-1]
    t = jnp.arange(n_tiles, dtype=jnp.int32)
    tc = jnp.minimum(t, total - 1)
    te = jnp.minimum(jnp.searchsorted(tcum, tc, side="right"), n_experts - 1).astype(jnp.int32)
    local = tc - tstart[te]
    rows_left = counts[te] - local * tm
    nsub = jnp.clip((rows_left + ROW_BLOCK - 1) // ROW_BLOCK, 0, tm // ROW_BLOCK)
    nsub = jnp.where(t < total, nsub, 0).astype(jnp.int32)
    trb = (tstart[te] + local).astype(jnp.int32)
    meta = (te, trb, nsub, jnp.reshape(total, (1,)).astype(jnp.int32))
    return pstart.astype(jnp.int32), meta, n_tiles


def _layer(x, p, w_in, w_conv, w_up_a, w_out_b, w_o, g_mix, g_ffn, w_group, b_group, w_router, b_router,
           w_gate, w_up, w_down, g_ple, w_ple_gate, w_ple_proj):
    s, d = x.shape
    cw = w_conv.shape[1]
    n_experts = w_gate.shape[0]
    ng = len(ATTN_DILATIONS)
    a_out = HEADS_PER_GROUP * HEAD_DIM
    q_cols = ng * a_out
    qkv_cols = q_cols + 2 * a_out
    col_u, col_gb, col_gc = qkv_cols, qkv_cols + cw, qkv_cols + 2 * cw
    col_gates = qkv_cols + 3 * cw

    h = _rmsnorm(x, g_mix, BF16, NORM_ROWS)
    rope_c, rope_s = _rope_tables(s)
    qkv = _qkv_proj(h, w_in, rope_c, rope_s, qkv_cols, q_cols + a_out, QKV_TILE_ROWS, qkv_cols // 2)
    y_a = _dilated_attention(qkv)
    y_b = _conv_proj(h, w_in, w_conv, col_u, col_gb, col_gc, *CONV_PROJ_TILE)
    gates = _mm_cols(h, w_in, col_gates, *GATE_PROJ_TILE)
    merged = _merge(y_a, y_b, gates, w_up_a, w_out_b, 0, d, *MERGE_TILE)
    x1 = _resid_mm(merged, w_o, x, *OUT_PROJ_TILE)

    n_route = N_EXPERT_GROUPS + n_experts
    w_route = jnp.pad(jnp.concatenate([w_group, w_router], axis=1), ((0, 0), (0, LANES - n_route)))
    b_route = jnp.pad(jnp.concatenate([b_group, b_router]), (0, LANES - n_route)).reshape(1, LANES)
    h2, eid, wts = _router(x1, g_ffn, w_route, b_route, N_EXPERT_GROUPS, n_experts, ROUTER_ROWS)
    rank, cnt = _rank(eid, RANK_ROWS)
    counts = cnt[0, :n_experts]
    tm_e = EXPERT_TILE_ROWS
    pstart, meta, n_tiles = _expert_tiles(counts, s * TOP_K, tm_e)
    dest = _dest_rows(eid, rank, pstart, RANK_ROWS)[:, :TOP_K].reshape(-1)
    tokens = jnp.repeat(jnp.arange(s, dtype=jnp.int32), TOP_K)
    row_tok = jnp.zeros((n_tiles * tm_e,), jnp.int32).at[dest].set(tokens, unique_indices=True)
    down_cols = min(FFN_DOWN_COLS, d)
    y = _expert_ffn(h2, row_tok, w_gate, w_up, w_down, meta, n_tiles, tm_e, FFN_UP_COLS, down_cols)
    x2, h3 = _combine(x1, y, dest, wts, g_ple, COMBINE_ROWS, down_cols)

    return _ple(h3, p, w_ple_gate, w_ple_proj, x2, *PLE_TILE)


def kernel(x, p, w_in, w_conv, w_up_a, w_out_b, w_o, norm_mix, norm_ffn, w_group, b_group, w_router, b_router,
           w_gate, w_up, w_down, norm_ple, w_ple_gate, w_ple_proj, norm_final):
    b, s, d = x.shape
    depth = w_in.shape[0]
    outs = []
    for bi in range(b):
        xb = x[bi]
        for i in range(depth):
            xb = _layer(xb, p[i, bi], w_in[i], w_conv[i], w_up_a[i], w_out_b[i], w_o[i], norm_mix[i], norm_ffn[i],
                        w_group[i], b_group[i], w_router[i], b_router[i], w_gate[i], w_up[i], w_down[i],
                        norm_ple[i], w_ple_gate[i], w_ple_proj[i])
        outs.append(_rmsnorm(xb, norm_final, x.dtype, NORM_ROWS))
    return jnp.stack(outs)
```

```python
import functools

import jax
import jax.numpy as jnp
from jax import lax
from jax.experimental import pallas as pl
from jax.experimental.pallas import tpu as pltpu

F32 = jnp.float32
BF16 = jnp.bfloat16

HEAD_DIM = 128
ATTN_DILATIONS = (1, 4, 16)
HEADS_PER_GROUP = 4
ROPE_THETA = 10000.0
ATTN_BLOCK = 128
NEG_BIG = -1e30
N_EXPERT_GROUPS = 4
TOP_K = 2
EPS = 1e-6
ROW_BLOCK = 128
LANES = 128

VMEM_LIMIT_BYTES = 58 * 1024 * 1024


def _params(*sem):
    return pltpu.CompilerParams(dimension_semantics=sem, vmem_limit_bytes=VMEM_LIMIT_BYTES)


def _rmsnorm_kernel(x_ref, g_ref, o_ref):
    x = x_ref[...]
    ms = jnp.mean(x * x, axis=-1, keepdims=True)
    o_ref[...] = (x * lax.rsqrt(ms + EPS) * g_ref[...]).astype(o_ref.dtype)


def _rmsnorm(x, gain, out_dtype, tm):
    n, d = x.shape
    tm = min(tm, n)
    return pl.pallas_call(
        _rmsnorm_kernel,
        grid=(n // tm,),
        in_specs=[pl.BlockSpec((tm, d), lambda i: (i, 0)), pl.BlockSpec((1, d), lambda i: (0, 0))],
        out_specs=pl.BlockSpec((tm, d), lambda i: (i, 0)),
        out_shape=jax.ShapeDtypeStruct((n, d), out_dtype),
        compiler_params=_params("arbitrary"),
        name="rmsnorm",
    )(x, gain.reshape(1, d))


def _staged_weight(w_hbm, stage_ref, wb_ref, sem, tn, col0=0):
    j = pl.program_id(0)
    i = pl.program_id(1)

    def fetch(jj):
        cols = pl.ds(pl.multiple_of(col0 + jj * tn, LANES), tn)
        return pltpu.make_async_copy(w_hbm.at[:, cols], stage_ref, sem)

    @pl.when(i == 0)
    def _():
        @pl.when(j == 0)
        def _():
            fetch(0).start()

        fetch(j).wait()
        wb_ref[...] = stage_ref[...].astype(BF16)

        @pl.when(j + 1 < pl.num_programs(0))
        def _():
            fetch(j + 1).start()


def _qkv_proj_kernel(h_ref, c_ref, s_ref, w_hbm, o_ref, stage_ref, wb_ref, sem, *, tn, n_rope_heads):
    _staged_weight(w_hbm, stage_ref, wb_ref, sem, tn)
    acc = jnp.dot(h_ref[...], wb_ref[...], preferred_element_type=F32)
    heads = o_ref.shape[0]
    first_head = pl.program_id(0) * heads
    for hh in range(heads):
        t = acc[:, hh * HEAD_DIM:(hh + 1) * HEAD_DIM]
        rotary = first_head + hh < n_rope_heads
        c = jnp.where(rotary, c_ref[...], 1.0)
        s = jnp.where(rotary, s_ref[...], 0.0)
        o_ref[hh] = t * c + pltpu.roll(t, HEAD_DIM // 2, 1) * s


def _qkv_proj(h, w, rope_c, rope_s, n_cols, n_rope_cols, tm, tn):
    m, k = h.shape
    tm = min(tm, m)
    assert n_cols % tn == 0 and tn % HEAD_DIM == 0 and n_rope_cols % HEAD_DIM == 0 and m % tm == 0
    heads = tn // HEAD_DIM
    return pl.pallas_call(
        functools.partial(_qkv_proj_kernel, tn=tn, n_rope_heads=n_rope_cols // HEAD_DIM),
        grid=(n_cols // tn, m // tm),
        in_specs=[
            pl.BlockSpec((tm, k), lambda j, i: (i, 0)),
            pl.BlockSpec((tm, HEAD_DIM), lambda j, i: (i, 0)),
            pl.BlockSpec((tm, HEAD_DIM), lambda j, i: (i, 0)),
            pl.BlockSpec(memory_space=pl.ANY),
        ],
        out_specs=pl.BlockSpec((heads, tm, HEAD_DIM), lambda j, i: (j, i, 0)),
        out_shape=jax.ShapeDtypeStruct((n_cols // HEAD_DIM, m, HEAD_DIM), F32),
        scratch_shapes=[pltpu.VMEM((k, tn), F32), pltpu.VMEM((k, tn), BF16), pltpu.SemaphoreType.DMA],
        compiler_params=_params("arbitrary", "arbitrary"),
        name="qkv_proj_rope",
    )(h, rope_c, rope_s, w)


def _mm_kernel(a_ref, w_hbm, o_ref, stage_ref, wb_ref, sem, *, tn, col0):
    _staged_weight(w_hbm, stage_ref, wb_ref, sem, tn, col0)
    o_ref[...] = jnp.dot(a_ref[...], wb_ref[...], preferred_element_type=F32).astype(o_ref.dtype)


def _mm_cols(a, w, col0, tm, tn):
    m, k = a.shape
    n = w.shape[1] - col0
    tm = min(tm, m)
    while n % tn:
        tn //= 2
    assert tn % LANES == 0 and col0 % LANES == 0 and m % tm == 0
    return pl.pallas_call(
        functools.partial(_mm_kernel, tn=tn, col0=col0),
        grid=(n // tn, m // tm),
        in_specs=[
            pl.BlockSpec((tm, k), lambda j, i: (i, 0)),
            pl.BlockSpec(memory_space=pl.ANY),
        ],
        out_specs=pl.BlockSpec((tm, tn), lambda j, i: (i, j)),
        out_shape=jax.ShapeDtypeStruct((m, n), BF16),
        scratch_shapes=[pltpu.VMEM((k, tn), F32), pltpu.VMEM((k, tn), BF16), pltpu.SemaphoreType.DMA],
        compiler_params=_params("arbitrary", "arbitrary"),
        name="merge_gate_proj",
    )(a, w)


ATTN_SUPER = ATTN_BLOCK * max(ATTN_DILATIONS)
LOG2_E = 1.4426950408889634
LN_2 = 0.6931471805599453


def _rows(start, size, stride):
    return pl.ds(start, size) if stride == 1 else pl.ds(start, size, stride=stride)


def _dilated_attn_kernel(q0_ref, q1_ref, q2_ref, kc_ref, kp_ref, vc_ref, vp_ref, y_ref, o_ref, lse_ref):
    sb = pl.program_id(1)
    sup = ATTN_SUPER
    qi = lax.broadcasted_iota(jnp.int32, (ATTN_BLOCK, 2 * ATTN_BLOCK), 0)
    kj = lax.broadcasted_iota(jnp.int32, (ATTN_BLOCK, 2 * ATTN_BLOCK), 1)
    dist = qi + ATTN_BLOCK - kj
    band = (dist >= 0) & (dist <= ATTN_BLOCK)
    band_first = band & ((kj >= ATTN_BLOCK) | (sb > 0))
    q_scale = HEAD_DIM ** -0.5 * LOG2_E

    for g, (q_ref, d) in enumerate(zip((q0_ref, q1_ref, q2_ref), ATTN_DILATIONS)):
        span = ATTN_BLOCK * d
        for j in range(sup // span):
          @pl.when(sb >= 0)
          def _(g=g, q_ref=q_ref, d=d, span=span, j=j):
            for r in range(d):
                row0 = j * span + r
                rows = _rows(row0, ATTN_BLOCK, d)
                if j == 0:
                    prev_rows = _rows(sup - span + r, ATTN_BLOCK, d)
                    k_prev, v_prev = kp_ref[prev_rows, :], vp_ref[prev_rows, :]
                else:
                    prev_rows = _rows(row0 - span, ATTN_BLOCK, d)
                    k_prev, v_prev = kc_ref[prev_rows, :], vc_ref[prev_rows, :]
                q = (q_ref[rows, :] * q_scale).astype(BF16)
                k = jnp.concatenate([k_prev, kc_ref[rows, :]], axis=0).astype(BF16)
                v = jnp.concatenate([v_prev, vc_ref[rows, :]], axis=0).astype(BF16)
                s = lax.dot_general(q, k, (((1,), (1,)), ((), ())), preferred_element_type=F32)
                s = jnp.where(band_first if j == 0 else band, s, NEG_BIG)
                m = jnp.max(s, axis=-1, keepdims=True)
                e = jnp.exp2(s - m)
                ssum = jnp.sum(e, axis=-1, keepdims=True)
                p = (e * (1.0 / ssum)).astype(BF16)
                o_ref[g, rows, :] = jnp.dot(p, v, preferred_element_type=F32)
                lse = m * LN_2 + jnp.log(ssum)
                lse_ref[g, rows, :] = jnp.broadcast_to(lse, (ATTN_BLOCK, HEAD_DIM))

    lse = lse_ref[...]
    m = jnp.max(lse, axis=0)
    e = jnp.exp(lse - m[None])
    w = e / jnp.sum(e, axis=0)[None]
    y_ref[...] = jnp.sum(w * o_ref[...], axis=0).astype(y_ref.dtype)


def _dilated_attention(qkv):
    nh, s, hd = qkv.shape
    ng = len(ATTN_DILATIONS)
    hpg = HEADS_PER_GROUP
    assert nh == (ng + 2) * hpg and hd == HEAD_DIM and s % ATTN_SUPER == 0
    sup = ATTN_SUPER

    def q_spec(g):
        return pl.BlockSpec((None, sup, hd), lambda h, b: (g * hpg + h, b, 0))

    def kv_spec(first_head, prev):
        if prev:
            return pl.BlockSpec((None, sup, hd), lambda h, b: (first_head + h, jnp.maximum(b - 1, 0), 0))
        return pl.BlockSpec((None, sup, hd), lambda h, b: (first_head + h, b, 0))

    k0, v0 = ng * hpg, (ng + 1) * hpg
    return pl.pallas_call(
        _dilated_attn_kernel,
        grid=(hpg, s // sup),
        in_specs=[q_spec(0), q_spec(1), q_spec(2), kv_spec(k0, False), kv_spec(k0, True),
                  kv_spec(v0, False), kv_spec(v0, True)],
        out_specs=pl.BlockSpec((sup, hd), lambda h, b: (b, h)),
        out_shape=jax.ShapeDtypeStruct((s, hpg * hd), BF16),
        scratch_shapes=[pltpu.VMEM((ng, sup, hd), F32), pltpu.VMEM((ng, sup, hd), F32)],
        compiler_params=_params("arbitrary", "arbitrary"),
        name="dilated_attention",
    )(qkv, qkv, qkv, qkv, qkv, qkv, qkv)


CONV_HALO = 16


def _conv_proj_kernel(h_ref, wc_ref, w_hbm, o_ref, stage_ref, wb_ref, carry_ref, sem, *, tc, cols):
    j = pl.program_id(0)
    i = pl.program_id(1)

    def fetch(jj):
        return [pltpu.make_async_copy(w_hbm.at[:, pl.ds(pl.multiple_of(c0 + jj * tc, LANES), tc)],
                                      stage_ref.at[g], sem.at[g]) for g, c0 in enumerate(cols)]

    @pl.when(i == 0)
    def _():
        @pl.when(j == 0)
        def _():
            for c in fetch(0):
                c.start()

        for c in fetch(j):
            c.wait()
        for g in range(len(cols)):
            wb_ref[:, g * tc:(g + 1) * tc] = stage_ref[g].astype(BF16)
        carry_ref[...] = jnp.zeros_like(carry_ref)

        @pl.when(j + 1 < pl.num_programs(0))
        def _():
            for c in fetch(j + 1):
                c.start()

    acc = jnp.dot(h_ref[...], wb_ref[...], preferred_element_type=F32)
    u, gate_b, gate_c = acc[:, :tc], acc[:, tc:2 * tc], acc[:, 2 * tc:]
    cur = gate_c * u
    big = jnp.concatenate([carry_ref[...], cur], axis=0)
    back1 = pltpu.roll(big, 1, 0)[CONV_HALO:]
    back2 = pltpu.roll(big, 2, 0)[CONV_HALO:]
    w = wc_ref[...]
    y = w[0:1] * back2 + w[1:2] * back1 + w[2:3] * cur
    o_ref[...] = (gate_b * y).astype(o_ref.dtype)
    carry_ref[...] = cur[cur.shape[0] - CONV_HALO:, :]


def _conv_proj(h, w, w_conv, col_u, col_gb, col_gc, tm, tc):
    s, k = h.shape
    kw, cw = w_conv.shape
    assert kw == 3
    tm = min(tm, s)
    tc = min(tc, cw)
    assert cw % tc == 0 and tc % LANES == 0 and s % tm == 0 and tm % CONV_HALO == 0
    cols = (col_u, col_gb, col_gc)
    return pl.pallas_call(
        functools.partial(_conv_proj_kernel, tc=tc, cols=cols),
        grid=(cw // tc, s // tm),
        in_specs=[
            pl.BlockSpec((tm, k), lambda j, i: (i, 0)),
            pl.BlockSpec((kw, tc), lambda j, i: (0, j)),
            pl.BlockSpec(memory_space=pl.ANY),
        ],
        out_specs=pl.BlockSpec((tm, tc), lambda j, i: (i, j)),
        out_shape=jax.ShapeDtypeStruct((s, cw), BF16),
        scratch_shapes=[pltpu.VMEM((len(cols), k, tc), F32), pltpu.VMEM((k, len(cols) * tc), BF16),
                        pltpu.VMEM((CONV_HALO, tc), F32), pltpu.SemaphoreType.DMA((len(cols),))],
        compiler_params=_params("arbitrary", "arbitrary"),
        name="conv_proj",
    )(h, w_conv, w)


def _merge_kernel(ya_ref, yb_ref, wa_ref, wb_ref, ma_ref, mb_ref, o_ref, wab_ref, wbb_ref):
    i = pl.program_id(1)

    @pl.when(i == 0)
    def _():
        wab_ref[...] = wa_ref[...].astype(BF16)
        wbb_ref[...] = wb_ref[...].astype(BF16)

    a = jnp.dot(ya_ref[...], wab_ref[...], preferred_element_type=F32)
    b = jnp.dot(yb_ref[...], wbb_ref[...], preferred_element_type=F32)
    ga = jax.nn.sigmoid(ma_ref[...].astype(F32))
    gb = jax.nn.sigmoid(mb_ref[...].astype(F32))
    o_ref[...] = (ga * a + gb * b).astype(o_ref.dtype)


def _merge(y_a, y_b, proj, w_a, w_b, col_ma, col_mb, tm, tn):
    m, ka = y_a.shape
    kb = y_b.shape[1]
    n = w_a.shape[1]
    tm = min(tm, m)
    tn = min(tn, n)
    while col_ma % tn or col_mb % tn:
        tn //= 2
    assert tn % LANES == 0
    return pl.pallas_call(
        _merge_kernel,
        grid=(n // tn, m // tm),
        in_specs=[
            pl.BlockSpec((tm, ka), lambda j, i: (i, 0)),
            pl.BlockSpec((tm, kb), lambda j, i: (i, 0)),
            pl.BlockSpec((ka, tn), lambda j, i: (0, j)),
            pl.BlockSpec((kb, tn), lambda j, i: (0, j)),
            pl.BlockSpec((tm, tn), lambda j, i: (i, col_ma // tn + j)),
            pl.BlockSpec((tm, tn), lambda j, i: (i, col_mb // tn + j)),
        ],
        out_specs=pl.BlockSpec((tm, tn), lambda j, i: (i, j)),
        out_shape=jax.ShapeDtypeStruct((m, n), BF16),
        scratch_shapes=[pltpu.VMEM((ka, tn), BF16), pltpu.VMEM((kb, tn), BF16)],
        compiler_params=_params("arbitrary", "arbitrary"),
        name="mixer_merge",
    )(y_a, y_b, w_a, w_b, proj, proj)


def _resid_mm_kernel(a_ref, x_ref, w_hbm, o_ref, stage_ref, wb_ref, sem, *, tn):
    _staged_weight(w_hbm, stage_ref, wb_ref, sem, tn)
    o_ref[...] = x_ref[...] + jnp.dot(a_ref[...], wb_ref[...], preferred_element_type=F32)


def _resid_mm(a, w, x, tm, tn):
    m, k = a.shape
    n = w.shape[1]
    tm = min(tm, m)
    tn = min(tn, n)
    return pl.pallas_call(
        functools.partial(_resid_mm_kernel, tn=tn),
        grid=(n // tn, m // tm),
        in_specs=[
            pl.BlockSpec((tm, k), lambda j, i: (i, 0)),
            pl.BlockSpec((tm, tn), lambda j, i: (i, j)),
            pl.BlockSpec(memory_space=pl.ANY),
        ],
        out_specs=pl.BlockSpec((tm, tn), lambda j, i: (i, j)),
        out_shape=jax.ShapeDtypeStruct((m, n), F32),
        scratch_shapes=[pltpu.VMEM((k, tn), F32), pltpu.VMEM((k, tn), BF16), pltpu.SemaphoreType.DMA],
        compiler_params=_params("arbitrary", "arbitrary"),
        name="out_proj_residual",
    )(a, x, w)


def _split_bf16(a):
    hi = a.astype(BF16)
    lo = (a - hi.astype(F32)).astype(BF16)
    return hi, lo


def _pack_bf16_pair(a):
    c = a.shape[1] // 2
    bits = lax.bitcast_convert_type(a.astype(BF16).astype(F32), jnp.uint32)
    return (bits[:, :c] >> 16) | (bits[:, c:] & jnp.uint32(0xFFFF0000))


def _unpack_bf16_pair(w):
    lo = lax.bitcast_convert_type(w << 16, F32)
    hi = lax.bitcast_convert_type(w & jnp.uint32(0xFFFF0000), F32)
    return jnp.concatenate([lo, hi], axis=1)


def _router_kernel(x_ref, g_ref, w_ref, b_ref, h_ref, eid_ref, wt_ref, whl_ref, *, n_groups, n_experts):
    @pl.when(pl.program_id(0) == 0)
    def _():
        whl_ref[:, :LANES], whl_ref[:, LANES:] = _split_bf16(w_ref[...])

    x = x_ref[...]
    ms = jnp.mean(x * x, axis=-1, keepdims=True)
    h = x * lax.rsqrt(ms + EPS) * g_ref[...]
    h_ref[...] = _pack_bf16_pair(h)
    hh, hl = _split_bf16(h)
    both = jnp.dot(hh, whl_ref[...], preferred_element_type=F32)
    logits = (both[:, :LANES] + both[:, LANES:]
              + jnp.dot(hl, whl_ref[:, :LANES], preferred_element_type=F32)) + b_ref[...]
    tm = x.shape[0]
    per_group = n_experts // n_groups
    lane = lax.broadcasted_iota(jnp.int32, (tm, LANES), 1)
    is_g = lane < n_groups
    gl = jnp.where(is_g, logits, NEG_BIG)
    gmax = jnp.max(gl, axis=-1, keepdims=True)
    gidx = jnp.min(jnp.where(gl == gmax, lane, LANES), axis=-1, keepdims=True)
    gsum = jnp.sum(jnp.where(is_g, jnp.exp(gl - gmax), 0.0), axis=-1, keepdims=True)
    g_w = 1.0 / gsum
    lane_e = lane - n_groups
    sel = (lane_e >= 0) & (lane_e < n_experts) & (lane_e // per_group == gidx)
    el = jnp.where(sel, logits, NEG_BIG)
    emax = jnp.max(el, axis=-1, keepdims=True)
    i1 = jnp.min(jnp.where(el == emax, lane, LANES), axis=-1, keepdims=True)
    esum = jnp.sum(jnp.where(sel, jnp.exp(el - emax), 0.0), axis=-1, keepdims=True)
    el2 = jnp.where(lane == i1, NEG_BIG, el)
    emax2 = jnp.max(el2, axis=-1, keepdims=True)
    i2 = jnp.min(jnp.where(el2 == emax2, lane, LANES), axis=-1, keepdims=True)
    p1 = 1.0 / esum
    p2 = jnp.exp(emax2 - emax) / esum
    den = p1 + p2
    eid_ref[...] = jnp.where(lane == 0, i1 - n_groups, jnp.where(lane == 1, i2 - n_groups, 0))
    wt_ref[...] = jnp.where(lane == 0, g_w * (p1 / den), jnp.where(lane == 1, g_w * (p2 / den), 0.0))


def _router(x, gain, w_route, b_route, n_groups, n_experts, tm):
    t, d = x.shape
    tm = min(tm, t)
    row = pl.BlockSpec((tm, d), lambda i: (i, 0))
    narrow = pl.BlockSpec((tm, LANES), lambda i: (i, 0))
    return pl.pallas_call(
        functools.partial(_router_kernel, n_groups=n_groups, n_experts=n_experts),
        grid=(t // tm,),
        in_specs=[row, pl.BlockSpec((1, d), lambda i: (0, 0)), pl.BlockSpec((d, LANES), lambda i: (0, 0)),
                  pl.BlockSpec((1, LANES), lambda i: (0, 0))],
        out_specs=[pl.BlockSpec((tm, d // 2), lambda i: (i, 0)), narrow, narrow],
        out_shape=[jax.ShapeDtypeStruct((t, d // 2), jnp.uint32), jax.ShapeDtypeStruct((t, LANES), jnp.int32),
                   jax.ShapeDtypeStruct((t, LANES), F32)],
        scratch_shapes=[pltpu.VMEM((d, 2 * LANES), BF16)],
        compiler_params=_params("arbitrary"),
        name="ffn_norm_router",
    )(x, gain.reshape(1, d), w_route, b_route)


def _rank_kernel(eid_ref, rank_ref, cnt_ref, carry_ref):
    i = pl.program_id(0)

    @pl.when(i == 0)
    def _():
        carry_ref[...] = jnp.zeros_like(carry_ref)

    eid = eid_ref[...]
    tm = eid.shape[0]
    lane = lax.broadcasted_iota(jnp.int32, (tm, LANES), 1)
    oh0 = lane == eid[:, 0:1]
    oh1 = lane == eid[:, 1:2]
    ohs = oh0.astype(F32) + oh1.astype(F32)
    r = lax.broadcasted_iota(jnp.int32, (tm, tm), 0)
    c = lax.broadcasted_iota(jnp.int32, (tm, tm), 1)
    earlier = (c < r).astype(BF16)
    before = jnp.dot(earlier, ohs.astype(BF16), preferred_element_type=F32) + carry_ref[...]
    rank0 = jnp.sum(jnp.where(oh0, before, 0.0), axis=-1, keepdims=True)
    rank1 = jnp.sum(jnp.where(oh1, before, 0.0), axis=-1, keepdims=True)
    rank_ref[...] = jnp.where(lane == 0, rank0, jnp.where(lane == 1, rank1, 0.0)).astype(jnp.int32)
    total = carry_ref[...] + jnp.sum(ohs, axis=0, keepdims=True)
    carry_ref[...] = total
    cnt_ref[...] = jnp.broadcast_to(total, cnt_ref.shape).astype(jnp.int32)


def _rank(eid, tm):
    t = eid.shape[0]
    tm = min(tm, t)
    narrow = pl.BlockSpec((tm, LANES), lambda i: (i, 0))
    return pl.pallas_call(
        _rank_kernel,
        grid=(t // tm,),
        in_specs=[narrow],
        out_specs=[narrow, pl.BlockSpec((8, LANES), lambda i: (0, 0))],
        out_shape=[jax.ShapeDtypeStruct((t, LANES), jnp.int32), jax.ShapeDtypeStruct((8, LANES), jnp.int32)],
        scratch_shapes=[pltpu.VMEM((1, LANES), F32)],
        compiler_params=_params("arbitrary"),
        name="expert_rank",
    )(eid)


def _dest_kernel(eid_ref, rank_ref, ps_ref, o_ref):
    eid = eid_ref[...]
    lane = lax.broadcasted_iota(jnp.int32, eid.shape, 1)
    ps = ps_ref[...]
    s0 = jnp.sum(jnp.where(lane == eid[:, 0:1], ps, 0.0), axis=-1, keepdims=True)
    s1 = jnp.sum(jnp.where(lane == eid[:, 1:2], ps, 0.0), axis=-1, keepdims=True)
    start = jnp.where(lane == 0, s0, jnp.where(lane == 1, s1, 0.0))
    o_ref[...] = start.astype(jnp.int32) + rank_ref[...]


def _dest_rows(eid, rank, pstart, tm):
    t = eid.shape[0]
    tm = min(tm, t)
    ps = jnp.zeros((1, LANES), F32).at[0, :pstart.shape[0]].set(pstart.astype(F32))
    narrow = pl.BlockSpec((tm, LANES), lambda i: (i, 0))
    return pl.pallas_call(
        _dest_kernel,
        grid=(t // tm,),
        in_specs=[narrow, narrow, pl.BlockSpec((1, LANES), lambda i: (0, 0))],
        out_specs=narrow,
        out_shape=jax.ShapeDtypeStruct((t, LANES), jnp.int32),
        compiler_params=_params("arbitrary"),
        name="expert_dest_rows",
    )(eid, rank, ps)


GATHER_UNROLL = 8
GATHER_PRIORITY = 1


def _expert_ffn_kernel(te_ref, trb_ref, nsub_ref, ntot_ref, tok_ref, tok_next_ref, h_hbm, wg_ref, wu_ref, wd_ref,
                       y_ref, land_ref, hid_ref, sem, *, tm, n_up):
    t = pl.program_id(0)
    s = pl.program_id(1)
    n_tiles = pl.num_programs(0)
    nsub = nsub_ref[t]
    slot = t & 1

    def row_copy(tok, r, sl):
        return pltpu.make_async_copy(h_hbm.at[pl.ds(tok, 1)], land_ref.at[sl, pl.ds(r, 1)], sem.at[sl])

    def start_gather(toks_ref, blocks, sl):
        def issue(i, carry):
            for u in range(GATHER_UNROLL):
                r = i * GATHER_UNROLL + u
                row_copy(toks_ref[0, 0, r], r, sl).start(priority=GATHER_PRIORITY)
            return carry

        lax.fori_loop(0, blocks * (ROW_BLOCK // GATHER_UNROLL), issue, 0)

    @pl.when(nsub > 0)
    def _():
        @pl.when(s == 0)
        def _():
            @pl.when(t == 0)
            def _():
                start_gather(tok_ref, nsub, slot)

            def drain(i, carry):
                for u in range(GATHER_UNROLL):
                    row_copy(0, 0, slot).wait()
                return carry

            lax.fori_loop(0, nsub * (ROW_BLOCK // GATHER_UNROLL), drain, 0)

        @pl.when(s == 1)
        def _():
            nxt = jnp.minimum(t + 1, n_tiles - 1)
            n_next = jnp.where(t + 1 < n_tiles, nsub_ref[nxt], 0)
            start_gather(tok_next_ref, n_next, 1 - slot)

        for n in range(1, tm // ROW_BLOCK + 1):
            rows = n * ROW_BLOCK

            @pl.when((nsub == n) & (s < n_up))
            def _(rows=rows):
                x = _unpack_bf16_pair(land_ref[slot, pl.ds(0, rows), :])
                g = jnp.dot(x, wg_ref[...], preferred_element_type=F32)
                u = jnp.dot(x, wu_ref[...], preferred_element_type=F32)
                hid_ref[s, pl.ds(0, rows), :] = g * jax.nn.sigmoid(g) * u

            @pl.when((nsub == n) & (s >= n_up))
            def _(rows=rows):
                hid = jnp.concatenate([hid_ref[c, pl.ds(0, rows), :] for c in range(n_up)], axis=1)
                y = jnp.dot(hid, wd_ref[...], preferred_element_type=F32)
                y_ref[pl.ds(0, rows), :] = _pack_bf16_pair(y)
                if rows < tm:
                    y_ref[pl.ds(rows, tm - rows), :] = jnp.zeros((tm - rows, y_ref.shape[1]), y_ref.dtype)


def _expert_ffn(h, row_tok, w_gate, w_up, w_down, meta, n_tiles, tm, tf, tn):
    d = w_gate.shape[1]
    f = w_gate.shape[2]
    tf = min(tf, f)
    tn = min(tn, d)
    n_up = f // tf
    n_down = d // tn
    n_steps = n_up + n_down

    def step(t, s, ntot):
        return jnp.where(t < ntot[0], s, n_steps - 1)

    def up_idx(t, s, te, trb, ns, ntot):
        return (te[t], 0, jnp.minimum(step(t, s, ntot), n_up - 1))

    def down_col(t, s, ntot):
        return jnp.maximum(step(t, s, ntot) - n_up, 0)

    grid_spec = pltpu.PrefetchScalarGridSpec(
        num_scalar_prefetch=4,
        grid=(n_tiles, n_steps),
        in_specs=[
            pl.BlockSpec((1, 1, tm), lambda t, s, te, trb, ns, ntot: (trb[t], 0, 0), memory_space=pltpu.SMEM),
            pl.BlockSpec((1, 1, tm), lambda t, s, te, trb, ns, ntot: (trb[jnp.minimum(t + 1, n_tiles - 1)], 0, 0),
                         memory_space=pltpu.SMEM),
            pl.BlockSpec(memory_space=pl.ANY),
            pl.BlockSpec((None, d, tf), up_idx),
            pl.BlockSpec((None, d, tf), up_idx),
            pl.BlockSpec((None, f, tn), lambda t, s, te, trb, ns, ntot: (te[t], 0, down_col(t, s, ntot))),
        ],
        out_specs=pl.BlockSpec((tm, tn // 2), lambda t, s, te, trb, ns, ntot: (trb[t], down_col(t, s, ntot))),
        scratch_shapes=[
            pltpu.VMEM((2, tm, d // 2), jnp.uint32),
            pltpu.VMEM((n_up, tm, tf), F32),
            pltpu.SemaphoreType.DMA((2,)),
        ],
    )
    return pl.pallas_call(
        functools.partial(_expert_ffn_kernel, tm=tm, n_up=n_up),
        grid_spec=grid_spec,
        out_shape=jax.ShapeDtypeStruct((n_tiles * tm, d // 2), jnp.uint32),
        compiler_params=_params("arbitrary", "arbitrary"),
        name="expert_ffn",
    )(*meta, row_tok.reshape(n_tiles, 1, tm), row_tok.reshape(n_tiles, 1, tm), h, w_gate, w_up, w_down)


def _combine_kernel(dest_ref, dest_next_ref, wt_ref, x_ref, g_ref, y_hbm, x2_ref, h3_ref, buf_ref, sem, *, tb, chunk):
    i = pl.program_id(0)
    slot = i & 1

    def row_copy(src, k, r, sl):
        return pltpu.make_async_copy(y_hbm.at[pl.ds(src, 1)], buf_ref.at[sl, k, pl.ds(r, 1)], sem.at[sl])

    def start_gather(rows_ref, sl):
        def issue(r, carry):
            for k in range(TOP_K):
                row_copy(rows_ref[0, 0, TOP_K * r + k], k, r, sl).start(priority=GATHER_PRIORITY)
            return carry

        lax.fori_loop(0, tb, issue, 0, unroll=GATHER_UNROLL // TOP_K)

    @pl.when(i == 0)
    def _():
        start_gather(dest_ref, slot)

    @pl.when(i + 1 < pl.num_programs(0))
    def _():
        start_gather(dest_next_ref, 1 - slot)

    def drain(r, carry):
        for k in range(TOP_K):
            row_copy(0, k, r, slot).wait()
        return carry

    lax.fori_loop(0, tb, drain, 0, unroll=GATHER_UNROLL // TOP_K)

    def expert_rows(k):
        words = buf_ref[slot, k]
        half = chunk // 2
        return jnp.concatenate([_unpack_bf16_pair(words[:, c * half:(c + 1) * half])
                                for c in range(words.shape[1] // half)], axis=1)

    wt = wt_ref[...]
    x2 = x_ref[...] + (wt[:, 0:1] * expert_rows(0) + wt[:, 1:2] * expert_rows(1))
    x2_ref[...] = x2
    ms = jnp.mean(x2 * x2, axis=-1, keepdims=True)
    h3_ref[...] = (x2 * lax.rsqrt(ms + EPS) * g_ref[...]).astype(h3_ref.dtype)


def _combine(x, y, dest, wts, gain, tb, chunk):
    t, d = x.shape
    tb = min(tb, t)
    nb = t // tb
    dest3 = dest.reshape(nb, 1, TOP_K * tb)
    row = pl.BlockSpec((tb, d), lambda i: (i, 0))
    return pl.pallas_call(
        functools.partial(_combine_kernel, tb=tb, chunk=chunk),
        grid=(nb,),
        in_specs=[
            pl.BlockSpec((1, 1, TOP_K * tb), lambda i: (i, 0, 0), memory_space=pltpu.SMEM),
            pl.BlockSpec((1, 1, TOP_K * tb), lambda i: (jnp.minimum(i + 1, nb - 1), 0, 0), memory_space=pltpu.SMEM),
            pl.BlockSpec((tb, LANES), lambda i: (i, 0)),
            row,
            pl.BlockSpec((1, d), lambda i: (0, 0)),
            pl.BlockSpec(memory_space=pl.ANY),
        ],
        out_specs=[row, row],
        out_shape=[jax.ShapeDtypeStruct((t, d), F32), jax.ShapeDtypeStruct((t, d), BF16)],
        scratch_shapes=[pltpu.VMEM((2, TOP_K, tb, d // 2), jnp.uint32), pltpu.SemaphoreType.DMA((2,))],
        compiler_params=_params("arbitrary"),
        name="moe_combine_norm",
    )(dest3, dest3, wts, x, gain.reshape(1, d), y)


def _ple_kernel(h_ref, p_ref, wp_ref, x_ref, wg_hbm, o_ref, stage_ref, wgb_ref, wpb_ref, sem, *, tn):
    _staged_weight(wg_hbm, stage_ref, wgb_ref, sem, tn)

    @pl.when(pl.program_id(1) == 0)
    def _():
        wpb_ref[...] = wp_ref[...].astype(BF16)

    a = jnp.dot(h_ref[...], wgb_ref[...], preferred_element_type=F32)
    b = jnp.dot(p_ref[...].astype(BF16), wpb_ref[...], preferred_element_type=F32)
    o_ref[...] = x_ref[...] + jax.nn.sigmoid(a) * b


def _ple(h, p, w_gate, w_proj, x, tm, tn):
    m, k = h.shape
    kp = p.shape[1]
    n = w_gate.shape[1]
    tm = min(tm, m)
    tn = min(tn, n)
    return pl.pallas_call(
        functools.partial(_ple_kernel, tn=tn),
        grid=(n // tn, m // tm),
        in_specs=[
            pl.BlockSpec((tm, k), lambda j, i: (i, 0)),
            pl.BlockSpec((tm, kp), lambda j, i: (i, 0)),
            pl.BlockSpec((kp, tn), lambda j, i: (0, j)),
            pl.BlockSpec((tm, tn), lambda j, i: (i, j)),
            pl.BlockSpec(memory_space=pl.ANY),
        ],
        out_specs=pl.BlockSpec((tm, tn), lambda j, i: (i, j)),
        out_shape=jax.ShapeDtypeStruct((m, n), F32),
        scratch_shapes=[pltpu.VMEM((k, tn), F32), pltpu.VMEM((k, tn), BF16), pltpu.VMEM((kp, tn), BF16),
                        pltpu.SemaphoreType.DMA],
        compiler_params=_params("arbitrary", "arbitrary"),
        name="ple_gate_residual",
    )(h, p, w_proj, x, w_gate)


EXPERT_TILE_ROWS = 640

NORM_ROWS = 512
QKV_TILE_ROWS = 512
CONV_PROJ_TILE = (1024, 256)
GATE_PROJ_TILE = (1024, 1024)
MERGE_TILE = (1024, 1024)
OUT_PROJ_TILE = (512, 1024)
PLE_TILE = (512, 1024)
ROUTER_ROWS = 512
RANK_ROWS = 1024
FFN_UP_COLS = 256
FFN_DOWN_COLS = 1024
COMBINE_ROWS = 256


def _rope_tables(s):
    half = HEAD_DIM // 2
    inv_freq = ROPE_THETA ** (-jnp.arange(half, dtype=F32) / half)
    ang = jnp.arange(s, dtype=jnp.int32).astype(F32)[:, None] * inv_freq[None, :]
    cos, sin = jnp.cos(ang), jnp.sin(ang)
    return jnp.concatenate([cos, cos], axis=-1), jnp.concatenate([-sin, sin], axis=-1)


def _expert_tiles(counts, n_assign, tm):
    n_experts = counts.shape[0]
    n_tiles = -(-n_assign // tm) + n_experts
    tiles_e = (counts + tm - 1) // tm
    tcum = jnp.cumsum(tiles_e)
    tstart = tcum - tiles_e
    pstart = tstart * tm
    total = tcum[-1]
    t = jnp.arange(n_tiles, dtype=jnp.int32)
    tc = jnp.minimum(t, total - 1)
    te = jnp.minimum(jnp.searchsorted(tcum, tc, side="right"), n_experts - 1).astype(jnp.int32)
    local = tc - tstart[te]
    rows_left = counts[te] - local * tm
    nsub = jnp.clip((rows_left + ROW_BLOCK - 1) // ROW_BLOCK, 0, tm // ROW_BLOCK)
    nsub = jnp.where(t < total, nsub, 0).astype(jnp.int32)
    trb = (tstart[te] + local).astype(jnp.int32)
    meta = (te, trb, nsub, jnp.reshape(total, (1,)).astype(jnp.int32))
    return pstart.astype(jnp.int32), meta, n_tiles


def _layer(x, p, w_in, w_conv, w_up_a, w_out_b, w_o, g_mix, g_ffn, w_group, b_group, w_router, b_router,
           w_gate, w_up, w_down, g_ple, w_ple_gate, w_ple_proj):
    s, d = x.shape
    cw = w_conv.shape[1]
    n_experts = w_gate.shape[0]
    ng = len(ATTN_DILATIONS)
    a_out = HEADS_PER_GROUP * HEAD_DIM
    q_cols = ng * a_out
    qkv_cols = q_cols + 2 * a_out
    col_u, col_gb, col_gc = qkv_cols, qkv_cols + cw, qkv_cols + 2 * cw
    col_gates = qkv_cols + 3 * cw

    h = _rmsnorm(x, g_mix, BF16, NORM_ROWS)
    rope_c, rope_s = _rope_tables(s)
    qkv = _qkv_proj(h, w_in, rope_c, rope_s, qkv_cols, q_cols + a_out, QKV_TILE_ROWS, qkv_cols // 2)
    y_a = _dilated_attention(qkv)
    y_b = _conv_proj(h, w_in, w_conv, col_u, col_gb, col_gc, *CONV_PROJ_TILE)
    gates = _mm_cols(h, w_in, col_gates, *GATE_PROJ_TILE)
    merged = _merge(y_a, y_b, gates, w_up_a, w_out_b, 0, d, *MERGE_TILE)
    x1 = _resid_mm(merged, w_o, x, *OUT_PROJ_TILE)

    n_route = N_EXPERT_GROUPS + n_experts
    w_route = jnp.pad(jnp.concatenate([w_group, w_router], axis=1), ((0, 0), (0, LANES - n_route)))
    b_route = jnp.pad(jnp.concatenate([b_group, b_router]), (0, LANES - n_route)).reshape(1, LANES)
    h2, eid, wts = _router(x1, g_ffn, w_route, b_route, N_EXPERT_GROUPS, n_experts, ROUTER_ROWS)
    rank, cnt = _rank(eid, RANK_ROWS)
    counts = cnt[0, :n_experts]
    tm_e = EXPERT_TILE_ROWS
    pstart, meta, n_tiles = _expert_tiles(counts, s * TOP_K, tm_e)
    dest = _dest_rows(eid, rank, pstart, RANK_ROWS)[:, :TOP_K].reshape(-1)
    tokens = jnp.repeat(jnp.arange(s, dtype=jnp.int32), TOP_K)
    row_tok = jnp.zeros((n_tiles * tm_e,), jnp.int32).at[dest].set(tokens, unique_indices=True)
    down_cols = min(FFN_DOWN_COLS, d)
    y = _expert_ffn(h2, row_tok, w_gate, w_up, w_down, meta, n_tiles, tm_e, FFN_UP_COLS, down_cols)
    x2, h3 = _combine(x1, y, dest, wts, g_ple, COMBINE_ROWS, down_cols)

    return _ple(h3, p, w_ple_gate, w_ple_proj, x2, *PLE_TILE)


def kernel(x, p, w_in, w_conv, w_up_a, w_out_b, w_o, norm_mix, norm_ffn, w_group, b_group, w_router, b_router,
           w_gate, w_up, w_down, norm_ple, w_ple_gate, w_ple_proj, norm_final):
    b, s, d = x.shape
    depth = w_in.shape[0]
    outs = []
    for bi in range(b):
        xb = x[bi]
        for i in range(depth):
            xb = _layer(xb, p[i, bi], w_in[i], w_conv[i], w_up_a[i], w_out_b[i], w_o[i], norm_mix[i], norm_ffn[i],
                        w_group[i], b_group[i], w_router[i], b_router[i], w_gate[i], w_up[i], w_down[i],
                        norm_ple[i], w_ple_gate[i], w_ple_proj[i])
        outs.append(_rmsnorm(xb, norm_final, x.dtype, NORM_ROWS))
    return jnp.stack(outs)
```
